```python
import math
import jax, jax.numpy as jnp
from jax import lax
import numpy as np

D_MODEL = 2048
BATCH = 4
SEQ = 4096
DEPTH = 2

F32 = jnp.float32
EPS = 1e-6
PLE_DIM = 256

GDN_HEADS = 4
GDN_HEAD_DIM = 128
GDN_WIDTH = GDN_HEADS * GDN_HEAD_DIM
GDN_CONV = 4
GDN_CHUNK = 64

SSD_HEADS = 16
SSD_HEAD_DIM = 64
SSD_WIDTH = SSD_HEADS * SSD_HEAD_DIM
SSD_GROUPS = 2
SSD_STATE = 128
SSD_CONV = 4
SSD_CHUNK = 256
SSD_CONV_DIM = SSD_WIDTH + 2 * SSD_GROUPS * SSD_STATE

MLA_HEADS = 4
MLA_Q_RANK = 384
MLA_KV_RANK = 256
MLA_NOPE_DIM = 128
MLA_ROPE_DIM = 64
MLA_QK_DIM = MLA_NOPE_DIM + MLA_ROPE_DIM
MLA_V_DIM = 128
MLA_WIDTH = MLA_HEADS * MLA_V_DIM
MLA_BLOCK = 128
ROPE_THETA = 10000.0

D_MIX = GDN_WIDTH + SSD_WIDTH + MLA_WIDTH

GDN_COLS = 4 * GDN_WIDTH + 2 * GDN_HEADS
SSD_COLS = SSD_WIDTH + SSD_CONV_DIM + SSD_HEADS
MLA_COLS = MLA_Q_RANK + MLA_KV_RANK + MLA_ROPE_DIM
D_IN = GDN_COLS + SSD_COLS + MLA_COLS

D_FF_DENSE = 5632
N_EXPERTS = 8
TOP_K = 2
D_FF_EXPERT = 7168
N_DENSE = (DEPTH + 1) // 2
N_MOE = DEPTH // 2

kernel_name = 'hybrid_gdn_ssd_mla_moe_block'


def rmsnorm(x, gain):
    xf = x.astype(F32)
    y = xf * lax.rsqrt(jnp.mean(xf * xf, axis=-1, keepdims=True) + EPS)
    return (y * gain.astype(F32)).astype(x.dtype)


def l2norm(x):
    xf = x.astype(F32)
    return xf * lax.rsqrt(jnp.sum(xf * xf, axis=-1, keepdims=True) + EPS)


def causal_conv(x, w):
    k_w = w.shape[-1]
    seq = x.shape[1]
    xp = jnp.pad(x, ((0, 0), (k_w - 1, 0), (0, 0)))
    return sum(xp[:, j:j + seq, :] * w[:, j] for j in range(k_w))


def rotate(x, cos, sin):
    x1, x2 = jnp.split(x, 2, axis=-1)
    return jnp.concatenate([x1 * cos - x2 * sin, x2 * cos + x1 * sin], axis=-1)


def gated_deltanet(cols, conv_w, a_log, dt_bias, norm_g):
    bsz, seq, _ = cols.shape
    H, Dh, C = GDN_HEADS, GDN_HEAD_DIM, GDN_CHUNK
    n_chunks = seq // C
    qkv, z, b, a = jnp.split(cols, [3 * GDN_WIDTH, 4 * GDN_WIDTH, 4 * GDN_WIDTH + H], axis=-1)
    qkv = jax.nn.silu(causal_conv(qkv, conv_w))
    q, k, v = jnp.split(qkv, 3, axis=-1)
    q = l2norm(q.reshape(bsz, seq, H, Dh)) * (Dh ** -0.5)
    k = l2norm(k.reshape(bsz, seq, H, Dh))
    v = v.reshape(bsz, seq, H, Dh).astype(F32)
    beta = jax.nn.sigmoid(b.astype(F32))
    g = -jnp.exp(a_log.astype(F32)) * jax.nn.softplus(a.astype(F32) + dt_bias.astype(F32))

    def to_chunks(t):
        t = t.reshape((bsz, n_chunks, C, H) + t.shape[3:])
        return jnp.moveaxis(t, 3, 1)

    q, k, v, beta, g = (to_chunks(t) for t in (q, k, v, beta, g))
    g = jnp.cumsum(g, axis=-1)
    causal = jnp.tril(jnp.ones((C, C), bool))
    strict = jnp.tril(jnp.ones((C, C), bool), -1)
    decay = jnp.exp(jnp.where(causal, g[..., :, None] - g[..., None, :], -jnp.inf))
    kb = k * beta[..., None]
    m = jnp.where(strict, jnp.einsum('bhncd,bhnsd->bhncs', kb, k) * decay, 0.0)
    rhs = jnp.concatenate([v * beta[..., None], kb * jnp.exp(g)[..., None]], axis=-1)
    sol = lax.linalg.triangular_solve(m + jnp.eye(C, dtype=F32), rhs, left_side=True,
                                      lower=True, unit_diagonal=True)
    u_c, w_c = jnp.split(sol, 2, axis=-1)
    intra = jnp.einsum('bhncd,bhnsd->bhncs', q, k) * decay
    q_dec = q * jnp.exp(g)[..., None]
    k_dec = k * jnp.exp(g[..., -1:] - g)[..., None]
    g_end = jnp.exp(g[..., -1])

    def step(state, xs):
        u_i, w_i, intra_i, qd_i, kd_i, ge_i = xs
        v_new = u_i - jnp.einsum('bhcd,bhde->bhce', w_i, state)
        o_i = jnp.einsum('bhcd,bhde->bhce', qd_i, state) + jnp.einsum('bhcs,bhse->bhce', intra_i, v_new)
        state = state * ge_i[..., None, None] + jnp.einsum('bhcd,bhce->bhde', kd_i, v_new)
        return state, o_i

    xs = tuple(jnp.moveaxis(t, 2, 0) for t in (u_c, w_c, intra, q_dec, k_dec, g_end))
    _, o = lax.scan(step, jnp.zeros((bsz, H, Dh, Dh), F32), xs)
    o = jnp.transpose(o, (1, 0, 3, 2, 4)).reshape(bsz, seq, H, Dh)
    o = rmsnorm(o, norm_g) * jax.nn.silu(z.reshape(bsz, seq, H, Dh).astype(F32))
    return o.reshape(bsz, seq, GDN_WIDTH)


def mamba2_ssd(cols, conv_w, conv_b, a_log, dt_bias, d_skip, norm_g):
    bsz, seq, _ = cols.shape
    H, P, G, N, C = SSD_HEADS, SSD_HEAD_DIM, SSD_GROUPS, SSD_STATE, SSD_CHUNK
    E = H // G
    n_chunks = seq // C
    z, xbc, dt = jnp.split(cols, [SSD_WIDTH, SSD_WIDTH + SSD_CONV_DIM], axis=-1)
    xbc = jax.nn.silu(causal_conv(xbc, conv_w) + conv_b)
    xs, bm, cm = jnp.split(xbc, [SSD_WIDTH, SSD_WIDTH + G * N], axis=-1)
    x = xs.astype(F32).reshape(bsz, n_chunks, C, G, E, P)
    bm = bm.astype(F32).reshape(bsz, n_chunks, C, G, N)
    cm = cm.astype(F32).reshape(bsz, n_chunks, C, G, N)
    dt = jax.nn.softplus(dt.astype(F32) + dt_bias.astype(F32)).reshape(bsz, n_chunks, C, G, E)
    x_dt = x * dt[..., None]
    da = jnp.transpose(dt * (-jnp.exp(a_log.astype(F32))).reshape(G, E), (0, 3, 4, 1, 2))
    a_cs = jnp.cumsum(da, axis=-1)
    causal = jnp.tril(jnp.ones((C, C), bool))
    seg_decay = jnp.exp(jnp.where(causal, a_cs[..., :, None] - a_cs[..., None, :], -jnp.inf))
    cb = jnp.einsum('bclgn,bcsgn->bgcls', cm, bm)
    y_diag = jnp.einsum('bgcls,bgecls,bcsgep->bclgep', cb, seg_decay, x_dt)
    states = jnp.einsum('bclgn,bgecl,bclgep->bcgepn', bm, jnp.exp(a_cs[..., -1:] - a_cs), x_dt)

    def step(h, inp):
        s_i, d_i = inp
        return h * d_i[..., None, None] + s_i, h

    _, prev = lax.scan(step, jnp.zeros((bsz, G, E, P, N), F32),
                       (jnp.moveaxis(states, 1, 0), jnp.moveaxis(jnp.exp(a_cs[..., -1]), 3, 0)))
    y_off = jnp.einsum('bclgn,cbgepn,bgecl->bclgep', cm, prev, jnp.exp(a_cs))
    y = (y_diag + y_off + x * d_skip.astype(F32).reshape(G, E, 1)).reshape(bsz, seq, SSD_WIDTH)
    y = y * jax.nn.silu(z.astype(F32))
    y = rmsnorm(y.reshape(bsz, seq, G, SSD_WIDTH // G), norm_g.reshape(G, SSD_WIDTH // G))
    return y.reshape(bsz, seq, SSD_WIDTH)


def mla(cols, cos, sin, q_a_g, w_q_b, kv_a_g, w_kv_b, q_norm_g, k_norm_g):
    bsz, seq, _ = cols.shape
    H = MLA_HEADS
    q_lat, kv_lat, k_pe = jnp.split(cols, [MLA_Q_RANK, MLA_Q_RANK + MLA_KV_RANK], axis=-1)
    q = (rmsnorm(q_lat, q_a_g) @ w_q_b).reshape(bsz, seq, H, MLA_QK_DIM)
    kv = (rmsnorm(kv_lat, kv_a_g) @ w_kv_b).reshape(bsz, seq, H, MLA_NOPE_DIM + MLA_V_DIM)
    k_nope, v = jnp.split(kv, [MLA_NOPE_DIM], axis=-1)
    k = jnp.concatenate([k_nope, jnp.broadcast_to(k_pe[:, :, None, :], (bsz, seq, H, MLA_ROPE_DIM))], axis=-1)
    q = rmsnorm(q, q_norm_g)
    k = rmsnorm(k, k_norm_g)
    q = jnp.concatenate([q[..., :MLA_NOPE_DIM], rotate(q[..., MLA_NOPE_DIM:], cos, sin)], axis=-1)
    k = jnp.concatenate([k[..., :MLA_NOPE_DIM], rotate(k[..., MLA_NOPE_DIM:], cos, sin)], axis=-1)
    n_blocks = seq // MLA_BLOCK
    qb = jnp.transpose(q.reshape(bsz, n_blocks, MLA_BLOCK, H, MLA_QK_DIM), (1, 0, 3, 2, 4))
    kt = jnp.transpose(k, (0, 2, 1, 3))
    vt = jnp.transpose(v, (0, 2, 1, 3))
    k_idx = jnp.arange(seq)
    scale = MLA_QK_DIM ** -0.5

    def attend(args):
        q_blk, blk = args
        q_idx = blk * MLA_BLOCK + jnp.arange(MLA_BLOCK)
        s = jnp.einsum('bhqd,bhkd->bhqk', q_blk, kt).astype(F32) * scale
        s = jnp.where(q_idx[:, None] >= k_idx[None, :], s, -jnp.inf)
        prob = jax.nn.softmax(s, axis=-1)
        return jnp.einsum('bhqk,bhkd->bhqd', prob.astype(vt.dtype), vt)

    o = lax.map(attend, (qb, jnp.arange(n_blocks)))
    return jnp.transpose(o, (1, 0, 3, 2, 4)).reshape(bsz, seq, MLA_WIDTH)


def swiglu(x, w_gate, w_up, w_down):
    return (jax.nn.silu(x @ w_gate) * (x @ w_up)) @ w_down


def moe(x, w_router, w_gate, w_up, w_down):
    bsz, seq, d = x.shape
    xf = x.reshape(-1, d)
    logits = (xf @ w_router).astype(F32)
    top_val, top_idx = lax.top_k(logits, TOP_K)
    top_w = jax.nn.softmax(top_val, axis=-1)
    gates = jnp.sum(jax.nn.one_hot(top_idx, N_EXPERTS, dtype=F32) * top_w[..., None], axis=1)
    out = jnp.zeros(xf.shape, F32)
    for e in range(N_EXPERTS):
        out = out + gates[:, e, None] * swiglu(xf, w_gate[e], w_up[e], w_down[e])
    return out.reshape(bsz, seq, d).astype(x.dtype)


def setup_inputs(seed: int = 0) -> dict:
    key = jax.random.key(seed)
    ks = iter(jax.random.split(key, 48))

    def nrm(shape, scale):
        return jax.random.normal(next(ks), shape, F32) * scale

    def gain(shape):
        return 1.0 + nrm(shape, 0.02)

    def a_log(shape):
        return jnp.log(jax.random.uniform(next(ks), shape, F32, 1.0, 16.0))

    def dt_bias(shape):
        dt = jnp.exp(jax.random.uniform(next(ks), shape, F32, math.log(1e-3), math.log(1e-1)))
        return dt + jnp.log(-jnp.expm1(-dt))

    x = nrm((BATCH, SEQ, D_MODEL), 1.0)
    p = nrm((DEPTH, BATCH, SEQ, PLE_DIM), 1.0)
    positions = jnp.arange(SEQ, dtype=jnp.int32)[None, :] + jax.random.randint(next(ks), (BATCH, 1), 0, 1024, jnp.int32)
    return {
        'x': x, 'p': p, 'positions': positions,
        'norm_mix_g': gain((DEPTH, D_MODEL)),
        'w_in': nrm((DEPTH, D_MODEL, D_IN), D_MODEL ** -0.5),
        'w_out': nrm((DEPTH, D_MIX, D_MODEL), D_MIX ** -0.5),
        'gdn_conv_w': nrm((DEPTH, 3 * GDN_WIDTH, GDN_CONV), GDN_CONV ** -0.5),
        'gdn_a_log': a_log((DEPTH, GDN_HEADS)),
        'gdn_dt_bias': dt_bias((DEPTH, GDN_HEADS)),
        'gdn_norm_g': gain((DEPTH, GDN_HEAD_DIM)),
        'ssd_conv_w': nrm((DEPTH, SSD_CONV_DIM, SSD_CONV), SSD_CONV ** -0.5),
        'ssd_conv_b': nrm((DEPTH, SSD_CONV_DIM), 0.02),
        'ssd_a_log': a_log((DEPTH, SSD_HEADS)),
        'ssd_dt_bias': dt_bias((DEPTH, SSD_HEADS)),
        'ssd_d': gain((DEPTH, SSD_HEADS)),
        'ssd_norm_g': gain((DEPTH, SSD_WIDTH)),
        'mla_q_a_g': gain((DEPTH, MLA_Q_RANK)),
        'mla_w_q_b': nrm((DEPTH, MLA_Q_RANK, MLA_HEADS * MLA_QK_DIM), MLA_Q_RANK ** -0.5),
        'mla_kv_a_g': gain((DEPTH, MLA_KV_RANK)),
        'mla_w_kv_b': nrm((DEPTH, MLA_KV_RANK, MLA_HEADS * (MLA_NOPE_DIM + MLA_V_DIM)), MLA_KV_RANK ** -0.5),
        'mla_q_norm_g': gain((DEPTH, MLA_QK_DIM)),
        'mla_k_norm_g': gain((DEPTH, MLA_QK_DIM)),
        'norm_ffn_g': gain((DEPTH, D_MODEL)),
        'ffn_w_gate': nrm((N_DENSE, D_MODEL, D_FF_DENSE), D_MODEL ** -0.5),
        'ffn_w_up': nrm((N_DENSE, D_MODEL, D_FF_DENSE), D_MODEL ** -0.5),
        'ffn_w_down': nrm((N_DENSE, D_FF_DENSE, D_MODEL), D_FF_DENSE ** -0.5),
        'router_w': nrm((N_MOE, D_MODEL, N_EXPERTS), D_MODEL ** -0.5),
        'moe_w_gate': nrm((N_MOE, N_EXPERTS, D_MODEL, D_FF_EXPERT), D_MODEL ** -0.5),
        'moe_w_up': nrm((N_MOE, N_EXPERTS, D_MODEL, D_FF_EXPERT), D_MODEL ** -0.5),
        'moe_w_down': nrm((N_MOE, N_EXPERTS, D_FF_EXPERT, D_MODEL), D_FF_EXPERT ** -0.5),
        'ple_w_proj': nrm((DEPTH, PLE_DIM, D_MODEL), PLE_DIM ** -0.5),
        'ple_w_gate': nrm((DEPTH, D_MODEL, D_MODEL), D_MODEL ** -0.5),
        'ple_norm_g': gain((DEPTH, D_MODEL)),
    }


def reference(x, p, positions, norm_mix_g, w_in, w_out, gdn_conv_w, gdn_a_log, gdn_dt_bias, gdn_norm_g,
              ssd_conv_w, ssd_conv_b, ssd_a_log, ssd_dt_bias, ssd_d, ssd_norm_g,
              mla_q_a_g, mla_w_q_b, mla_kv_a_g, mla_w_kv_b, mla_q_norm_g, mla_k_norm_g,
              norm_ffn_g, ffn_w_gate, ffn_w_up, ffn_w_down, router_w, moe_w_gate, moe_w_up, moe_w_down,
              ple_w_proj, ple_w_gate, ple_norm_g):
    half = MLA_ROPE_DIM // 2
    inv_freq = 1.0 / (ROPE_THETA ** (jnp.arange(half, dtype=F32) / half))
    ang = positions.astype(F32)[..., None] * inv_freq
    cos = jnp.cos(ang)[:, :, None, :]
    sin = jnp.sin(ang)[:, :, None, :]
    h = x
    for i in range(DEPTH):
        u = rmsnorm(h, norm_mix_g[i]) @ w_in[i]
        u_gdn, u_ssd, u_mla = jnp.split(u, [GDN_COLS, GDN_COLS + SSD_COLS], axis=-1)
        mix = jnp.concatenate([
            gated_deltanet(u_gdn, gdn_conv_w[i], gdn_a_log[i], gdn_dt_bias[i], gdn_norm_g[i]),
            mamba2_ssd(u_ssd, ssd_conv_w[i], ssd_conv_b[i], ssd_a_log[i], ssd_dt_bias[i], ssd_d[i], ssd_norm_g[i]),
            mla(u_mla, cos, sin, mla_q_a_g[i], mla_w_q_b[i], mla_kv_a_g[i], mla_w_kv_b[i],
                mla_q_norm_g[i], mla_k_norm_g[i]),
        ], axis=-1).astype(h.dtype)
        h = h + mix @ w_out[i]
        hn = rmsnorm(h, norm_ffn_g[i])
        if i % 2 == 0:
            j = i // 2
            h = h + swiglu(hn, ffn_w_gate[j], ffn_w_up[j], ffn_w_down[j])
        else:
            j = i // 2
            h = h + moe(hn, router_w[j], moe_w_gate[j], moe_w_up[j], moe_w_down[j])
        e = rmsnorm(p[i] @ ple_w_proj[i], ple_norm_g[i])
        h = h + (jax.nn.sigmoid(h @ ple_w_gate[i]) * e).astype(h.dtype)
    return h
```

```python
import functools

import jax
import jax.numpy as jnp
from jax import lax
from jax.experimental import pallas as pl
from jax.experimental.pallas import tpu as pltpu

F32 = jnp.float32
BF16 = jnp.bfloat16
U32 = jnp.uint32
I32 = jnp.int32
EPS = 1e-6

D_MODEL = 2048
GDN_HEADS, GDN_DH, GDN_CHUNK = 4, 128, 64
GDN_WIDTH = GDN_HEADS * GDN_DH
SSD_HEADS, SSD_P, SSD_GROUPS, SSD_N, SSD_CHUNK = 16, 64, 2, 128, 256
SSD_WIDTH = SSD_HEADS * SSD_P
SSD_CONV_DIM = SSD_WIDTH + 2 * SSD_GROUPS * SSD_N
MLA_HEADS, MLA_Q_RANK, MLA_KV_RANK = 4, 384, 256
MLA_NOPE, MLA_ROPE, MLA_V = 128, 64, 128
MLA_QK = MLA_NOPE + MLA_ROPE
ROPE_THETA = 10000.0
N_EXPERTS = 8
LANES = 128

U_COLS = 5632
COL_GDN_QKV, COL_GDN_Z, COL_SSD_Z, COL_SSD_XBC = 0, 1536, 2048, 3072
COL_QLAT, COL_KPE, COL_KVLAT, COL_SMALL = 4608, 4992, 5120, 5376
LANE_GDN_B, LANE_GDN_A, LANE_SSD_DT = 64, 68, 72

SEQ_BLOCK = 256
VMEM_LIMIT_BYTES = 56 * 1024 * 1024
MOE_TM = 512
GATHER_ROWS = 256
COMBINE_ROWS = 128


def _cp(*sem):
    return pltpu.CompilerParams(dimension_semantics=sem, vmem_limit_bytes=VMEM_LIMIT_BYTES)


def _rms(x, gain):
    return x * lax.rsqrt(jnp.mean(x * x, axis=-1, keepdims=True) + EPS) * gain


def _silu(x):
    return x * jax.nn.sigmoid(x)


def _bdot(a, b):
    return jnp.dot(a.astype(BF16), b.astype(BF16), preferred_element_type=F32)


def _bdot_nt(a, b):
    return lax.dot_general(a.astype(BF16), b.astype(BF16), (((1,), (1,)), ((), ())), preferred_element_type=F32)


def _bdot_tn(a, b):
    return lax.dot_general(a.astype(BF16), b.astype(BF16), (((0,), (0,)), ((), ())), preferred_element_type=F32)


def _split3(x):
    x1 = x.astype(BF16)
    r1 = x - x1.astype(F32)
    x2 = r1.astype(BF16)
    x3 = (r1 - x2.astype(F32)).astype(BF16)
    return x1, x2, x3


def _dot_exact_lhs(m01, x):
    x1, x2, x3 = _split3(x)
    d = functools.partial(jnp.dot, preferred_element_type=F32)
    return d(m01, x1) + d(m01, x2) + d(m01, x3)


def _dot_x3(a, b):
    ah = a.astype(BF16)
    al = (a - ah.astype(F32)).astype(BF16)
    bh = b.astype(BF16)
    bl = (b - bh.astype(F32)).astype(BF16)
    d = functools.partial(jnp.dot, preferred_element_type=F32)
    return d(ah, bh) + d(ah, bl) + d(al, bh)


def _mm_kernel(*refs, norm, cast, has_res):
    it = iter(refs)
    x_ref = next(it)
    g_ref = next(it) if norm else None
    w_ref = next(it)
    res_ref = next(it) if has_res else None
    o_ref = next(it)
    if cast:
        xs_ref = next(it)

        @pl.when(pl.program_id(1) == 0)
        def _():
            x = x_ref[...].astype(F32)
            if norm:
                x = _rms(x, g_ref[...])
            xs_ref[...] = x.astype(BF16)

        a = xs_ref[...]
    else:
        a = x_ref[...]
    acc = jnp.dot(a, w_ref[...], preferred_element_type=F32)
    if has_res:
        acc = acc + res_ref[...]
    o_ref[...] = acc.astype(o_ref.dtype)


def _mm(x, w, *, gain=None, res=None, out_dtype=F32, tm, tn, name):
    m, k = x.shape
    n = w.shape[1]
    tm, tn = min(tm, m), min(tn, n)
    norm = gain is not None
    cast = norm or x.dtype != BF16
    in_specs = [pl.BlockSpec((tm, k), lambda i, j: (i, 0))]
    args = [x]
    if norm:
        in_specs.append(pl.BlockSpec((1, k), lambda i, j: (0, 0)))
        args.append(gain.reshape(1, k).astype(F32))
    in_specs.append(pl.BlockSpec((k, tn), lambda i, j: (0, j)))
    args.append(w)
    if res is not None:
        in_specs.append(pl.BlockSpec((tm, tn), lambda i, j: (i, j)))
        args.append(res)
    return pl.pallas_call(
        functools.partial(_mm_kernel, norm=norm, cast=cast, has_res=res is not None),
        grid=(m // tm, n // tn),
        in_specs=in_specs,
        out_specs=pl.BlockSpec((tm, tn), lambda i, j: (i, j)),
        out_shape=jax.ShapeDtypeStruct((m, n), out_dtype),
        scratch_shapes=[pltpu.VMEM((tm, k), BF16)] if cast else [],
        compiler_params=_cp("parallel", "arbitrary"),
        name=name,
    )(*args)


def _swiglu_up_kernel(x_ref, g_ref, wg_ref, wu_ref, o_ref, xs_ref):
    @pl.when(pl.program_id(1) == 0)
    def _():
        xs_ref[...] = _rms(x_ref[...], g_ref[...]).astype(BF16)

    a = xs_ref[...]
    gate = jnp.dot(a, wg_ref[...], preferred_element_type=F32)
    up = jnp.dot(a, wu_ref[...], preferred_element_type=F32)
    o_ref[...] = (_silu(gate) * up).astype(o_ref.dtype)


def _swiglu_up(x, gain, wg, wu, *, tm, tn):
    m, k = x.shape
    n = wg.shape[1]
    tm, tn = min(tm, m), min(tn, n)
    return pl.pallas_call(
        _swiglu_up_kernel,
        grid=(m // tm, n // tn),
        in_specs=[pl.BlockSpec((tm, k), lambda i, j: (i, 0)),
                  pl.BlockSpec((1, k), lambda i, j: (0, 0)),
                  pl.BlockSpec((k, tn), lambda i, j: (0, j)),
                  pl.BlockSpec((k, tn), lambda i, j: (0, j))],
        out_specs=pl.BlockSpec((tm, tn), lambda i, j: (i, j)),
        out_shape=jax.ShapeDtypeStruct((m, n), BF16),
        scratch_shapes=[pltpu.VMEM((tm, k), BF16)],
        compiler_params=_cp("parallel", "arbitrary"),
        name="swiglu_up",
    )(x, gain.reshape(1, k).astype(F32), wg, wu)


def _ple_kernel(h_ref, p_ref, wp_ref, g_ref, wg_ref, o_ref):
    h = h_ref[...]
    e = jnp.dot(p_ref[...].astype(BF16), wp_ref[...], preferred_element_type=F32)
    e = _rms(e, g_ref[...])
    gate = jax.nn.sigmoid(jnp.dot(h.astype(BF16), wg_ref[...], preferred_element_type=F32))
    o_ref[...] = h + gate * e


def _ple(h, p, wp, gain, wg, *, tm):
    m, d = h.shape
    dp = p.shape[1]
    tm = min(tm, m)
    return pl.pallas_call(
        _ple_kernel,
        grid=(m // tm,),
        in_specs=[pl.BlockSpec((tm, d), lambda i: (i, 0)),
                  pl.BlockSpec((tm, dp), lambda i: (i, 0)),
                  pl.BlockSpec((dp, d), lambda i: (0, 0)),
                  pl.BlockSpec((1, d), lambda i: (0, 0)),
                  pl.BlockSpec((d, d), lambda i: (0, 0))],
        out_specs=pl.BlockSpec((tm, d), lambda i: (i, 0)),
        out_shape=jax.ShapeDtypeStruct((m, d), F32),
        compiler_params=_cp("parallel"),
        name="ple",
    )(h, p, wp, gain.reshape(1, d).astype(F32), wg)


def _conv_silu_slab(x, first, tail_ref, xbuf_ref, cw, bias):
    rows = x.shape[0]

    @pl.when(first)
    def _():
        tail_ref[...] = jnp.zeros_like(tail_ref)

    xbuf_ref[0:8, :] = tail_ref[...]
    xbuf_ref[8:8 + rows, :] = x
    tail_ref[...] = x[rows - 8:rows, :]
    y = (xbuf_ref[5:5 + rows, :] * cw[0:1, :] + xbuf_ref[6:6 + rows, :] * cw[1:2, :]
         + xbuf_ref[7:7 + rows, :] * cw[2:3, :] + x * cw[3:4, :])
    if bias is not None:
        y = y + bias
    return _silu(y)


def _neumann_inverse(m, chunk):
    n = m.shape[0]
    row = lax.broadcasted_iota(I32, (n, n), 0)
    col = lax.broadcasted_iota(I32, (n, n), 1)
    x = jnp.where(row == col, 1.0, 0.0).astype(F32) - m
    p = m
    span = 2
    while span < chunk:
        p = _dot_x3(p, p)
        x = x + _dot_x3(x, p)
        span *= 2
    return x


def _gdn_kernel(qkv_ref, z_ref, sm_ref, cw_ref, vec_ref, ng_ref, o_ref, state_ref, tail_ref, xbuf_ref, xc_ref):
    rows = SEQ_BLOCK
    c = GDN_CHUNK
    first = pl.program_id(1) == 0

    @pl.when(first)
    def _():
        state_ref[...] = jnp.zeros_like(state_ref)

    for s in range(3):
        sl = slice(s * GDN_WIDTH, (s + 1) * GDN_WIDTH)
        xc_ref[:, sl] = _conv_silu_slab(qkv_ref[0, :, sl], first, tail_ref.at[s], xbuf_ref, cw_ref[:, sl], None)

    sm = sm_ref[0]
    beta_all = jax.nn.sigmoid(sm)
    g_all = vec_ref[0:1, :] * jax.nn.softplus(sm + vec_ref[1:2, :])

    row = lax.broadcasted_iota(I32, (rows, rows), 0)
    col = lax.broadcasted_iota(I32, (rows, rows), 1)
    same = (row // c) == (col // c)
    causal = same & (col <= row)
    strict = same & (col < row)
    tri01 = jnp.where(causal, 1.0, 0.0).astype(BF16)
    blk01 = jnp.where(same, 1.0, 0.0).astype(BF16)

    for h in range(GDN_HEADS):
        hs = slice(h * GDN_DH, (h + 1) * GDN_DH)
        q = xc_ref[:, hs]
        k = xc_ref[:, GDN_WIDTH + h * GDN_DH:GDN_WIDTH + (h + 1) * GDN_DH]
        v = xc_ref[:, 2 * GDN_WIDTH + h * GDN_DH:2 * GDN_WIDTH + (h + 1) * GDN_DH]
        q = q * lax.rsqrt(jnp.sum(q * q, axis=-1, keepdims=True) + EPS) * (GDN_DH ** -0.5)
        k = k * lax.rsqrt(jnp.sum(k * k, axis=-1, keepdims=True) + EPS)
        beta = beta_all[:, LANE_GDN_B + h:LANE_GDN_B + h + 1]
        gb = jnp.broadcast_to(g_all[:, LANE_GDN_A + h:LANE_GDN_A + h + 1], (rows, LANES))
        gc = _dot_exact_lhs(tri01, gb)
        gl = _dot_exact_lhs(blk01, gb)
        gcc = jnp.concatenate([gc, gc], axis=1)
        decay = jnp.where(causal, jnp.exp(gcc - gcc.T), 0.0)
        kb = k * beta
        m = jnp.where(strict, _bdot_nt(kb, k) * decay, 0.0)
        tinv = _neumann_inverse(m, c)
        eg = jnp.exp(gc)
        sol = _dot_x3(tinv, jnp.concatenate([v * beta, kb * eg], axis=1))
        u = sol[:, :GDN_DH]
        w = sol[:, GDN_DH:]
        intra = (_bdot_nt(q, k) * decay).astype(BF16)
        qd = q * eg
        kd = k * jnp.exp(gl - gc)
        ge = jnp.exp(gl)
        st = state_ref[h]
        outs = []
        for ci in range(rows // c):
            rs = slice(ci * c, (ci + 1) * c)
            v_new = u[rs] - _bdot(w[rs], st)
            pieces = []
            if ci > 0:
                pieces.append(jnp.zeros((ci * c, GDN_DH), F32))
            pieces.append(v_new)
            if (ci + 1) * c < rows:
                pieces.append(jnp.zeros((rows - (ci + 1) * c, GDN_DH), F32))
            v_pad = jnp.concatenate(pieces, axis=0) if len(pieces) > 1 else v_new
            outs.append(_bdot(qd[rs], st) + jnp.dot(intra[rs], v_pad.astype(BF16), preferred_element_type=F32))
            st = st * ge[ci * c:ci * c + 1, :] + _bdot_tn(kd[rs], v_new)
        state_ref[h] = st
        o = jnp.concatenate(outs, axis=0)
        o = _rms(o, ng_ref[...]) * _silu(z_ref[0, :, hs])
        o_ref[0, :, hs] = o.astype(o_ref.dtype)


def _gdn(u3, conv_w, a_log, dt_bias, norm_g):
    b, s, _ = u3.shape
    cw = conv_w.T.astype(F32)
    vec = jnp.zeros((2, LANES), F32)
    vec = vec.at[0, LANE_GDN_A:LANE_GDN_A + GDN_HEADS].set(-jnp.exp(a_log.astype(F32)))
    vec = vec.at[1, LANE_GDN_A:LANE_GDN_A + GDN_HEADS].set(dt_bias.astype(F32))
    blk = SEQ_BLOCK
    return pl.pallas_call(
        _gdn_kernel,
        grid=(b, s // blk),
        in_specs=[pl.BlockSpec((1, blk, 3 * GDN_WIDTH), lambda i, j: (i, j, COL_GDN_QKV // (3 * GDN_WIDTH))),
                  pl.BlockSpec((1, blk, GDN_WIDTH), lambda i, j: (i, j, COL_GDN_Z // GDN_WIDTH)),
                  pl.BlockSpec((1, blk, LANES), lambda i, j: (i, j, COL_SMALL // LANES)),
                  pl.BlockSpec((4, 3 * GDN_WIDTH), lambda i, j: (0, 0)),
                  pl.BlockSpec((2, LANES), lambda i, j: (0, 0)),
                  pl.BlockSpec((1, GDN_DH), lambda i, j: (0, 0))],
        out_specs=pl.BlockSpec((1, blk, GDN_WIDTH), lambda i, j: (i, j, 0)),
        out_shape=jax.ShapeDtypeStruct((b, s, GDN_WIDTH), BF16),
        scratch_shapes=[pltpu.VMEM((GDN_HEADS, GDN_DH, GDN_DH), F32),
                        pltpu.VMEM((3, 8, GDN_WIDTH), F32),
                        pltpu.VMEM((blk + 8, GDN_WIDTH), F32),
                        pltpu.VMEM((blk, 3 * GDN_WIDTH), F32)],
        compiler_params=_cp("parallel", "arbitrary"),
        name="gdn",
    )(u3, u3, u3, cw, vec, norm_g.reshape(1, GDN_DH).astype(F32))


def _ssd_kernel(z_ref, xbc_ref, sm_ref, cw_ref, cb_ref, vec_ref, dsk_ref, ng_ref, o_ref,
                state_ref, tail_ref, xbuf_ref, xc_ref, y_ref):
    rows = SEQ_BLOCK
    first = pl.program_id(1) == 0
    half = SSD_P

    @pl.when(first)
    def _():
        state_ref[...] = jnp.zeros_like(state_ref)

    slab = 512
    for s in range(SSD_CONV_DIM // slab):
        sl = slice(s * slab, (s + 1) * slab)
        xc_ref[:, sl] = _conv_silu_slab(xbc_ref[0, :, sl], first, tail_ref.at[s], xbuf_ref, cw_ref[:, sl], cb_ref[:, sl])

    dt_all = jax.nn.softplus(sm_ref[0] + vec_ref[1:2, :])
    da_all = dt_all * vec_ref[0:1, :]
    row = lax.broadcasted_iota(I32, (rows, rows), 0)
    col = lax.broadcasted_iota(I32, (rows, rows), 1)
    causal = col <= row
    tri01 = jnp.where(causal, 1.0, 0.0).astype(BF16)
    acs = _dot_exact_lhs(tri01, da_all)
    acs_t = acs.T
    lane = lax.broadcasted_iota(I32, (rows, LANES), 1)
    lo = lane < half
    lane1 = lax.broadcasted_iota(I32, (1, LANES), 1)
    lo1 = lane1 < half

    for g in range(SSD_GROUPS):
        bm = xc_ref[:, SSD_WIDTH + g * SSD_N:SSD_WIDTH + (g + 1) * SSD_N]
        cm = xc_ref[:, SSD_WIDTH + (SSD_GROUPS + g) * SSD_N:SSD_WIDTH + (SSD_GROUPS + g + 1) * SSD_N]
        cb = _bdot_nt(cm, bm)
        bm_t = bm.T
        cm_b = cm.astype(BF16)
        pairs_per_group = SSD_HEADS // SSD_GROUPS // 2
        for pi in range(pairs_per_group):
            p = g * pairs_per_group + pi
            la = LANE_SSD_DT + 2 * p
            lb = la + 1
            ps = slice(p * LANES, (p + 1) * LANES)
            col_a, col_b = acs[:, la:la + 1], acs[:, lb:lb + 1]
            row_a, row_b = acs_t[la:la + 1, :], acs_t[lb:lb + 1, :]
            seg_a = jnp.where(causal, jnp.exp(col_a - row_a), 0.0)
            seg_b = jnp.where(causal, jnp.exp(col_b - row_b), 0.0)
            x_pair = xc_ref[:, ps]
            dt_pair = jnp.where(lo, dt_all[:, la:la + 1], dt_all[:, lb:lb + 1])
            xdt = x_pair * dt_pair
            xdt_a = jnp.where(lo, xdt, 0.0).astype(BF16)
            xdt_b = jnp.where(lo, 0.0, xdt).astype(BF16)
            d = functools.partial(jnp.dot, preferred_element_type=F32)
            y_diag = d((cb * seg_a).astype(BF16), xdt_a) + d((cb * seg_b).astype(BF16), xdt_b)
            last_a, last_b = row_a[:, rows - 1:rows], row_b[:, rows - 1:rows]
            st_new = (d((bm_t * jnp.exp(last_a - row_a)).astype(BF16), xdt_a)
                      + d((bm_t * jnp.exp(last_b - row_b)).astype(BF16), xdt_b))
            prev = state_ref[p]
            y_off = d(cm_b, prev.astype(BF16)) * jnp.where(lo, jnp.exp(col_a), jnp.exp(col_b))
            state_ref[p] = prev * jnp.where(lo1, jnp.exp(last_a), jnp.exp(last_b)) + st_new
            y_ref[:, ps] = y_diag + y_off + x_pair * dsk_ref[:, ps]

    gw = SSD_WIDTH // SSD_GROUPS
    for g in range(SSD_GROUPS):
        gs = slice(g * gw, (g + 1) * gw)
        y = y_ref[:, gs] * _silu(z_ref[0, :, gs])
        o_ref[0, :, gs] = _rms(y, ng_ref[:, gs]).astype(o_ref.dtype)


def _ssd(u3, conv_w, conv_b, a_log, dt_bias, d_skip, norm_g):
    b, s, _ = u3.shape
    cw = conv_w.T.astype(F32)
    vec = jnp.zeros((2, LANES), F32)
    vec = vec.at[0, LANE_SSD_DT:LANE_SSD_DT + SSD_HEADS].set(-jnp.exp(a_log.astype(F32)))
    vec = vec.at[1, LANE_SSD_DT:LANE_SSD_DT + SSD_HEADS].set(dt_bias.astype(F32))
    dsk = jnp.repeat(d_skip.astype(F32), SSD_P).reshape(1, SSD_WIDTH)
    blk = SEQ_BLOCK
    return pl.pallas_call(
        _ssd_kernel,
        grid=(b, s // blk),
        in_specs=[pl.BlockSpec((1, blk, SSD_WIDTH), lambda i, j: (i, j, COL_SSD_Z // SSD_WIDTH)),
                  pl.BlockSpec((1, blk, SSD_CONV_DIM), lambda i, j: (i, j, COL_SSD_XBC // SSD_CONV_DIM)),
                  pl.BlockSpec((1, blk, LANES), lambda i, j: (i, j, COL_SMALL // LANES)),
                  pl.BlockSpec((4, SSD_CONV_DIM), lambda i, j: (0, 0)),
                  pl.BlockSpec((1, SSD_CONV_DIM), lambda i, j: (0, 0)),
                  pl.BlockSpec((2, LANES), lambda i, j: (0, 0)),
                  pl.BlockSpec((1, SSD_WIDTH), lambda i, j: (0, 0)),
                  pl.BlockSpec((1, SSD_WIDTH), lambda i, j: (0, 0))],
        out_specs=pl.BlockSpec((1, blk, SSD_WIDTH), lambda i, j: (i, j, 0)),
        out_shape=jax.ShapeDtypeStruct((b, s, SSD_WIDTH), BF16),
        scratch_shapes=[pltpu.VMEM((SSD_HEADS // 2, SSD_N, 2 * SSD_P), F32),
                        pltpu.VMEM((SSD_CONV_DIM // 512, 8, 512), F32),
                        pltpu.VMEM((blk + 8, 512), F32),
                        pltpu.VMEM((blk, SSD_CONV_DIM), F32),
                        pltpu.VMEM((blk, SSD_WIDTH), F32)],
        compiler_params=_cp("parallel", "arbitrary"),
        name="ssd",
    )(u3, u3, u3, cw, conv_b.reshape(1, SSD_CONV_DIM).astype(F32), vec, dsk,
      norm_g.reshape(1, SSD_WIDTH).astype(F32))


def _mla_pre_kernel(ql_ref, kpe_ref, kvl_ref, pos_ref, qag_ref, kvag_ref, wq_ref, wkv_ref, vec_ref,
                    q_ref, k_ref, v_ref):
    qn = _rms(ql_ref[0], qag_ref[...])
    qall = jnp.dot(qn.astype(BF16), wq_ref[...], preferred_element_type=F32)
    kvn = _rms(kvl_ref[0], kvag_ref[...])
    kv = jnp.dot(kvn.astype(BF16), wkv_ref[...], preferred_element_type=F32)
    kpe = kpe_ref[0]
    ang = pos_ref[0] * vec_ref[6:7, :]
    cos = jnp.cos(ang)
    sin = jnp.sin(ang) * vec_ref[7:8, :]
    gqn, gqa, gqb = vec_ref[0:1, :], vec_ref[1:2, :], vec_ref[2:3, :]
    gkn, gka, gkb = vec_ref[3:4, :], vec_ref[4:5, :], vec_ref[5:6, :]
    scale = MLA_QK ** -0.5
    k_rot = kpe * (cos * gka) + pltpu.roll(kpe, 64, 1) * (sin * gkb)
    k_pe_ss = 0.5 * jnp.sum(kpe * kpe, axis=-1, keepdims=True)
    nh = MLA_HEADS
    for h in range(nh):
        q_nope = qall[:, h * LANES:(h + 1) * LANES]
        q_pe = qall[:, (nh + h) * LANES:(nh + h + 1) * LANES]
        ss = jnp.sum(q_nope * q_nope, axis=-1, keepdims=True) + 0.5 * jnp.sum(q_pe * q_pe, axis=-1, keepdims=True)
        rstd = lax.rsqrt(ss * (1.0 / MLA_QK) + EPS)
        q_rot = q_pe * (cos * gqa) + pltpu.roll(q_pe, 64, 1) * (sin * gqb)
        q_ref[0, h, :, 0:LANES] = (q_nope * gqn * rstd * scale).astype(q_ref.dtype)
        q_ref[0, h, :, LANES:2 * LANES] = (q_rot * rstd * (0.5 * scale)).astype(q_ref.dtype)
        k_nope = kv[:, 2 * h * LANES:(2 * h + 1) * LANES]
        ssk = jnp.sum(k_nope * k_nope, axis=-1, keepdims=True) + k_pe_ss
        rstdk = lax.rsqrt(ssk * (1.0 / MLA_QK) + EPS)
        k_ref[0, h, :, 0:LANES] = (k_nope * gkn * rstdk).astype(k_ref.dtype)
        k_ref[0, h, :, LANES:2 * LANES] = (k_rot * rstdk).astype(k_ref.dtype)
        v_ref[0, h] = kv[:, (2 * h + 1) * LANES:(2 * h + 2) * LANES].astype(v_ref.dtype)


def _rope_pair_gains(g):
    g1, g2 = g[MLA_NOPE:MLA_NOPE + 32], g[MLA_NOPE + 32:MLA_NOPE + 64]
    return jnp.concatenate([g1, g2, g2, g1]), jnp.concatenate([g2, g1, g1, g2])


def _mla_pre(u3, pos_f, q_a_g, w_q_b, kv_a_g, w_kv_b, q_norm_g, k_norm_g, *, ts):
    b, s, _ = u3.shape
    nh = MLA_HEADS
    wq = w_q_b.reshape(MLA_Q_RANK, nh, MLA_QK)
    x1, x2 = wq[:, :, MLA_NOPE:MLA_NOPE + 32], wq[:, :, MLA_NOPE + 32:]
    wq_all = jnp.concatenate([wq[:, :, :MLA_NOPE].reshape(MLA_Q_RANK, nh * MLA_NOPE),
                              jnp.concatenate([x1, x2, x2, x1], axis=-1).reshape(MLA_Q_RANK, nh * LANES)],
                             axis=1).astype(BF16)
    half = MLA_ROPE // 2
    inv_freq = 1.0 / (ROPE_THETA ** (jnp.arange(half, dtype=F32) / half))
    gqa, gqb = _rope_pair_gains(q_norm_g.astype(F32))
    gka, gkb = _rope_pair_gains(k_norm_g.astype(F32))
    ones = jnp.ones((half,), F32)
    vec = jnp.stack([q_norm_g[:MLA_NOPE].astype(F32), gqa, gqb, k_norm_g[:MLA_NOPE].astype(F32), gka, gkb,
                     jnp.tile(inv_freq, 4), jnp.concatenate([-ones, ones, ones, -ones])])
    ts = min(ts, s)
    qk_shape = jax.ShapeDtypeStruct((b, nh, s, 2 * LANES), BF16)
    return pl.pallas_call(
        _mla_pre_kernel,
        grid=(b, s // ts),
        in_specs=[pl.BlockSpec((1, ts, MLA_Q_RANK), lambda i, j: (i, j, COL_QLAT // MLA_Q_RANK)),
                  pl.BlockSpec((1, ts, LANES), lambda i, j: (i, j, COL_KPE // LANES)),
                  pl.BlockSpec((1, ts, MLA_KV_RANK), lambda i, j: (i, j, COL_KVLAT // MLA_KV_RANK)),
                  pl.BlockSpec((1, ts, 1), lambda i, j: (i, j, 0)),
                  pl.BlockSpec((1, MLA_Q_RANK), lambda i, j: (0, 0)),
                  pl.BlockSpec((1, MLA_KV_RANK), lambda i, j: (0, 0)),
                  pl.BlockSpec((MLA_Q_RANK, 2 * nh * LANES), lambda i, j: (0, 0)),
                  pl.BlockSpec((MLA_KV_RANK, 2 * nh * LANES), lambda i, j: (0, 0)),
                  pl.BlockSpec((8, LANES), lambda i, j: (0, 0))],
        out_specs=[pl.BlockSpec((1, nh, ts, 2 * LANES), lambda i, j: (i, 0, j, 0)),
                   pl.BlockSpec((1, nh, ts, 2 * LANES), lambda i, j: (i, 0, j, 0)),
                   pl.BlockSpec((1, nh, ts, LANES), lambda i, j: (i, 0, j, 0))],
        out_shape=[qk_shape, qk_shape, jax.ShapeDtypeStruct((b, nh, s, LANES), BF16)],
        compiler_params=_cp("parallel", "parallel"),
        name="mla_pre",
    )(u3, u3, u3, pos_f, q_a_g.reshape(1, -1).astype(F32), kv_a_g.reshape(1, -1).astype(F32),
      wq_all, w_kv_b.astype(BF16), vec)


def _flash_kernel(q_ref, k_ref, v_ref, o_ref, m_ref, l_ref, acc_ref, *, tq, tk):
    qi = pl.program_id(2)
    kj = pl.program_id(3)

    @pl.when(kj == 0)
    def _():
        m_ref[...] = jnp.full_like(m_ref, -jnp.inf)
        l_ref[...] = jnp.zeros_like(l_ref)
        acc_ref[...] = jnp.zeros_like(acc_ref)

    @pl.when(kj * tk <= qi * tq + (tq - 1))
    def _():
        s = lax.dot_general(q_ref[0, 0], k_ref[0, 0], (((1,), (1,)), ((), ())), preferred_element_type=F32)
        row = qi * tq + lax.broadcasted_iota(I32, (tq, tk), 0)
        col = kj * tk + lax.broadcasted_iota(I32, (tq, tk), 1)
        s = jnp.where(row >= col, s, -jnp.inf)
        m_old = m_ref[...]
        m_new = jnp.maximum(m_old, jnp.max(s, axis=-1, keepdims=True))
        p = jnp.exp(s - m_new)
        alpha = jnp.exp(m_old - m_new)
        l_ref[...] = alpha * l_ref[...] + jnp.sum(p, axis=-1, keepdims=True)
        acc_ref[...] = alpha * acc_ref[...] + jnp.dot(p.astype(BF16), v_ref[0, 0], preferred_element_type=F32)
        m_ref[...] = m_new

    @pl.when(kj == pl.num_programs(3) - 1)
    def _():
        o_ref[0] = (acc_ref[...] / l_ref[...]).astype(o_ref.dtype)


def _flash(q, k, v, *, tq, tk):
    b, nh, s, dq = q.shape
    dv = v.shape[-1]
    tq, tk = min(tq, s), min(tk, s)

    def kv_map(i, h, qi, kj):
        return (i, h, jnp.minimum(kj, (qi * tq + tq - 1) // tk), 0)

    return pl.pallas_call(
        functools.partial(_flash_kernel, tq=tq, tk=tk),
        grid=(b, nh, s // tq, s // tk),
        in_specs=[pl.BlockSpec((1, 1, tq, dq), lambda i, h, qi, kj: (i, h, qi, 0)),
                  pl.BlockSpec((1, 1, tk, dq), kv_map),
                  pl.BlockSpec((1, 1, tk, dv), kv_map)],
        out_specs=pl.BlockSpec((1, tq, dv), lambda i, h, qi, kj: (i, qi, h)),
        out_shape=jax.ShapeDtypeStruct((b, s, nh * dv), BF16),
        scratch_shapes=[pltpu.VMEM((tq, 1), F32), pltpu.VMEM((tq, 1), F32), pltpu.VMEM((tq, dv), F32)],
        compiler_params=_cp("parallel", "parallel", "parallel", "arbitrary"),
        name="mla_flash",
    )(q, k, v)


def _router_kernel(h_ref, g_ref, wr_ref, hp_ref, idx_ref, wt_ref):
    xn = _rms(h_ref[...], g_ref[...])
    logits = jnp.dot(xn, wr_ref[...], precision=lax.Precision.HIGHEST, preferred_element_type=F32)
    lane = lax.broadcasted_iota(I32, logits.shape, 1)
    logits = jnp.where(lane < N_EXPERTS, logits, -jnp.inf)
    m1 = jnp.max(logits, axis=-1, keepdims=True)
    i1 = jnp.min(jnp.where(logits == m1, lane, LANES), axis=-1, keepdims=True)
    rest = jnp.where(lane == i1, -jnp.inf, logits)
    m2 = jnp.max(rest, axis=-1, keepdims=True)
    i2 = jnp.min(jnp.where(rest == m2, lane, LANES), axis=-1, keepdims=True)
    e2 = jnp.exp(m2 - m1)
    w1 = 1.0 / (1.0 + e2)
    w2 = e2 / (1.0 + e2)
    idx_ref[...] = jnp.where(lane == 0, i1, jnp.where(lane == 1, i2, 0))
    wt_ref[...] = jnp.where(lane == 0, w1, jnp.where(lane == 1, w2, 0.0))
    half = xn.shape[1] // 2
    lo = pltpu.bitcast(xn[:, :half].astype(BF16).astype(F32), U32)
    hi = pltpu.bitcast(xn[:, half:].astype(BF16).astype(F32), U32)
    hp_ref[...] = (lo >> 16) | (hi & jnp.uint32(0xFFFF0000))


def _router(h, gain, w_router, *, tm):
    m, d = h.shape
    tm = min(tm, m)
    wr = jnp.zeros((d, LANES), F32).at[:, :N_EXPERTS].set(w_router.astype(F32))
    return pl.pallas_call(
        _router_kernel,
        grid=(m // tm,),
        in_specs=[pl.BlockSpec((tm, d), lambda i: (i, 0)),
                  pl.BlockSpec((1, d), lambda i: (0, 0)),
                  pl.BlockSpec((d, LANES), lambda i: (0, 0))],
        out_specs=[pl.BlockSpec((tm, d // 2), lambda i: (i, 0)),
                   pl.BlockSpec((tm, LANES), lambda i: (i, 0)),
                   pl.BlockSpec((tm, LANES), lambda i: (i, 0))],
        out_shape=[jax.ShapeDtypeStruct((m, d // 2), U32),
                   jax.ShapeDtypeStruct((m, LANES), I32),
                   jax.ShapeDtypeStruct((m, LANES), F32)],
        compiler_params=_cp("parallel"),
        name="moe_router",
    )(h, gain.reshape(1, d).astype(F32), wr)


def _row_copy(src_ref, t, dst_ref, r, sem):
    return pltpu.make_async_copy(src_ref.at[pl.ds(t, 1), :], dst_ref.at[pl.ds(r, 1), :], sem)


def _gather_kernel(idx_ref, src_ref, o_ref, buf_ref, sem):
    rows = buf_ref.shape[0]

    def issue(r, c):
        _row_copy(src_ref, idx_ref[0, 0, r], buf_ref, r, sem).start()
        return c

    def wait(r, c):
        _row_copy(src_ref, 0, buf_ref, r, sem).wait()
        return c

    lax.fori_loop(0, rows, issue, 0)
    lax.fori_loop(0, rows, wait, 0)
    w = buf_ref[...]
    half = w.shape[1]
    o_ref[:, :half] = pltpu.bitcast(w << 16, F32).astype(BF16)
    o_ref[:, half:] = pltpu.bitcast(w & jnp.uint32(0xFFFF0000), F32).astype(BF16)


def _gather_rows(src_tok, hp):
    p = src_tok.shape[0]
    half = hp.shape[1]
    rows = GATHER_ROWS
    return pl.pallas_call(
        _gather_kernel,
        grid=(p // rows,),
        in_specs=[pl.BlockSpec((1, 1, rows), lambda i: (i, 0, 0), memory_space=pltpu.SMEM),
                  pl.BlockSpec(memory_space=pl.ANY)],
        out_specs=pl.BlockSpec((rows, 2 * half), lambda i: (i, 0)),
        out_shape=jax.ShapeDtypeStruct((p, 2 * half), BF16),
        scratch_shapes=[pltpu.VMEM((rows, half), U32), pltpu.SemaphoreType.DMA(())],
        compiler_params=_cp("arbitrary"),
        name="moe_gather",
    )(src_tok.reshape(p // rows, 1, rows), hp)


def _gup_kernel(te_ref, nu_ref, x_ref, wg_ref, wu_ref, o_ref):
    used = pl.program_id(1) < nu_ref[0]

    @pl.when(used)
    def _():
        x = x_ref[...]
        gate = jnp.dot(x, wg_ref[0], preferred_element_type=F32)
        up = jnp.dot(x, wu_ref[0], preferred_element_type=F32)
        o_ref[...] = (_silu(gate) * up).astype(o_ref.dtype)

    @pl.when(jnp.logical_not(used))
    def _():
        o_ref[...] = jnp.zeros_like(o_ref)


def _grouped_up(tile_expert, n_used, xs, wg, wu, *, tm, tf):
    p, d = xs.shape
    f = wg.shape[2]
    grid_spec = pltpu.PrefetchScalarGridSpec(
        num_scalar_prefetch=2,
        grid=(f // tf, p // tm),
        in_specs=[pl.BlockSpec((tm, d), lambda j, i, te, nu: (jnp.minimum(i, nu[0] - 1), 0)),
                  pl.BlockSpec((1, d, tf), lambda j, i, te, nu: (te[i], 0, j)),
                  pl.BlockSpec((1, d, tf), lambda j, i, te, nu: (te[i], 0, j))],
        out_specs=pl.BlockSpec((tm, tf), lambda j, i, te, nu: (i, j)),
    )
    return pl.pallas_call(
        _gup_kernel,
        grid_spec=grid_spec,
        out_shape=jax.ShapeDtypeStruct((p, f), BF16),
        compiler_params=_cp("arbitrary", "arbitrary"),
        name="moe_up",
    )(tile_expert, n_used, xs, wg, wu)


def _gdown_kernel(te_ref, nu_ref, x_ref, wd_ref, gate_ref, o_ref):
    used = pl.program_id(1) < nu_ref[0]

    @pl.when(used)
    def _():
        o_ref[...] = jnp.dot(x_ref[...], wd_ref[0], preferred_element_type=F32) * gate_ref[...]

    @pl.when(jnp.logical_not(used))
    def _():
        o_ref[...] = jnp.zeros_like(o_ref)


def _grouped_down(tile_expert, n_used, hff, wd, row_gate, *, tm, tn):
    p, f = hff.shape
    d = wd.shape[2]
    grid_spec = pltpu.PrefetchScalarGridSpec(
        num_scalar_prefetch=2,
        grid=(d // tn, p // tm),
        in_specs=[pl.BlockSpec((tm, f), lambda j, i, te, nu: (jnp.minimum(i, nu[0] - 1), 0)),
                  pl.BlockSpec((1, f, tn), lambda j, i, te, nu: (te[i], 0, j)),
                  pl.BlockSpec((tm, 1), lambda j, i, te, nu: (i, 0))],
        out_specs=pl.BlockSpec((tm, tn), lambda j, i, te, nu: (i, j)),
    )
    return pl.pallas_call(
        _gdown_kernel,
        grid_spec=grid_spec,
        out_shape=jax.ShapeDtypeStruct((p, d), F32),
        compiler_params=_cp("arbitrary", "arbitrary"),
        name="moe_down",
    )(tile_expert, n_used, hff, wd, row_gate)


def _combine_kernel(pos_ref, h_ref, y_ref, o_ref, buf_ref, sem):
    rows = h_ref.shape[0]

    def issue(r, c):
        _row_copy(y_ref, pos_ref[0, 0, 2 * r], buf_ref.at[0], r, sem).start()
        _row_copy(y_ref, pos_ref[0, 0, 2 * r + 1], buf_ref.at[1], r, sem).start()
        return c

    def wait(r, c):
        _row_copy(y_ref, 0, buf_ref.at[0], r, sem).wait()
        _row_copy(y_ref, 0, buf_ref.at[1], r, sem).wait()
        return c

    lax.fori_loop(0, rows, issue, 0)
    lax.fori_loop(0, rows, wait, 0)
    o_ref[...] = h_ref[...] + buf_ref[0] + buf_ref[1]


def _combine(h, y, pos):
    m, d = h.shape
    rows = min(COMBINE_ROWS, m)
    return pl.pallas_call(
        _combine_kernel,
        grid=(m // rows,),
        in_specs=[pl.BlockSpec((1, 1, 2 * rows), lambda i: (i, 0, 0), memory_space=pltpu.SMEM),
                  pl.BlockSpec((rows, d), lambda i: (i, 0)),
                  pl.BlockSpec(memory_space=pl.ANY)],
        out_specs=pl.BlockSpec((rows, d), lambda i: (i, 0)),
        out_shape=jax.ShapeDtypeStruct((m, d), F32),
        scratch_shapes=[pltpu.VMEM((2, rows, d), F32), pltpu.SemaphoreType.DMA(())],
        compiler_params=_cp("arbitrary"),
        name="moe_combine",
    )(pos.reshape(m // rows, 1, 2 * rows), h, y)


def _moe(h, gain, w_router, wg, wu, wd):
    t, d = h.shape
    tm = MOE_TM
    hp, idx, wts = _router(h, gain, w_router, tm=512)
    flat_e = idx[:, :2].reshape(-1)
    flat_w = wts[:, :2].reshape(-1)
    onehot = (flat_e[:, None] == jnp.arange(N_EXPERTS, dtype=I32)[None, :]).astype(I32)
    csum = jnp.cumsum(onehot, axis=0)
    rank = jnp.sum((csum - onehot) * onehot, axis=1)
    counts = csum[-1]
    padded = ((counts + tm - 1) // tm) * tm
    ends = jnp.cumsum(padded)
    starts = ends - padded
    pos = starts[flat_e] + rank
    p_rows = 2 * t + N_EXPERTS * tm
    src_tok = jnp.zeros((p_rows,), I32).at[pos].set(jnp.arange(2 * t, dtype=I32) // 2)
    row_gate = jnp.zeros((p_rows,), F32).at[pos].set(flat_w).reshape(p_rows, 1)
    n_tiles = p_rows // tm
    n_used = (ends[-1] // tm).astype(I32).reshape(1)
    tile_start = jnp.arange(n_tiles, dtype=I32) * tm
    tile_expert = jnp.sum((tile_start[:, None] >= ends[None, :]).astype(I32), axis=1)
    last_expert = jnp.sum((ends[-1] - 1 >= ends).astype(I32))
    tile_expert = jnp.minimum(tile_expert, last_expert).astype(I32)

    xs = _gather_rows(src_tok, hp)
    hff = _grouped_up(tile_expert, n_used, xs, wg, wu, tm=tm, tf=512)
    y = _grouped_down(tile_expert, n_used, hff, wd, row_gate, tm=tm, tn=512)
    return _combine(h, y, pos.astype(I32))


def _rearranged_w_in(w_in):
    d, n = w_in.shape
    zero = n
    gdn0, ssd0 = 0, 4 * GDN_WIDTH + 2 * GDN_HEADS
    mla0 = ssd0 + SSD_WIDTH + SSD_CONV_DIM + SSD_HEADS
    ar = lambda a, b: jnp.arange(a, b, dtype=I32)
    kpe0 = mla0 + MLA_Q_RANK + MLA_KV_RANK
    small = jnp.full((LANES,), zero, I32)
    small = small.at[LANE_GDN_B:LANE_GDN_B + GDN_HEADS].set(ar(gdn0 + 4 * GDN_WIDTH, gdn0 + 4 * GDN_WIDTH + GDN_HEADS))
    small = small.at[LANE_GDN_A:LANE_GDN_A + GDN_HEADS].set(
        ar(gdn0 + 4 * GDN_WIDTH + GDN_HEADS, gdn0 + 4 * GDN_WIDTH + 2 * GDN_HEADS))
    dt0 = ssd0 + SSD_WIDTH + SSD_CONV_DIM
    small = small.at[LANE_SSD_DT:LANE_SSD_DT + SSD_HEADS].set(ar(dt0, dt0 + SSD_HEADS))
    cols = jnp.concatenate([
        ar(gdn0, gdn0 + 4 * GDN_WIDTH),
        ar(ssd0, ssd0 + SSD_WIDTH + SSD_CONV_DIM),
        ar(mla0, mla0 + MLA_Q_RANK),
        ar(kpe0, kpe0 + 64), ar(kpe0 + 32, kpe0 + 64), ar(kpe0, kpe0 + 32),
        ar(mla0 + MLA_Q_RANK, mla0 + MLA_Q_RANK + MLA_KV_RANK),
        small,
        jnp.full((LANES,), zero, I32),
    ])
    assert cols.shape[0] == U_COLS
    w_ext = jnp.concatenate([w_in, jnp.zeros((d, 1), w_in.dtype)], axis=1)
    return jnp.take(w_ext, cols, axis=1).astype(BF16)


def kernel(x, p, positions, norm_mix_g, w_in, w_out, gdn_conv_w, gdn_a_log, gdn_dt_bias, gdn_norm_g, ssd_conv_w, ssd_conv_b, ssd_a_log, ssd_dt_bias, ssd_d, ssd_norm_g, mla_q_a_g, mla_w_q_b, mla_kv_a_g, mla_w_kv_b, mla_q_norm_g, mla_k_norm_g, norm_ffn_g, ffn_w_gate, ffn_w_up, ffn_w_down, router_w, moe_w_gate, moe_w_up, moe_w_down, ple_w_proj, ple_w_gate, ple_norm_g):
    b, s, d = x.shape
    t = b * s
    depth = w_in.shape[0]
    h = x.reshape(t, d).astype(F32)
    pos_f = positions.astype(F32).reshape(b, s, 1)
    for i in range(depth):
        u = _mm(h, _rearranged_w_in(w_in[i]), gain=norm_mix_g[i], tm=1024, tn=512, name="in_proj")
        u3 = u.reshape(b, s, U_COLS)
        o_gdn = _gdn(u3, gdn_conv_w[i], gdn_a_log[i], gdn_dt_bias[i], gdn_norm_g[i])
        o_ssd = _ssd(u3, ssd_conv_w[i], ssd_conv_b[i], ssd_a_log[i], ssd_dt_bias[i], ssd_d[i], ssd_norm_g[i])
        q, k, v = _mla_pre(u3, pos_f, mla_q_a_g[i], mla_w_q_b[i], mla_kv_a_g[i], mla_w_kv_b[i],
                           mla_q_norm_g[i], mla_k_norm_g[i], ts=512)
        o_mla = _flash(q, k, v, tq=512, tk=512)
        mix = jnp.concatenate([o_gdn, o_ssd, o_mla], axis=-1).reshape(t, d)
        h = _mm(mix, w_out[i].astype(BF16), res=h, tm=1024, tn=1024, name="out_proj")
        j = i // 2
        if i % 2 == 0:
            ff = _swiglu_up(h, norm_ffn_g[i], ffn_w_gate[j].astype(BF16), ffn_w_up[j].astype(BF16), tm=1024, tn=512)
            h = _mm(ff, ffn_w_down[j].astype(BF16), res=h, tm=512, tn=512, name="ffn_down")
        else:
            h = _moe(h, norm_ffn_g[i], router_w[j], moe_w_gate[j].astype(BF16), moe_w_up[j].astype(BF16),
                     moe_w_down[j].astype(BF16))
        h = _ple(h, p[i].reshape(t, -1), ple_w_proj[i].astype(BF16), ple_norm_g[i], ple_w_gate[i].astype(BF16), tm=512)
    return h.reshape(b, s, d).astype(x.dtype)
```

```python
import functools

import jax
import jax.numpy as jnp
from jax import lax
from jax.experimental import pallas as pl
from jax.experimental.pallas import tpu as pltpu

F32 = jnp.float32
BF16 = jnp.bfloat16
U32 = jnp.uint32
I32 = jnp.int32
EPS = 1e-6

D_MODEL = 2048
GDN_HEADS, GDN_DH, GDN_CHUNK = 4, 128, 64
GDN_WIDTH = GDN_HEADS * GDN_DH
SSD_HEADS, SSD_P, SSD_GROUPS, SSD_N, SSD_CHUNK = 16, 64, 2, 128, 256
SSD_WIDTH = SSD_HEADS * SSD_P
SSD_CONV_DIM = SSD_WIDTH + 2 * SSD_GROUPS * SSD_N
MLA_HEADS, MLA_Q_RANK, MLA_KV_RANK = 4, 384, 256
MLA_NOPE, MLA_ROPE, MLA_V = 128, 64, 128
MLA_QK = MLA_NOPE + MLA_ROPE
ROPE_THETA = 10000.0
N_EXPERTS = 8
LANES = 128
MXU_COLS = 256
LOG2_E = 1.4426950408889634

U_COLS = 5632
COL_GDN_QKV, COL_GDN_Z, COL_SSD_Z, COL_SSD_XBC = 0, 1536, 2048, 3072
COL_QLAT, COL_KPE, COL_KVLAT, COL_SMALL = 4608, 4992, 5120, 5376
LANE_GDN_B, LANE_GDN_A, LANE_SSD_DT = 64, 68, 72

SEQ_BLOCK = 256
VMEM_LIMIT_BYTES = 56 * 1024 * 1024
MOE_TM = 512
GATHER_ROWS = 512
COMBINE_ROWS = 128


def _cp(*sem):
    return pltpu.CompilerParams(dimension_semantics=sem, vmem_limit_bytes=VMEM_LIMIT_BYTES)


def _rms(x, gain):
    return x * lax.rsqrt(jnp.mean(x * x, axis=-1, keepdims=True) + EPS) * gain


def _silu(x):
    return x * jax.nn.sigmoid(x)


def _bdot(a, b):
    return jnp.dot(a.astype(BF16), b.astype(BF16), preferred_element_type=F32)


def _bdot_nt(a, b):
    return lax.dot_general(a.astype(BF16), b.astype(BF16), (((1,), (1,)), ((), ())), preferred_element_type=F32)


def _bdot_tn(a, b):
    return lax.dot_general(a.astype(BF16), b.astype(BF16), (((0,), (0,)), ((), ())), preferred_element_type=F32)


def _split3(x):
    x1 = x.astype(BF16)
    r1 = x - x1.astype(F32)
    x2 = r1.astype(BF16)
    x3 = (r1 - x2.astype(F32)).astype(BF16)
    return x1, x2, x3


def _dot_exact_lhs(m01, x):
    x1, x2, x3 = _split3(x)
    d = functools.partial(jnp.dot, preferred_element_type=F32)
    return d(m01, x1) + d(m01, x2) + d(m01, x3)


def _mm_kernel(*refs, norm, cast, has_res):
    it = iter(refs)
    x_ref = next(it)
    g_ref = next(it) if norm else None
    w_ref = next(it)
    res_ref = next(it) if has_res else None
    o_ref = next(it)
    if cast:
        xs_ref = next(it)

        @pl.when(pl.program_id(1) == 0)
        def _():
            x = x_ref[...].astype(F32)
            if norm:
                x = _rms(x, g_ref[...])
            xs_ref[...] = x.astype(BF16)

        a = xs_ref[...]
    else:
        a = x_ref[...]
    acc = jnp.dot(a, w_ref[...], preferred_element_type=F32)
    if has_res:
        acc = acc + res_ref[...]
    o_ref[...] = acc.astype(o_ref.dtype)


def _mm(x, w, *, gain=None, res=None, out_dtype=F32, tm, tn, name):
    m, k = x.shape
    n = w.shape[1]
    tm, tn = min(tm, m), min(tn, n)
    norm = gain is not None
    cast = norm or x.dtype != BF16
    in_specs = [pl.BlockSpec((tm, k), lambda i, j: (i, 0))]
    args = [x]
    if norm:
        in_specs.append(pl.BlockSpec((1, k), lambda i, j: (0, 0)))
        args.append(gain.reshape(1, k).astype(F32))
    in_specs.append(pl.BlockSpec((k, tn), lambda i, j: (0, j)))
    args.append(w)
    if res is not None:
        in_specs.append(pl.BlockSpec((tm, tn), lambda i, j: (i, j)))
        args.append(res)
    return pl.pallas_call(
        functools.partial(_mm_kernel, norm=norm, cast=cast, has_res=res is not None),
        grid=(m // tm, n // tn),
        in_specs=in_specs,
        out_specs=pl.BlockSpec((tm, tn), lambda i, j: (i, j)),
        out_shape=jax.ShapeDtypeStruct((m, n), out_dtype),
        scratch_shapes=[pltpu.VMEM((tm, k), BF16)] if cast else [],
        compiler_params=_cp("parallel", "arbitrary"),
        name=name,
    )(*args)


def _mm_parts_kernel(*refs, widths):
    n = len(widths)
    x_refs, w_ref, res_ref, o_ref = refs[:n], refs[n], refs[n + 1], refs[n + 2]
    acc = res_ref[...]
    off = 0
    for x_ref, width in zip(x_refs, widths):
        acc = acc + jnp.dot(x_ref[...], w_ref[off:off + width, :], preferred_element_type=F32)
        off += width
    o_ref[...] = acc


def _mm_parts(xs, w, res, *, tm, tn, name):
    m = xs[0].shape[0]
    widths = tuple(x.shape[1] for x in xs)
    k, n = w.shape
    assert sum(widths) == k
    tm, tn = min(tm, m), min(tn, n)
    in_specs = [pl.BlockSpec((tm, width), lambda i, j: (i, 0)) for width in widths]
    in_specs += [pl.BlockSpec((k, tn), lambda i, j: (0, j)), pl.BlockSpec((tm, tn), lambda i, j: (i, j))]
    return pl.pallas_call(
        functools.partial(_mm_parts_kernel, widths=widths),
        grid=(m // tm, n // tn),
        in_specs=in_specs,
        out_specs=pl.BlockSpec((tm, tn), lambda i, j: (i, j)),
        out_shape=jax.ShapeDtypeStruct((m, n), F32),
        compiler_params=_cp("parallel", "arbitrary"),
        name=name,
    )(*xs, w, res)


def _swiglu_up_kernel(x_ref, g_ref, wg_ref, wu_ref, o_ref, xs_ref):
    @pl.when(pl.program_id(1) == 0)
    def _():
        xs_ref[...] = _rms(x_ref[...], g_ref[...]).astype(BF16)

    a = xs_ref[...]
    for c in range(0, o_ref.shape[1], MXU_COLS):
        cs = slice(c, c + MXU_COLS)
        gate = jnp.dot(a, wg_ref[:, cs], preferred_element_type=F32)
        up = jnp.dot(a, wu_ref[:, cs], preferred_element_type=F32)
        o_ref[:, cs] = (_silu(gate) * up).astype(o_ref.dtype)


def _swiglu_up(x, gain, wg, wu, *, tm, tn):
    m, k = x.shape
    n = wg.shape[1]
    tm, tn = min(tm, m), min(tn, n)
    return pl.pallas_call(
        _swiglu_up_kernel,
        grid=(m // tm, n // tn),
        in_specs=[pl.BlockSpec((tm, k), lambda i, j: (i, 0)),
                  pl.BlockSpec((1, k), lambda i, j: (0, 0)),
                  pl.BlockSpec((k, tn), lambda i, j: (0, j)),
                  pl.BlockSpec((k, tn), lambda i, j: (0, j))],
        out_specs=pl.BlockSpec((tm, tn), lambda i, j: (i, j)),
        out_shape=jax.ShapeDtypeStruct((m, n), BF16),
        scratch_shapes=[pltpu.VMEM((tm, k), BF16)],
        compiler_params=_cp("parallel", "arbitrary"),
        name="swiglu_up",
    )(x, gain.reshape(1, k).astype(F32), wg, wu)


def _ple_kernel(h_ref, p_ref, wp_ref, g_ref, wg_ref, o_ref):
    h = h_ref[...]
    e = jnp.dot(p_ref[...].astype(BF16), wp_ref[...], preferred_element_type=F32)
    e = _rms(e, g_ref[...])
    gate = jax.nn.sigmoid(jnp.dot(h.astype(BF16), wg_ref[...], preferred_element_type=F32))
    o_ref[...] = h + gate * e


def _ple(h, p, wp, gain, wg, *, tm):
    m, d = h.shape
    dp = p.shape[1]
    tm = min(tm, m)
    return pl.pallas_call(
        _ple_kernel,
        grid=(m // tm,),
        in_specs=[pl.BlockSpec((tm, d), lambda i: (i, 0)),
                  pl.BlockSpec((tm, dp), lambda i: (i, 0)),
                  pl.BlockSpec((dp, d), lambda i: (0, 0)),
                  pl.BlockSpec((1, d), lambda i: (0, 0)),
                  pl.BlockSpec((d, d), lambda i: (0, 0))],
        out_specs=pl.BlockSpec((tm, d), lambda i: (i, 0)),
        out_shape=jax.ShapeDtypeStruct((m, d), F32),
        compiler_params=_cp("parallel"),
        name="ple",
    )(h, p, wp, gain.reshape(1, d).astype(F32), wg)


def _conv_silu_slab(x, first, tail_ref, xbuf_ref, cw, bias):
    rows = x.shape[0]

    @pl.when(first)
    def _():
        tail_ref[...] = jnp.zeros_like(tail_ref)

    xbuf_ref[0:8, :] = tail_ref[...]
    xbuf_ref[8:8 + rows, :] = x
    tail_ref[...] = x[rows - 8:rows, :]
    y = (xbuf_ref[5:5 + rows, :] * cw[0:1, :] + xbuf_ref[6:6 + rows, :] * cw[1:2, :]
         + xbuf_ref[7:7 + rows, :] * cw[2:3, :] + x * cw[3:4, :])
    if bias is not None:
        y = y + bias
    return _silu(y)


def _gdn_kernel(qkv_ref, z_ref, sm_ref, cw_ref, vec_ref, ng_ref, o_ref, state_ref, tail_ref, xbuf_ref, xc_ref):
    rows = SEQ_BLOCK
    c = GDN_CHUNK
    first = pl.program_id(1) == 0

    @pl.when(first)
    def _():
        state_ref[...] = jnp.zeros_like(state_ref)

    for s in range(3):
        sl = slice(s * GDN_WIDTH, (s + 1) * GDN_WIDTH)
        xc_ref[:, sl] = _conv_silu_slab(qkv_ref[0, :, sl], first, tail_ref.at[s], xbuf_ref, cw_ref[:, sl], None)

    sm = sm_ref[0]
    beta_all = jax.nn.sigmoid(sm)
    g_all = vec_ref[0:1, :] * jax.nn.softplus(sm + vec_ref[1:2, :])

    row = lax.broadcasted_iota(I32, (rows, rows), 0)
    col = lax.broadcasted_iota(I32, (rows, rows), 1)
    same = (row // c) == (col // c)
    causal = same & (col <= row)
    strict = same & (col < row)
    tri01 = jnp.where(causal, 1.0, 0.0).astype(BF16)
    blk01 = jnp.where(same, 1.0, 0.0).astype(BF16)
    gc_all = _dot_exact_lhs(tri01, g_all)
    gl_all = _dot_exact_lhs(blk01, g_all)
    gc_all_t = gc_all.T

    heads = range(GDN_HEADS)
    eye = jnp.where(row == col, 1.0, 0.0).astype(F32)
    q_l, k_l, vb_l, kb_l, gc_l, gl_l, decay_l, p_l, x_l = [], [], [], [], [], [], [], [], []
    for h in heads:
        la = LANE_GDN_A + h
        q = xc_ref[:, h * GDN_DH:(h + 1) * GDN_DH]
        k = xc_ref[:, GDN_WIDTH + h * GDN_DH:GDN_WIDTH + (h + 1) * GDN_DH]
        v = xc_ref[:, 2 * GDN_WIDTH + h * GDN_DH:2 * GDN_WIDTH + (h + 1) * GDN_DH]
        q = q * lax.rsqrt(jnp.sum(q * q, axis=-1, keepdims=True) + EPS) * (GDN_DH ** -0.5)
        k = k * lax.rsqrt(jnp.sum(k * k, axis=-1, keepdims=True) + EPS)
        beta = beta_all[:, LANE_GDN_B + h:LANE_GDN_B + h + 1]
        gc = gc_all[:, la:la + 1]
        decay = jnp.where(causal, jnp.exp(gc - gc_all_t[la:la + 1, :]), 0.0)
        kb = k * beta
        m = jnp.where(strict, _bdot_nt(kb, k) * decay, 0.0)
        q_l.append(q), k_l.append(k), vb_l.append(v * beta), kb_l.append(kb)
        gc_l.append(gc), gl_l.append(gl_all[:, la:la + 1]), decay_l.append(decay)
        p_l.append(m), x_l.append(eye - m)

    span = 2
    while span < c:
        for h in heads:
            p_l[h] = _bdot(p_l[h], p_l[h])
        for h in heads:
            x_l[h] = x_l[h] + _bdot(x_l[h], p_l[h])
        span *= 2

    u_l, w_l, intra_l, qd_l, kd_l, ge_l = [], [], [], [], [], []
    for h in heads:
        eg = jnp.exp(gc_l[h])
        sol = _bdot(x_l[h], jnp.concatenate([vb_l[h], kb_l[h] * eg], axis=1))
        u_l.append(sol[:, :GDN_DH]), w_l.append(sol[:, GDN_DH:])
        intra_l.append((_bdot_nt(q_l[h], k_l[h]) * decay_l[h]).astype(BF16))
        qd_l.append(q_l[h] * eg)
        kd_l.append(k_l[h] * jnp.exp(gl_l[h] - gc_l[h]))
        ge_l.append(jnp.exp(gl_l[h]))

    st_l = [state_ref[h] for h in heads]
    out_l = [[] for _ in heads]
    for ci in range(rows // c):
        rs = slice(ci * c, (ci + 1) * c)
        for h in heads:
            st = st_l[h]
            v_new = u_l[h][rs] - _bdot(w_l[h][rs], st)
            pieces = []
            if ci > 0:
                pieces.append(jnp.zeros((ci * c, GDN_DH), F32))
            pieces.append(v_new)
            if (ci + 1) * c < rows:
                pieces.append(jnp.zeros((rows - (ci + 1) * c, GDN_DH), F32))
            v_pad = jnp.concatenate(pieces, axis=0) if len(pieces) > 1 else v_new
            out_l[h].append(_bdot(qd_l[h][rs], st)
                            + jnp.dot(intra_l[h][rs], v_pad.astype(BF16), preferred_element_type=F32))
            st_l[h] = st * ge_l[h][ci * c:ci * c + 1, :] + _bdot_tn(kd_l[h][rs], v_new)

    for h in heads:
        hs = slice(h * GDN_DH, (h + 1) * GDN_DH)
        state_ref[h] = st_l[h]
        o = jnp.concatenate(out_l[h], axis=0)
        o = _rms(o, ng_ref[...]) * _silu(z_ref[0, :, hs])
        o_ref[0, :, hs] = o.astype(o_ref.dtype)


def _gdn(u3, conv_w, a_log, dt_bias, norm_g):
    b, s, _ = u3.shape
    cw = conv_w.T.astype(F32)
    vec = jnp.zeros((2, LANES), F32)
    vec = vec.at[0, LANE_GDN_A:LANE_GDN_A + GDN_HEADS].set(-jnp.exp(a_log.astype(F32)))
    vec = vec.at[1, LANE_GDN_A:LANE_GDN_A + GDN_HEADS].set(dt_bias.astype(F32))
    blk = SEQ_BLOCK
    return pl.pallas_call(
        _gdn_kernel,
        grid=(b, s // blk),
        in_specs=[pl.BlockSpec((1, blk, 3 * GDN_WIDTH), lambda i, j: (i, j, COL_GDN_QKV // (3 * GDN_WIDTH))),
                  pl.BlockSpec((1, blk, GDN_WIDTH), lambda i, j: (i, j, COL_GDN_Z // GDN_WIDTH)),
                  pl.BlockSpec((1, blk, LANES), lambda i, j: (i, j, COL_SMALL // LANES)),
                  pl.BlockSpec((4, 3 * GDN_WIDTH), lambda i, j: (0, 0)),
                  pl.BlockSpec((2, LANES), lambda i, j: (0, 0)),
                  pl.BlockSpec((1, GDN_DH), lambda i, j: (0, 0))],
        out_specs=pl.BlockSpec((1, blk, GDN_WIDTH), lambda i, j: (i, j, 0)),
        out_shape=jax.ShapeDtypeStruct((b, s, GDN_WIDTH), BF16),
        scratch_shapes=[pltpu.VMEM((GDN_HEADS, GDN_DH, GDN_DH), F32),
                        pltpu.VMEM((3, 8, GDN_WIDTH), F32),
                        pltpu.VMEM((blk + 8, GDN_WIDTH), F32),
                        pltpu.VMEM((blk, 3 * GDN_WIDTH), F32)],
        compiler_params=_cp("parallel", "arbitrary"),
        name="gdn",
    )(u3, u3, u3, cw, vec, norm_g.reshape(1, GDN_DH).astype(F32))


def _ssd_kernel(z_ref, xbc_ref, sm_ref, cw_ref, cb_ref, vec_ref, dsk_ref, ng_ref, o_ref,
                state_ref, tail_ref, xbuf_ref, xc_ref, y_ref):
    rows = SEQ_BLOCK
    first = pl.program_id(1) == 0
    half = SSD_P

    @pl.when(first)
    def _():
        state_ref[...] = jnp.zeros_like(state_ref)

    slab = 512
    for s in range(SSD_CONV_DIM // slab):
        sl = slice(s * slab, (s + 1) * slab)
        xc_ref[:, sl] = _conv_silu_slab(xbc_ref[0, :, sl], first, tail_ref.at[s], xbuf_ref, cw_ref[:, sl], cb_ref[:, sl])

    dt_all = jax.nn.softplus(sm_ref[0] + vec_ref[1:2, :])
    da_all = dt_all * vec_ref[0:1, :]
    row = lax.broadcasted_iota(I32, (rows, rows), 0)
    col = lax.broadcasted_iota(I32, (rows, rows), 1)
    causal = col <= row
    tri01 = jnp.where(causal, 1.0, 0.0).astype(BF16)
    acs = _dot_exact_lhs(tri01, da_all)
    acs_t = acs.T
    lane = lax.broadcasted_iota(I32, (rows, LANES), 1)
    lo = lane < half
    lane1 = lax.broadcasted_iota(I32, (1, LANES), 1)
    lo1 = lane1 < half

    d = functools.partial(jnp.dot, preferred_element_type=F32)
    pairs_per_group = SSD_HEADS // SSD_GROUPS // 2
    n_pairs = SSD_HEADS // 2
    cb_l, bmt_l, cmb_l = [], [], []
    for g in range(SSD_GROUPS):
        bm = xc_ref[:, SSD_WIDTH + g * SSD_N:SSD_WIDTH + (g + 1) * SSD_N]
        cm = xc_ref[:, SSD_WIDTH + (SSD_GROUPS + g) * SSD_N:SSD_WIDTH + (SSD_GROUPS + g + 1) * SSD_N]
        cb_l.append(_bdot_nt(cm, bm))
        bmt_l.append(bm.T)
        cmb_l.append(cm.astype(BF16))

    att_l, dec_l, xdt_l, x_l, expa_l, ge_l = [], [], [], [], [], []
    for p in range(n_pairs):
        g = p // pairs_per_group
        la = LANE_SSD_DT + 2 * p
        lb = la + 1
        col_a, col_b = acs[:, la:la + 1], acs[:, lb:lb + 1]
        row_a, row_b = acs_t[la:la + 1, :], acs_t[lb:lb + 1, :]
        last_a, last_b = row_a[:, rows - 1:rows], row_b[:, rows - 1:rows]
        att_l.append(((cb_l[g] * jnp.where(causal, jnp.exp(col_a - row_a), 0.0)).astype(BF16),
                      (cb_l[g] * jnp.where(causal, jnp.exp(col_b - row_b), 0.0)).astype(BF16)))
        dec_l.append(((bmt_l[g] * jnp.exp(last_a - row_a)).astype(BF16),
                      (bmt_l[g] * jnp.exp(last_b - row_b)).astype(BF16)))
        x_pair = xc_ref[:, p * LANES:(p + 1) * LANES]
        xdt = x_pair * jnp.where(lo, dt_all[:, la:la + 1], dt_all[:, lb:lb + 1])
        xdt_l.append((jnp.where(lo, xdt, 0.0).astype(BF16), jnp.where(lo, 0.0, xdt).astype(BF16)))
        x_l.append(x_pair)
        expa_l.append(jnp.where(lo, jnp.exp(col_a), jnp.exp(col_b)))
        ge_l.append(jnp.where(lo1, jnp.exp(last_a), jnp.exp(last_b)))

    for p in range(n_pairs):
        g = p // pairs_per_group
        ps = slice(p * LANES, (p + 1) * LANES)
        xdt_a, xdt_b = xdt_l[p]
        y_diag = d(att_l[p][0], xdt_a) + d(att_l[p][1], xdt_b)
        st_new = d(dec_l[p][0], xdt_a) + d(dec_l[p][1], xdt_b)
        prev = state_ref[p]
        y_off = d(cmb_l[g], prev.astype(BF16)) * expa_l[p]
        state_ref[p] = prev * ge_l[p] + st_new
        y_ref[:, ps] = y_diag + y_off + x_l[p] * dsk_ref[:, ps]

    gw = SSD_WIDTH // SSD_GROUPS
    for g in range(SSD_GROUPS):
        gs = slice(g * gw, (g + 1) * gw)
        y = y_ref[:, gs] * _silu(z_ref[0, :, gs])
        o_ref[0, :, gs] = _rms(y, ng_ref[:, gs]).astype(o_ref.dtype)


def _ssd(u3, conv_w, conv_b, a_log, dt_bias, d_skip, norm_g):
    b, s, _ = u3.shape
    cw = conv_w.T.astype(F32)
    vec = jnp.zeros((2, LANES), F32)
    vec = vec.at[0, LANE_SSD_DT:LANE_SSD_DT + SSD_HEADS].set(-jnp.exp(a_log.astype(F32)))
    vec = vec.at[1, LANE_SSD_DT:LANE_SSD_DT + SSD_HEADS].set(dt_bias.astype(F32))
    dsk = jnp.repeat(d_skip.astype(F32), SSD_P).reshape(1, SSD_WIDTH)
    blk = SEQ_BLOCK
    return pl.pallas_call(
        _ssd_kernel,
        grid=(b, s // blk),
        in_specs=[pl.BlockSpec((1, blk, SSD_WIDTH), lambda i, j: (i, j, COL_SSD_Z // SSD_WIDTH)),
                  pl.BlockSpec((1, blk, SSD_CONV_DIM), lambda i, j: (i, j, COL_SSD_XBC // SSD_CONV_DIM)),
                  pl.BlockSpec((1, blk, LANES), lambda i, j: (i, j, COL_SMALL // LANES)),
                  pl.BlockSpec((4, SSD_CONV_DIM), lambda i, j: (0, 0)),
                  pl.BlockSpec((1, SSD_CONV_DIM), lambda i, j: (0, 0)),
                  pl.BlockSpec((2, LANES), lambda i, j: (0, 0)),
                  pl.BlockSpec((1, SSD_WIDTH), lambda i, j: (0, 0)),
                  pl.BlockSpec((1, SSD_WIDTH), lambda i, j: (0, 0))],
        out_specs=pl.BlockSpec((1, blk, SSD_WIDTH), lambda i, j: (i, j, 0)),
        out_shape=jax.ShapeDtypeStruct((b, s, SSD_WIDTH), BF16),
        scratch_shapes=[pltpu.VMEM((SSD_HEADS // 2, SSD_N, 2 * SSD_P), F32),
                        pltpu.VMEM((SSD_CONV_DIM // 512, 8, 512), F32),
                        pltpu.VMEM((blk + 8, 512), F32),
                        pltpu.VMEM((blk, SSD_CONV_DIM), F32),
                        pltpu.VMEM((blk, SSD_WIDTH), F32)],
        compiler_params=_cp("parallel", "arbitrary"),
        name="ssd",
    )(u3, u3, u3, cw, conv_b.reshape(1, SSD_CONV_DIM).astype(F32), vec, dsk,
      norm_g.reshape(1, SSD_WIDTH).astype(F32))


def _mla_pre_kernel(ql_ref, kpe_ref, kvl_ref, pos_ref, qag_ref, kvag_ref, wq_ref, wkv_ref, vec_ref,
                    q_ref, k_ref, v_ref):
    qn = _rms(ql_ref[0], qag_ref[...])
    qall = jnp.dot(qn.astype(BF16), wq_ref[...], preferred_element_type=F32)
    kvn = _rms(kvl_ref[0], kvag_ref[...])
    kv = jnp.dot(kvn.astype(BF16), wkv_ref[...], preferred_element_type=F32)
    kpe = kpe_ref[0]
    ang = pos_ref[0] * vec_ref[6:7, :]
    cos = jnp.cos(ang)
    sin = jnp.sin(ang) * vec_ref[7:8, :]
    gqn, gqa, gqb = vec_ref[0:1, :], vec_ref[1:2, :], vec_ref[2:3, :]
    gkn, gka, gkb = vec_ref[3:4, :], vec_ref[4:5, :], vec_ref[5:6, :]
    scale = MLA_QK ** -0.5 * LOG2_E
    k_rot = kpe * (cos * gka) + pltpu.roll(kpe, 64, 1) * (sin * gkb)
    k_pe_ss = 0.5 * jnp.sum(kpe * kpe, axis=-1, keepdims=True)
    nh = MLA_HEADS
    for h in range(nh):
        q_nope = qall[:, h * LANES:(h + 1) * LANES]
        q_pe = qall[:, (nh + h) * LANES:(nh + h + 1) * LANES]
        ss = jnp.sum(q_nope * q_nope, axis=-1, keepdims=True) + 0.5 * jnp.sum(q_pe * q_pe, axis=-1, keepdims=True)
        rstd = lax.rsqrt(ss * (1.0 / MLA_QK) + EPS)
        q_rot = q_pe * (cos * gqa) + pltpu.roll(q_pe, 64, 1) * (sin * gqb)
        q_ref[0, h, :, 0:LANES] = (q_nope * gqn * rstd * scale).astype(q_ref.dtype)
        q_ref[0, h, :, LANES:2 * LANES] = (q_rot * rstd * (0.5 * scale)).astype(q_ref.dtype)
        k_nope = kv[:, 2 * h * LANES:(2 * h + 1) * LANES]
        ssk = jnp.sum(k_nope * k_nope, axis=-1, keepdims=True) + k_pe_ss
        rstdk = lax.rsqrt(ssk * (1.0 / MLA_QK) + EPS)
        k_ref[0, h, :, 0:LANES] = (k_nope * gkn * rstdk).astype(k_ref.dtype)
        k_ref[0, h, :, LANES:2 * LANES] = (k_rot * rstdk).astype(k_ref.dtype)
        v_ref[0, h, :, 0:LANES] = kv[:, (2 * h + 1) * LANES:(2 * h + 2) * LANES].astype(v_ref.dtype)
        v_ref[0, h, :, LANES:2 * LANES] = jnp.ones((kv.shape[0], LANES), v_ref.dtype)


def _rope_pair_gains(g):
    g1, g2 = g[MLA_NOPE:MLA_NOPE + 32], g[MLA_NOPE + 32:MLA_NOPE + 64]
    return jnp.concatenate([g1, g2, g2, g1]), jnp.concatenate([g2, g1, g1, g2])


def _mla_pre(u3, pos_f, q_a_g, w_q_b, kv_a_g, w_kv_b, q_norm_g, k_norm_g, *, ts):
    b, s, _ = u3.shape
    nh = MLA_HEADS
    wq = w_q_b.reshape(MLA_Q_RANK, nh, MLA_QK)
    x1, x2 = wq[:, :, MLA_NOPE:MLA_NOPE + 32], wq[:, :, MLA_NOPE + 32:]
    wq_all = jnp.concatenate([wq[:, :, :MLA_NOPE].reshape(MLA_Q_RANK, nh * MLA_NOPE),
                              jnp.concatenate([x1, x2, x2, x1], axis=-1).reshape(MLA_Q_RANK, nh * LANES)],
                             axis=1).astype(BF16)
    half = MLA_ROPE // 2
    inv_freq = 1.0 / (ROPE_THETA ** (jnp.arange(half, dtype=F32) / half))
    gqa, gqb = _rope_pair_gains(q_norm_g.astype(F32))
    gka, gkb = _rope_pair_gains(k_norm_g.astype(F32))
    ones = jnp.ones((half,), F32)
    vec = jnp.stack([q_norm_g[:MLA_NOPE].astype(F32), gqa, gqb, k_norm_g[:MLA_NOPE].astype(F32), gka, gkb,
                     jnp.tile(inv_freq, 4), jnp.concatenate([-ones, ones, ones, -ones])])
    ts = min(ts, s)
    qk_shape = jax.ShapeDtypeStruct((b, nh, s, 2 * LANES), BF16)
    return pl.pallas_call(
        _mla_pre_kernel,
        grid=(b, s // ts),
        in_specs=[pl.BlockSpec((1, ts, MLA_Q_RANK), lambda i, j: (i, j, COL_QLAT // MLA_Q_RANK)),
                  pl.BlockSpec((1, ts, LANES), lambda i, j: (i, j, COL_KPE // LANES)),
                  pl.BlockSpec((1, ts, MLA_KV_RANK), lambda i, j: (i, j, COL_KVLAT // MLA_KV_RANK)),
                  pl.BlockSpec((1, ts, 1), lambda i, j: (i, j, 0)),
                  pl.BlockSpec((1, MLA_Q_RANK), lambda i, j: (0, 0)),
                  pl.BlockSpec((1, MLA_KV_RANK), lambda i, j: (0, 0)),
                  pl.BlockSpec((MLA_Q_RANK, 2 * nh * LANES), lambda i, j: (0, 0)),
                  pl.BlockSpec((MLA_KV_RANK, 2 * nh * LANES), lambda i, j: (0, 0)),
                  pl.BlockSpec((8, LANES), lambda i, j: (0, 0))],
        out_specs=[pl.BlockSpec((1, nh, ts, 2 * LANES), lambda i, j: (i, 0, j, 0)),
                   pl.BlockSpec((1, nh, ts, 2 * LANES), lambda i, j: (i, 0, j, 0)),
                   pl.BlockSpec((1, nh, ts, 2 * LANES), lambda i, j: (i, 0, j, 0))],
        out_shape=[qk_shape, qk_shape, qk_shape],
        compiler_params=_cp("parallel", "parallel"),
        name="mla_pre",
    )(u3, u3, u3, pos_f, q_a_g.reshape(1, -1).astype(F32), kv_a_g.reshape(1, -1).astype(F32),
      wq_all, w_kv_b.astype(BF16), vec)


def _flash_kernel(q_ref, k_ref, v_ref, o_ref, m_ref, acc_ref, *, t):
    qi = pl.program_id(1)
    kj = pl.program_id(2)
    nh = q_ref.shape[1]

    @pl.when(kj == 0)
    def _():
        m_ref[...] = jnp.full_like(m_ref, -jnp.inf)
        acc_ref[...] = jnp.zeros_like(acc_ref)

    def step(diagonal):
        s_l = [lax.dot_general(q_ref[0, h], k_ref[0, h], (((1,), (1,)), ((), ())), preferred_element_type=F32)
               for h in range(nh)]
        if diagonal:
            keep = lax.broadcasted_iota(I32, (t, t), 0) >= lax.broadcasted_iota(I32, (t, t), 1)
            s_l = [jnp.where(keep, s, -jnp.inf) for s in s_l]
        p_l, alpha_l = [], []
        for h in range(nh):
            m_old = m_ref[h]
            m_new = jnp.maximum(m_old, jnp.max(s_l[h], axis=-1, keepdims=True))
            p_l.append(jnp.exp2(s_l[h] - jnp.tile(m_new, (1, t // LANES))).astype(BF16))
            alpha_l.append(jnp.exp2(m_old - m_new))
            m_ref[h] = m_new
        for h in range(nh):
            pv = jnp.dot(p_l[h], v_ref[0, h], preferred_element_type=F32)
            acc_ref[h] = jnp.tile(alpha_l[h], (1, 2)) * acc_ref[h] + pv

    @pl.when(kj < qi)
    def _():
        step(False)

    @pl.when(kj == qi)
    def _():
        step(True)
        for h in range(nh):
            acc = acc_ref[h]
            o_ref[0, :, h * LANES:(h + 1) * LANES] = (acc[:, :LANES] / acc[:, LANES:]).astype(o_ref.dtype)


def _flash(q, k, v, *, t):
    b, nh, s, dq = q.shape
    t = min(t, s)
    kv_spec = pl.BlockSpec((1, nh, t, dq), lambda i, qi, kj: (i, 0, jnp.minimum(kj, qi), 0))
    return pl.pallas_call(
        functools.partial(_flash_kernel, t=t),
        grid=(b, s // t, s // t),
        in_specs=[pl.BlockSpec((1, nh, t, dq), lambda i, qi, kj: (i, 0, qi, 0)), kv_spec, kv_spec],
        out_specs=pl.BlockSpec((1, t, nh * LANES), lambda i, qi, kj: (i, qi, 0)),
        out_shape=jax.ShapeDtypeStruct((b, s, nh * LANES), BF16),
        scratch_shapes=[pltpu.VMEM((nh, t, LANES), F32), pltpu.VMEM((nh, t, 2 * LANES), F32)],
        compiler_params=_cp("parallel", "parallel", "arbitrary"),
        name="mla_flash",
    )(q, k, v)


def _router_kernel(h_ref, g_ref, wr_ref, idx_ref, wt_ref):
    xn = _rms(h_ref[...], g_ref[...])
    logits = jnp.dot(xn, wr_ref[...], precision=lax.Precision.HIGHEST, preferred_element_type=F32)
    lane = lax.broadcasted_iota(I32, logits.shape, 1)
    logits = jnp.where(lane < N_EXPERTS, logits, -jnp.inf)
    m1 = jnp.max(logits, axis=-1, keepdims=True)
    i1 = jnp.min(jnp.where(logits == m1, lane, LANES), axis=-1, keepdims=True)
    rest = jnp.where(lane == i1, -jnp.inf, logits)
    m2 = jnp.max(rest, axis=-1, keepdims=True)
    i2 = jnp.min(jnp.where(rest == m2, lane, LANES), axis=-1, keepdims=True)
    e2 = jnp.exp(m2 - m1)
    w1 = 1.0 / (1.0 + e2)
    w2 = e2 / (1.0 + e2)
    idx_ref[...] = jnp.where(lane == 0, i1, jnp.where(lane == 1, i2, 0))
    wt_ref[...] = jnp.where(lane == 0, w1, jnp.where(lane == 1, w2, 0.0))


def _router(h, gain, w_router, *, tm):
    m, d = h.shape
    tm = min(tm, m)
    wr = jnp.zeros((d, LANES), F32).at[:, :N_EXPERTS].set(w_router.astype(F32))
    return pl.pallas_call(
        _router_kernel,
        grid=(m // tm,),
        in_specs=[pl.BlockSpec((tm, d), lambda i: (i, 0)),
                  pl.BlockSpec((1, d), lambda i: (0, 0)),
                  pl.BlockSpec((d, LANES), lambda i: (0, 0))],
        out_specs=[pl.BlockSpec((tm, LANES), lambda i: (i, 0)),
                   pl.BlockSpec((tm, LANES), lambda i: (i, 0))],
        out_shape=[jax.ShapeDtypeStruct((m, LANES), I32),
                   jax.ShapeDtypeStruct((m, LANES), F32)],
        compiler_params=_cp("parallel"),
        name="moe_router",
    )(h, gain.reshape(1, d).astype(F32), wr)


def _row_copy(src_ref, t, dst_ref, r, sem):
    return pltpu.make_async_copy(src_ref.at[pl.ds(t, 1), :], dst_ref.at[pl.ds(r, 1), :], sem)


def _gather_kernel(idx_ref, src_ref, g_ref, o_ref, buf_ref, sem):
    rows = buf_ref.shape[1]
    i = pl.program_id(0)
    n = pl.num_programs(0)

    def start_step(step, slot):
        def issue(r, c):
            _row_copy(src_ref, idx_ref[step * rows + r], buf_ref.at[slot], r, sem.at[slot]).start()
            return c
        lax.fori_loop(0, rows, issue, 0, unroll=8)

    @pl.when(i == 0)
    def _():
        start_step(0, 0)

    @pl.when(i + 1 < n)
    def _():
        start_step(i + 1, (i + 1) % 2)

    slot = i % 2

    def wait(r, c):
        _row_copy(src_ref, 0, buf_ref.at[slot], r, sem.at[slot]).wait()
        return c

    lax.fori_loop(0, rows, wait, 0, unroll=8)
    o_ref[...] = _rms(buf_ref[slot], g_ref[...]).astype(o_ref.dtype)


def _gather_norm_rows(src_tok, h, gain):
    p = src_tok.shape[0]
    d = h.shape[1]
    rows = GATHER_ROWS
    grid_spec = pltpu.PrefetchScalarGridSpec(
        num_scalar_prefetch=1,
        grid=(p // rows,),
        in_specs=[pl.BlockSpec(memory_space=pl.ANY), pl.BlockSpec((1, d), lambda i, idx: (0, 0))],
        out_specs=pl.BlockSpec((rows, d), lambda i, idx: (i, 0)),
        scratch_shapes=[pltpu.VMEM((2, rows, d), F32), pltpu.SemaphoreType.DMA((2,))],
    )
    return pl.pallas_call(
        _gather_kernel,
        grid_spec=grid_spec,
        out_shape=jax.ShapeDtypeStruct((p, d), BF16),
        compiler_params=_cp("arbitrary"),
        name="moe_gather",
    )(src_tok, h, gain.reshape(1, d).astype(F32))


def _gup_kernel(te_ref, nu_ref, x_ref, wg_ref, wu_ref, o_ref, wgb_ref, wub_ref):
    i = pl.program_id(1)
    used = i < nu_ref[0]

    @pl.when((i == 0) | (te_ref[i] != te_ref[jnp.maximum(i - 1, 0)]))
    def _():
        wgb_ref[...] = wg_ref[0].astype(BF16)
        wub_ref[...] = wu_ref[0].astype(BF16)

    @pl.when(used)
    def _():
        x = x_ref[...]
        for c in range(0, o_ref.shape[1], MXU_COLS):
            cs = slice(c, c + MXU_COLS)
            gate = jnp.dot(x, wgb_ref[:, cs], preferred_element_type=F32)
            up = jnp.dot(x, wub_ref[:, cs], preferred_element_type=F32)
            o_ref[:, cs] = (_silu(gate) * up).astype(o_ref.dtype)

    @pl.when(jnp.logical_not(used))
    def _():
        o_ref[...] = jnp.zeros_like(o_ref)


def _grouped_up(tile_expert, n_used, xs, wg, wu, *, tm, tf):
    p, d = xs.shape
    f = wg.shape[2]
    grid_spec = pltpu.PrefetchScalarGridSpec(
        num_scalar_prefetch=2,
        grid=(f // tf, p // tm),
        in_specs=[pl.BlockSpec((tm, d), lambda j, i, te, nu: (jnp.minimum(i, nu[0] - 1), 0)),
                  pl.BlockSpec((1, d, tf), lambda j, i, te, nu: (te[i], 0, j)),
                  pl.BlockSpec((1, d, tf), lambda j, i, te, nu: (te[i], 0, j))],
        out_specs=pl.BlockSpec((tm, tf), lambda j, i, te, nu: (i, j)),
        scratch_shapes=[pltpu.VMEM((d, tf), BF16), pltpu.VMEM((d, tf), BF16)],
    )
    return pl.pallas_call(
        _gup_kernel,
        grid_spec=grid_spec,
        out_shape=jax.ShapeDtypeStruct((p, f), BF16),
        compiler_params=_cp("arbitrary", "arbitrary"),
        name="moe_up",
    )(tile_expert, n_used, xs, wg, wu)


def _gdown_kernel(te_ref, nu_ref, x_ref, wd_ref, o_ref):
    used = pl.program_id(1) < nu_ref[0]

    @pl.when(used)
    def _():
        o_ref[...] = jnp.dot(x_ref[...], wd_ref[0], preferred_element_type=F32)

    @pl.when(jnp.logical_not(used))
    def _():
        o_ref[...] = jnp.zeros_like(o_ref)


def _grouped_down(tile_expert, n_used, hff, wd, *, tm, tn):
    p, f = hff.shape
    d = wd.shape[2]
    grid_spec = pltpu.PrefetchScalarGridSpec(
        num_scalar_prefetch=2,
        grid=(d // tn, p // tm),
        in_specs=[pl.BlockSpec((tm, f), lambda j, i, te, nu: (jnp.minimum(i, nu[0] - 1), 0)),
                  pl.BlockSpec((1, f, tn), lambda j, i, te, nu: (te[i], 0, j))],
        out_specs=pl.BlockSpec((tm, tn), lambda j, i, te, nu: (i, j)),
    )
    return pl.pallas_call(
        _gdown_kernel,
        grid_spec=grid_spec,
        out_shape=jax.ShapeDtypeStruct((p, d), F32),
        compiler_params=_cp("arbitrary", "arbitrary"),
        name="moe_down",
    )(tile_expert, n_used, hff, wd)


def _combine_kernel(pos_ref, h_ref, wt_ref, y_ref, o_ref, buf_ref, sem):
    rows = h_ref.shape[0]
    i = pl.program_id(0)
    n = pl.num_programs(0)

    def start_step(step, slot):
        def issue(r, c):
            base = 2 * (step * rows + r)
            _row_copy(y_ref, pos_ref[base], buf_ref.at[slot, 0], r, sem.at[slot]).start()
            _row_copy(y_ref, pos_ref[base + 1], buf_ref.at[slot, 1], r, sem.at[slot]).start()
            return c
        lax.fori_loop(0, rows, issue, 0, unroll=4)

    @pl.when(i == 0)
    def _():
        start_step(0, 0)

    @pl.when(i + 1 < n)
    def _():
        start_step(i + 1, (i + 1) % 2)

    slot = i % 2

    def wait(r, c):
        _row_copy(y_ref, 0, buf_ref.at[slot, 0], r, sem.at[slot]).wait()
        _row_copy(y_ref, 0, buf_ref.at[slot, 1], r, sem.at[slot]).wait()
        return c

    lax.fori_loop(0, rows, wait, 0, unroll=4)
    wt = wt_ref[...]
    o_ref[...] = h_ref[...] + wt[:, 0:1] * buf_ref[slot, 0] + wt[:, 1:2] * buf_ref[slot, 1]


def _combine(h, y, pos, wts):
    m, d = h.shape
    rows = min(COMBINE_ROWS, m)
    grid_spec = pltpu.PrefetchScalarGridSpec(
        num_scalar_prefetch=1,
        grid=(m // rows,),
        in_specs=[pl.BlockSpec((rows, d), lambda i, pos: (i, 0)),
                  pl.BlockSpec((rows, LANES), lambda i, pos: (i, 0)),
                  pl.BlockSpec(memory_space=pl.ANY)],
        out_specs=pl.BlockSpec((rows, d), lambda i, pos: (i, 0)),
        scratch_shapes=[pltpu.VMEM((2, 2, rows, d), F32), pltpu.SemaphoreType.DMA((2,))],
    )
    return pl.pallas_call(
        _combine_kernel,
        grid_spec=grid_spec,
        out_shape=jax.ShapeDtypeStruct((m, d), F32),
        compiler_params=_cp("arbitrary"),
        name="moe_combine",
    )(pos, h, wts, y)


def _moe(h, gain, w_router, wg, wu, wd):
    t, d = h.shape
    tm = MOE_TM
    n2 = 2 * t
    idx, wts = _router(h, gain, w_router, tm=512)
    flat_e = idx[:, :2].reshape(-1)
    onehot = (flat_e[:, None] == jnp.arange(N_EXPERTS, dtype=I32)[None, :]).astype(I32)
    csum = jnp.cumsum(onehot, axis=0)
    rank = jnp.sum((csum - onehot) * onehot, axis=1)
    counts = csum[-1]
    padded = ((counts + tm - 1) // tm) * tm
    ends = jnp.cumsum(padded)
    starts = ends - padded
    pos = (starts[flat_e] + rank).astype(I32)
    p_rows = n2 + N_EXPERTS * tm
    n_tiles = p_rows // tm
    n_used = (ends[-1] // tm).astype(I32).reshape(1)
    tile_start = jnp.arange(n_tiles, dtype=I32) * tm
    tile_expert = jnp.sum((tile_start[:, None] >= ends[None, :]).astype(I32), axis=1)
    last_expert = jnp.sum((ends[-1] - 1 >= ends).astype(I32))
    tile_expert = jnp.minimum(tile_expert, last_expert).astype(I32)
    order = jnp.sort(flat_e * n2 + jnp.arange(n2, dtype=I32)) % n2
    row = jnp.arange(p_rows, dtype=I32)
    row_e = jnp.repeat(tile_expert, tm)
    local = row - starts[row_e]
    first = (jnp.cumsum(counts) - counts)[row_e]
    src_tok = jnp.where(local < counts[row_e], order[jnp.clip(first + local, 0, n2 - 1)] // 2, 0).astype(I32)

    xs = _gather_norm_rows(src_tok, h, gain)
    hff = _grouped_up(tile_expert, n_used, xs, wg, wu, tm=tm, tf=1024)
    y = _grouped_down(tile_expert, n_used, hff, wd, tm=tm, tn=512)
    return _combine(h, y, pos, wts)


def _rearranged_w_in(w_in):
    d = w_in.shape[0]
    gdn0, ssd0 = 0, 4 * GDN_WIDTH + 2 * GDN_HEADS
    mla0 = ssd0 + SSD_WIDTH + SSD_CONV_DIM + SSD_HEADS
    kpe0 = mla0 + MLA_Q_RANK + MLA_KV_RANK
    gb0 = gdn0 + 4 * GDN_WIDTH
    dt0 = ssd0 + SSD_WIDTH + SSD_CONV_DIM
    half = MLA_ROPE // 2
    w = w_in.astype(BF16)
    zeros = lambda n: jnp.zeros((d, n), BF16)
    cols = lambda a, n: w[:, a:a + n]
    out = jnp.concatenate([
        cols(gdn0, 4 * GDN_WIDTH),
        cols(ssd0, SSD_WIDTH + SSD_CONV_DIM),
        cols(mla0, MLA_Q_RANK),
        cols(kpe0, MLA_ROPE), cols(kpe0 + half, half), cols(kpe0, half),
        cols(mla0 + MLA_Q_RANK, MLA_KV_RANK),
        zeros(LANE_GDN_B), cols(gb0, 2 * GDN_HEADS), cols(dt0, SSD_HEADS),
        zeros(LANES - LANE_SSD_DT - SSD_HEADS),
        zeros(LANES),
    ], axis=1)
    assert out.shape[1] == U_COLS and LANE_GDN_A == LANE_GDN_B + GDN_HEADS and LANE_SSD_DT == LANE_GDN_A + GDN_HEADS
    return out


def kernel(x, p, positions, norm_mix_g, w_in, w_out, gdn_conv_w, gdn_a_log, gdn_dt_bias, gdn_norm_g, ssd_conv_w, ssd_conv_b, ssd_a_log, ssd_dt_bias, ssd_d, ssd_norm_g, mla_q_a_g, mla_w_q_b, mla_kv_a_g, mla_w_kv_b, mla_q_norm_g, mla_k_norm_g, norm_ffn_g, ffn_w_gate, ffn_w_up, ffn_w_down, router_w, moe_w_gate, moe_w_up, moe_w_down, ple_w_proj, ple_w_gate, ple_norm_g):
    b, s, d = x.shape
    t = b * s
    depth = w_in.shape[0]
    h = x.reshape(t, d).astype(F32)
    pos_f = positions.astype(F32).reshape(b, s, 1)
    for i in range(depth):
        u = _mm(h, _rearranged_w_in(w_in[i]), gain=norm_mix_g[i], tm=1024, tn=1408, name="in_proj")
        u3 = u.reshape(b, s, U_COLS)
        o_gdn = _gdn(u3, gdn_conv_w[i], gdn_a_log[i], gdn_dt_bias[i], gdn_norm_g[i])
        o_ssd = _ssd(u3, ssd_conv_w[i], ssd_conv_b[i], ssd_a_log[i], ssd_dt_bias[i], ssd_d[i], ssd_norm_g[i])
        q, k, v = _mla_pre(u3, pos_f, mla_q_a_g[i], mla_w_q_b[i], mla_kv_a_g[i], mla_w_kv_b[i],
                           mla_q_norm_g[i], mla_k_norm_g[i], ts=512)
        o_mla = _flash(q, k, v, t=512)
        mix_parts = [o.reshape(t, o.shape[-1]) for o in (o_gdn, o_ssd, o_mla)]
        h = _mm_parts(mix_parts, w_out[i].astype(BF16), h, tm=1024, tn=1024, name="out_proj")
        j = i // 2
        if i % 2 == 0:
            ff = _swiglu_up(h, norm_ffn_g[i], ffn_w_gate[j].astype(BF16), ffn_w_up[j].astype(BF16), tm=1024, tn=512)
            h = _mm(ff, ffn_w_down[j].astype(BF16), res=h, tm=1024, tn=512, name="ffn_down")
        else:
            h = _moe(h, norm_ffn_g[i], router_w[j], moe_w_gate[j], moe_w_up[j], moe_w_down[j].astype(BF16))
        h = _ple(h, p[i].reshape(t, -1), ple_w_proj[i].astype(BF16), ple_norm_g[i], ple_w_gate[i].astype(BF16), tm=512)
    return h.reshape(b, s, d).astype(x.dtype)
```

```python
import functools

import jax
import jax.numpy as jnp
from jax import lax
from jax.experimental import pallas as pl
from jax.experimental.pallas import tpu as pltpu

F32 = jnp.float32
BF16 = jnp.bfloat16
U32 = jnp.uint32
I32 = jnp.int32
EPS = 1e-6

D_MODEL = 2048
GDN_HEADS, GDN_DH, GDN_CHUNK = 4, 128, 64
GDN_WIDTH = GDN_HEADS * GDN_DH
SSD_HEADS, SSD_P, SSD_GROUPS, SSD_N, SSD_CHUNK = 16, 64, 2, 128, 256
SSD_WIDTH = SSD_HEADS * SSD_P
SSD_CONV_DIM = SSD_WIDTH + 2 * SSD_GROUPS * SSD_N
MLA_HEADS, MLA_Q_RANK, MLA_KV_RANK = 4, 384, 256
MLA_NOPE, MLA_ROPE, MLA_V = 128, 64, 128
MLA_QK = MLA_NOPE + MLA_ROPE
ROPE_THETA = 10000.0
N_EXPERTS = 8
LANES = 128
MXU_COLS = 256
LOG2_E = 1.4426950408889634

U_COLS = 5632
COL_GDN_QKV, COL_GDN_Z, COL_SSD_Z, COL_SSD_XBC = 0, 1536, 2048, 3072
COL_QLAT, COL_KPE, COL_KVLAT, COL_SMALL = 4608, 4992, 5120, 5376
LANE_GDN_B, LANE_GDN_A, LANE_SSD_DT = 64, 68, 72

SEQ_BLOCK = 256
VMEM_LIMIT_BYTES = 56 * 1024 * 1024
MOE_TM = 512
GATHER_ROWS = 512
COMBINE_ROWS = 128


def _cp(*sem):
    return pltpu.CompilerParams(dimension_semantics=sem, vmem_limit_bytes=VMEM_LIMIT_BYTES)


def _rms(x, gain):
    return x * lax.rsqrt(jnp.mean(x * x, axis=-1, keepdims=True) + EPS) * gain


def _silu(x):
    return x * jax.nn.sigmoid(x)


def _bdot(a, b):
    return jnp.dot(a.astype(BF16), b.astype(BF16), preferred_element_type=F32)


def _bdot_nt(a, b):
    return lax.dot_general(a.astype(BF16), b.astype(BF16), (((1,), (1,)), ((), ())), preferred_element_type=F32)


def _bdot_tn(a, b):
    return lax.dot_general(a.astype(BF16), b.astype(BF16), (((0,), (0,)), ((), ())), preferred_element_type=F32)


def _split3(x):
    x1 = x.astype(BF16)
    r1 = x - x1.astype(F32)
    x2 = r1.astype(BF16)
    x3 = (r1 - x2.astype(F32)).astype(BF16)
    return x1, x2, x3


def _dot_exact_lhs(m01, x):
    x1, x2, x3 = _split3(x)
    d = functools.partial(jnp.dot, preferred_element_type=F32)
    return d(m01, x1) + d(m01, x2) + d(m01, x3)


def _mm_kernel(*refs, norm, cast, has_res):
    it = iter(refs)
    x_ref = next(it)
    g_ref = next(it) if norm else None
    w_ref = next(it)
    res_ref = next(it) if has_res else None
    o_ref = next(it)
    if cast:
        xs_ref = next(it)

        @pl.when(pl.program_id(1) == 0)
        def _():
            x = x_ref[...].astype(F32)
            if norm:
                x = _rms(x, g_ref[...])
            xs_ref[...] = x.astype(BF16)

        a = xs_ref[...]
    else:
        a = x_ref[...]
    acc = jnp.dot(a, w_ref[...], preferred_element_type=F32)
    if has_res:
        acc = acc + res_ref[...]
    o_ref[...] = acc.astype(o_ref.dtype)


def _mm(x, w, *, gain=None, res=None, out_dtype=F32, tm, tn, name):
    m, k = x.shape
    n = w.shape[1]
    tm, tn = min(tm, m), min(tn, n)
    norm = gain is not None
    cast = norm or x.dtype != BF16
    in_specs = [pl.BlockSpec((tm, k), lambda i, j: (i, 0))]
    args = [x]
    if norm:
        in_specs.append(pl.BlockSpec((1, k), lambda i, j: (0, 0)))
        args.append(gain.reshape(1, k).astype(F32))
    in_specs.append(pl.BlockSpec((k, tn), lambda i, j: (0, j)))
    args.append(w)
    if res is not None:
        in_specs.append(pl.BlockSpec((tm, tn), lambda i, j: (i, j)))
        args.append(res)
    return pl.pallas_call(
        functools.partial(_mm_kernel, norm=norm, cast=cast, has_res=res is not None),
        grid=(m // tm, n // tn),
        in_specs=in_specs,
        out_specs=pl.BlockSpec((tm, tn), lambda i, j: (i, j)),
        out_shape=jax.ShapeDtypeStruct((m, n), out_dtype),
        scratch_shapes=[pltpu.VMEM((tm, k), BF16)] if cast else [],
        compiler_params=_cp("parallel", "arbitrary"),
        name=name,
    )(*args)


def _mm_parts_kernel(*refs, widths):
    n = len(widths)
    x_refs, w_ref, res_ref, o_ref = refs[:n], refs[n], refs[n + 1], refs[n + 2]
    acc = res_ref[...]
    off = 0
    for x_ref, width in zip(x_refs, widths):
        acc = acc + jnp.dot(x_ref[...], w_ref[off:off + width, :], preferred_element_type=F32)
        off += width
    o_ref[...] = acc


def _mm_parts(xs, w, res, *, tm, tn, name):
    m = xs[0].shape[0]
    widths = tuple(x.shape[1] for x in xs)
    k, n = w.shape
    assert sum(widths) == k
    tm, tn = min(tm, m), min(tn, n)
    in_specs = [pl.BlockSpec((tm, width), lambda i, j: (i, 0)) for width in widths]
    in_specs += [pl.BlockSpec((k, tn), lambda i, j: (0, j)), pl.BlockSpec((tm, tn), lambda i, j: (i, j))]
    return pl.pallas_call(
        functools.partial(_mm_parts_kernel, widths=widths),
        grid=(m // tm, n // tn),
        in_specs=in_specs,
        out_specs=pl.BlockSpec((tm, tn), lambda i, j: (i, j)),
        out_shape=jax.ShapeDtypeStruct((m, n), F32),
        compiler_params=_cp("parallel", "arbitrary"),
        name=name,
    )(*xs, w, res)


def _swiglu_up_kernel(x_ref, g_ref, wg_ref, wu_ref, o_ref, xs_ref):
    @pl.when(pl.program_id(1) == 0)
    def _():
        xs_ref[...] = _rms(x_ref[...], g_ref[...]).astype(BF16)

    a = xs_ref[...]
    for c in range(0, o_ref.shape[1], MXU_COLS):
        cs = slice(c, c + MXU_COLS)
        gate = jnp.dot(a, wg_ref[:, cs], preferred_element_type=F32)
        up = jnp.dot(a, wu_ref[:, cs], preferred_element_type=F32)
        o_ref[:, cs] = (_silu(gate) * up).astype(o_ref.dtype)


def _swiglu_up(x, gain, wg, wu, *, tm, tn):
    m, k = x.shape
    n = wg.shape[1]
    tm, tn = min(tm, m), min(tn, n)
    return pl.pallas_call(
        _swiglu_up_kernel,
        grid=(m // tm, n // tn),
        in_specs=[pl.BlockSpec((tm, k), lambda i, j: (i, 0)),
                  pl.BlockSpec((1, k), lambda i, j: (0, 0)),
                  pl.BlockSpec((k, tn), lambda i, j: (0, j)),
                  pl.BlockSpec((k, tn), lambda i, j: (0, j))],
        out_specs=pl.BlockSpec((tm, tn), lambda i, j: (i, j)),
        out_shape=jax.ShapeDtypeStruct((m, n), BF16),
        scratch_shapes=[pltpu.VMEM((tm, k), BF16)],
        compiler_params=_cp("parallel", "arbitrary"),
        name="swiglu_up",
    )(x, gain.reshape(1, k).astype(F32), wg, wu)


def _ple_kernel(h_ref, p_ref, wp_ref, g_ref, wg_ref, o_ref):
    h = h_ref[...]
    e = jnp.dot(p_ref[...].astype(BF16), wp_ref[...], preferred_element_type=F32)
    e = _rms(e, g_ref[...])
    gate = jax.nn.sigmoid(jnp.dot(h.astype(BF16), wg_ref[...], preferred_element_type=F32))
    o_ref[...] = h + gate * e


def _ple(h, p, wp, gain, wg, *, tm):
    m, d = h.shape
    dp = p.shape[1]
    tm = min(tm, m)
    return pl.pallas_call(
        _ple_kernel,
        grid=(m // tm,),
        in_specs=[pl.BlockSpec((tm, d), lambda i: (i, 0)),
                  pl.BlockSpec((tm, dp), lambda i: (i, 0)),
                  pl.BlockSpec((dp, d), lambda i: (0, 0)),
                  pl.BlockSpec((1, d), lambda i: (0, 0)),
                  pl.BlockSpec((d, d), lambda i: (0, 0))],
        out_specs=pl.BlockSpec((tm, d), lambda i: (i, 0)),
        out_shape=jax.ShapeDtypeStruct((m, d), F32),
        compiler_params=_cp("parallel"),
        name="ple",
    )(h, p, wp, gain.reshape(1, d).astype(F32), wg)


def _conv_silu_slab(x, first, tail_ref, xbuf_ref, cw, bias):
    rows = x.shape[0]

    @pl.when(first)
    def _():
        tail_ref[...] = jnp.zeros_like(tail_ref)

    xbuf_ref[0:8, :] = tail_ref[...]
    xbuf_ref[8:8 + rows, :] = x
    tail_ref[...] = x[rows - 8:rows, :]
    y = (xbuf_ref[5:5 + rows, :] * cw[0:1, :] + xbuf_ref[6:6 + rows, :] * cw[1:2, :]
         + xbuf_ref[7:7 + rows, :] * cw[2:3, :] + x * cw[3:4, :])
    if bias is not None:
        y = y + bias
    return _silu(y)


def _gdn_kernel(qkv_ref, z_ref, sm_ref, cw_ref, vec_ref, ng_ref, o_ref, state_ref, tail_ref, xbuf_ref, xc_ref):
    rows = SEQ_BLOCK
    c = GDN_CHUNK
    first = pl.program_id(1) == 0

    @pl.when(first)
    def _():
        state_ref[...] = jnp.zeros_like(state_ref)

    for s in range(3):
        sl = slice(s * GDN_WIDTH, (s + 1) * GDN_WIDTH)
        xc_ref[:, sl] = _conv_silu_slab(qkv_ref[0, :, sl], first, tail_ref.at[s], xbuf_ref, cw_ref[:, sl], None)

    sm = sm_ref[0]
    beta_all = jax.nn.sigmoid(sm)
    g_all = vec_ref[0:1, :] * jax.nn.softplus(sm + vec_ref[1:2, :])

    row = lax.broadcasted_iota(I32, (rows, rows), 0)
    col = lax.broadcasted_iota(I32, (rows, rows), 1)
    same = (row // c) == (col // c)
    causal = same & (col <= row)
    strict = same & (col < row)
    tri01 = jnp.where(causal, 1.0, 0.0).astype(BF16)
    blk01 = jnp.where(same, 1.0, 0.0).astype(BF16)
    gc_all = _dot_exact_lhs(tri01, g_all)
    gl_all = _dot_exact_lhs(blk01, g_all)
    gc_all_t = gc_all.T

    heads = range(GDN_HEADS)
    eye = jnp.where(row == col, 1.0, 0.0).astype(F32)
    q_l, k_l, vb_l, kb_l, gc_l, gl_l, decay_l, p_l, x_l = [], [], [], [], [], [], [], [], []
    for h in heads:
        la = LANE_GDN_A + h
        q = xc_ref[:, h * GDN_DH:(h + 1) * GDN_DH]
        k = xc_ref[:, GDN_WIDTH + h * GDN_DH:GDN_WIDTH + (h + 1) * GDN_DH]
        v = xc_ref[:, 2 * GDN_WIDTH + h * GDN_DH:2 * GDN_WIDTH + (h + 1) * GDN_DH]
        q = q * lax.rsqrt(jnp.sum(q * q, axis=-1, keepdims=True) + EPS) * (GDN_DH ** -0.5)
        k = k * lax.rsqrt(jnp.sum(k * k, axis=-1, keepdims=True) + EPS)
        beta = beta_all[:, LANE_GDN_B + h:LANE_GDN_B + h + 1]
        gc = gc_all[:, la:la + 1]
        decay = jnp.where(causal, jnp.exp(gc - gc_all_t[la:la + 1, :]), 0.0)
        kb = k * beta
        m = jnp.where(strict, _bdot_nt(kb, k) * decay, 0.0)
        q_l.append(q), k_l.append(k), vb_l.append(v * beta), kb_l.append(kb)
        gc_l.append(gc), gl_l.append(gl_all[:, la:la + 1]), decay_l.append(decay)
        p_l.append(m), x_l.append(eye - m)

    span = 2
    while span < c:
        for h in heads:
            p_l[h] = _bdot(p_l[h], p_l[h])
        for h in heads:
            x_l[h] = x_l[h] + _bdot(x_l[h], p_l[h])
        span *= 2

    u_l, w_l, intra_l, qd_l, kd_l, ge_l = [], [], [], [], [], []
    for h in heads:
        eg = jnp.exp(gc_l[h])
        sol = _bdot(x_l[h], jnp.concatenate([vb_l[h], kb_l[h] * eg], axis=1))
        u_l.append(sol[:, :GDN_DH]), w_l.append(sol[:, GDN_DH:])
        intra_l.append((_bdot_nt(q_l[h], k_l[h]) * decay_l[h]).astype(BF16))
        qd_l.append(q_l[h] * eg)
        kd_l.append(k_l[h] * jnp.exp(gl_l[h] - gc_l[h]))
        ge_l.append(jnp.exp(gl_l[h]))

    st_l = [state_ref[h] for h in heads]
    out_l = [[] for _ in heads]
    for ci in range(rows // c):
        rs = slice(ci * c, (ci + 1) * c)
        for h in heads:
            st = st_l[h]
            v_new = u_l[h][rs] - _bdot(w_l[h][rs], st)
            pieces = []
            if ci > 0:
                pieces.append(jnp.zeros((ci * c, GDN_DH), F32))
            pieces.append(v_new)
            if (ci + 1) * c < rows:
                pieces.append(jnp.zeros((rows - (ci + 1) * c, GDN_DH), F32))
            v_pad = jnp.concatenate(pieces, axis=0) if len(pieces) > 1 else v_new
            out_l[h].append(_bdot(qd_l[h][rs], st)
                            + jnp.dot(intra_l[h][rs], v_pad.astype(BF16), preferred_element_type=F32))
            st_l[h] = st * ge_l[h][ci * c:ci * c + 1, :] + _bdot_tn(kd_l[h][rs], v_new)

    for h in heads:
        hs = slice(h * GDN_DH, (h + 1) * GDN_DH)
        state_ref[h] = st_l[h]
        o = jnp.concatenate(out_l[h], axis=0)
        o = _rms(o, ng_ref[...]) * _silu(z_ref[0, :, hs])
        o_ref[0, :, hs] = o.astype(o_ref.dtype)


def _gdn(u3, conv_w, a_log, dt_bias, norm_g):
    b, s, _ = u3.shape
    cw = conv_w.T.astype(F32)
    vec = jnp.zeros((2, LANES), F32)
    vec = vec.at[0, LANE_GDN_A:LANE_GDN_A + GDN_HEADS].set(-jnp.exp(a_log.astype(F32)))
    vec = vec.at[1, LANE_GDN_A:LANE_GDN_A + GDN_HEADS].set(dt_bias.astype(F32))
    blk = SEQ_BLOCK
    return pl.pallas_call(
        _gdn_kernel,
        grid=(b, s // blk),
        in_specs=[pl.BlockSpec((1, blk, 3 * GDN_WIDTH), lambda i, j: (i, j, COL_GDN_QKV // (3 * GDN_WIDTH))),
                  pl.BlockSpec((1, blk, GDN_WIDTH), lambda i, j: (i, j, COL_GDN_Z // GDN_WIDTH)),
                  pl.BlockSpec((1, blk, LANES), lambda i, j: (i, j, COL_SMALL // LANES)),
                  pl.BlockSpec((4, 3 * GDN_WIDTH), lambda i, j: (0, 0)),
                  pl.BlockSpec((2, LANES), lambda i, j: (0, 0)),
                  pl.BlockSpec((1, GDN_DH), lambda i, j: (0, 0))],
        out_specs=pl.BlockSpec((1, blk, GDN_WIDTH), lambda i, j: (i, j, 0)),
        out_shape=jax.ShapeDtypeStruct((b, s, GDN_WIDTH), BF16),
        scratch_shapes=[pltpu.VMEM((GDN_HEADS, GDN_DH, GDN_DH), F32),
                        pltpu.VMEM((3, 8, GDN_WIDTH), F32),
                        pltpu.VMEM((blk + 8, GDN_WIDTH), F32),
                        pltpu.VMEM((blk, 3 * GDN_WIDTH), F32)],
        compiler_params=_cp("parallel", "arbitrary"),
        name="gdn",
    )(u3, u3, u3, cw, vec, norm_g.reshape(1, GDN_DH).astype(F32))


def _ssd_kernel(z_ref, xbc_ref, sm_ref, cw_ref, cb_ref, vec_ref, dsk_ref, ng_ref, o_ref,
                state_ref, tail_ref, xbuf_ref, xc_ref, y_ref):
    rows = SEQ_BLOCK
    first = pl.program_id(1) == 0
    half = SSD_P

    @pl.when(first)
    def _():
        state_ref[...] = jnp.zeros_like(state_ref)

    slab = 512
    for s in range(SSD_CONV_DIM // slab):
        sl = slice(s * slab, (s + 1) * slab)
        xc_ref[:, sl] = _conv_silu_slab(xbc_ref[0, :, sl], first, tail_ref.at[s], xbuf_ref, cw_ref[:, sl], cb_ref[:, sl])

    dt_all = jax.nn.softplus(sm_ref[0] + vec_ref[1:2, :])
    da_all = dt_all * vec_ref[0:1, :]
    row = lax.broadcasted_iota(I32, (rows, rows), 0)
    col = lax.broadcasted_iota(I32, (rows, rows), 1)
    causal = col <= row
    tri01 = jnp.where(causal, 1.0, 0.0).astype(BF16)
    acs = _dot_exact_lhs(tri01, da_all)
    acs_t = acs.T
    lane = lax.broadcasted_iota(I32, (rows, LANES), 1)
    lo = lane < half
    lane1 = lax.broadcasted_iota(I32, (1, LANES), 1)
    lo1 = lane1 < half

    d = functools.partial(jnp.dot, preferred_element_type=F32)
    pairs_per_group = SSD_HEADS // SSD_GROUPS // 2
    n_pairs = SSD_HEADS // 2
    cb_l, bmt_l, cmb_l = [], [], []
    for g in range(SSD_GROUPS):
        bm = xc_ref[:, SSD_WIDTH + g * SSD_N:SSD_WIDTH + (g + 1) * SSD_N]
        cm = xc_ref[:, SSD_WIDTH + (SSD_GROUPS + g) * SSD_N:SSD_WIDTH + (SSD_GROUPS + g + 1) * SSD_N]
        cb_l.append(_bdot_nt(cm, bm))
        bmt_l.append(bm.T)
        cmb_l.append(cm.astype(BF16))

    att_l, dec_l, xdt_l, x_l, expa_l, ge_l = [], [], [], [], [], []
    for p in range(n_pairs):
        g = p // pairs_per_group
        la = LANE_SSD_DT + 2 * p
        lb = la + 1
        col_a, col_b = acs[:, la:la + 1], acs[:, lb:lb + 1]
        row_a, row_b = acs_t[la:la + 1, :], acs_t[lb:lb + 1, :]
        last_a, last_b = row_a[:, rows - 1:rows], row_b[:, rows - 1:rows]
        att_l.append(((cb_l[g] * jnp.where(causal, jnp.exp(col_a - row_a), 0.0)).astype(BF16),
                      (cb_l[g] * jnp.where(causal, jnp.exp(col_b - row_b), 0.0)).astype(BF16)))
        dec_l.append(((bmt_l[g] * jnp.exp(last_a - row_a)).astype(BF16),
                      (bmt_l[g] * jnp.exp(last_b - row_b)).astype(BF16)))
        x_pair = xc_ref[:, p * LANES:(p + 1) * LANES]
        xdt = x_pair * jnp.where(lo, dt_all[:, la:la + 1], dt_all[:, lb:lb + 1])
        xdt_l.append((jnp.where(lo, xdt, 0.0).astype(BF16), jnp.where(lo, 0.0, xdt).astype(BF16)))
        x_l.append(x_pair)
        expa_l.append(jnp.where(lo, jnp.exp(col_a), jnp.exp(col_b)))
        ge_l.append(jnp.where(lo1, jnp.exp(last_a), jnp.exp(last_b)))

    for p in range(n_pairs):
        g = p // pairs_per_group
        ps = slice(p * LANES, (p + 1) * LANES)
        xdt_a, xdt_b = xdt_l[p]
        y_diag = d(att_l[p][0], xdt_a) + d(att_l[p][1], xdt_b)
        st_new = d(dec_l[p][0], xdt_a) + d(dec_l[p][1], xdt_b)
        prev = state_ref[p]
        y_off = d(cmb_l[g], prev.astype(BF16)) * expa_l[p]
        state_ref[p] = prev * ge_l[p] + st_new
        y_ref[:, ps] = y_diag + y_off + x_l[p] * dsk_ref[:, ps]

    gw = SSD_WIDTH // SSD_GROUPS
    for g in range(SSD_GROUPS):
        gs = slice(g * gw, (g + 1) * gw)
        y = y_ref[:, gs] * _silu(z_ref[0, :, gs])
        o_ref[0, :, gs] = _rms(y, ng_ref[:, gs]).astype(o_ref.dtype)


def _ssd(u3, conv_w, conv_b, a_log, dt_bias, d_skip, norm_g):
    b, s, _ = u3.shape
    cw = conv_w.T.astype(F32)
    vec = jnp.zeros((2, LANES), F32)
    vec = vec.at[0, LANE_SSD_DT:LANE_SSD_DT + SSD_HEADS].set(-jnp.exp(a_log.astype(F32)))
    vec = vec.at[1, LANE_SSD_DT:LANE_SSD_DT + SSD_HEADS].set(dt_bias.astype(F32))
    dsk = jnp.repeat(d_skip.astype(F32), SSD_P).reshape(1, SSD_WIDTH)
    blk = SEQ_BLOCK
    return pl.pallas_call(
        _ssd_kernel,
        grid=(b, s // blk),
        in_specs=[pl.BlockSpec((1, blk, SSD_WIDTH), lambda i, j: (i, j, COL_SSD_Z // SSD_WIDTH)),
                  pl.BlockSpec((1, blk, SSD_CONV_DIM), lambda i, j: (i, j, COL_SSD_XBC // SSD_CONV_DIM)),
                  pl.BlockSpec((1, blk, LANES), lambda i, j: (i, j, COL_SMALL // LANES)),
                  pl.BlockSpec((4, SSD_CONV_DIM), lambda i, j: (0, 0)),
                  pl.BlockSpec((1, SSD_CONV_DIM), lambda i, j: (0, 0)),
                  pl.BlockSpec((2, LANES), lambda i, j: (0, 0)),
                  pl.BlockSpec((1, SSD_WIDTH), lambda i, j: (0, 0)),
                  pl.BlockSpec((1, SSD_WIDTH), lambda i, j: (0, 0))],
        out_specs=pl.BlockSpec((1, blk, SSD_WIDTH), lambda i, j: (i, j, 0)),
        out_shape=jax.ShapeDtypeStruct((b, s, SSD_WIDTH), BF16),
        scratch_shapes=[pltpu.VMEM((SSD_HEADS // 2, SSD_N, 2 * SSD_P), F32),
                        pltpu.VMEM((SSD_CONV_DIM // 512, 8, 512), F32),
                        pltpu.VMEM((blk + 8, 512), F32),
                        pltpu.VMEM((blk, SSD_CONV_DIM), F32),
                        pltpu.VMEM((blk, SSD_WIDTH), F32)],
        compiler_params=_cp("parallel", "arbitrary"),
        name="ssd",
    )(u3, u3, u3, cw, conv_b.reshape(1, SSD_CONV_DIM).astype(F32), vec, dsk,
      norm_g.reshape(1, SSD_WIDTH).astype(F32))


def _mla_pre_kernel(ql_ref, kpe_ref, kvl_ref, pos_ref, qag_ref, kvag_ref, wq_ref, wkv_ref, vec_ref,
                    q_ref, k_ref, v_ref):
    qn = _rms(ql_ref[0], qag_ref[...])
    qall = jnp.dot(qn.astype(BF16), wq_ref[...], preferred_element_type=F32)
    kvn = _rms(kvl_ref[0], kvag_ref[...])
    kv = jnp.dot(kvn.astype(BF16), wkv_ref[...], preferred_element_type=F32)
    kpe = kpe_ref[0]
    ang = pos_ref[0] * vec_ref[6:7, :]
    cos = jnp.cos(ang)
    sin = jnp.sin(ang) * vec_ref[7:8, :]
    gqn, gqa, gqb = vec_ref[0:1, :], vec_ref[1:2, :], vec_ref[2:3, :]
    gkn, gka, gkb = vec_ref[3:4, :], vec_ref[4:5, :], vec_ref[5:6, :]
    scale = MLA_QK ** -0.5 * LOG2_E
    k_rot = kpe * (cos * gka) + pltpu.roll(kpe, 64, 1) * (sin * gkb)
    k_pe_ss = 0.5 * jnp.sum(kpe * kpe, axis=-1, keepdims=True)
    nh = MLA_HEADS
    for h in range(nh):
        q_nope = qall[:, h * LANES:(h + 1) * LANES]
        q_pe = qall[:, (nh + h) * LANES:(nh + h + 1) * LANES]
        ss = jnp.sum(q_nope * q_nope, axis=-1, keepdims=True) + 0.5 * jnp.sum(q_pe * q_pe, axis=-1, keepdims=True)
        rstd = lax.rsqrt(ss * (1.0 / MLA_QK) + EPS)
        q_rot = q_pe * (cos * gqa) + pltpu.roll(q_pe, 64, 1) * (sin * gqb)
        q_ref[0, h, :, 0:LANES] = (q_nope * gqn * rstd * scale).astype(q_ref.dtype)
        q_ref[0, h, :, LANES:2 * LANES] = (q_rot * rstd * (0.5 * scale)).astype(q_ref.dtype)
        k_nope = kv[:, 2 * h * LANES:(2 * h + 1) * LANES]
        ssk = jnp.sum(k_nope * k_nope, axis=-1, keepdims=True) + k_pe_ss
        rstdk = lax.rsqrt(ssk * (1.0 / MLA_QK) + EPS)
        k_ref[0, h, :, 0:LANES] = (k_nope * gkn * rstdk).astype(k_ref.dtype)
        k_ref[0, h, :, LANES:2 * LANES] = (k_rot * rstdk).astype(k_ref.dtype)
        v_ref[0, h, :, 0:LANES] = kv[:, (2 * h + 1) * LANES:(2 * h + 2) * LANES].astype(v_ref.dtype)
        v_ref[0, h, :, LANES:2 * LANES] = jnp.ones((kv.shape[0], LANES), v_ref.dtype)


def _rope_pair_gains(g):
    g1, g2 = g[MLA_NOPE:MLA_NOPE + 32], g[MLA_NOPE + 32:MLA_NOPE + 64]
    return jnp.concatenate([g1, g2, g2, g1]), jnp.concatenate([g2, g1, g1, g2])


def _mla_pre(u3, pos_f, q_a_g, w_q_b, kv_a_g, w_kv_b, q_norm_g, k_norm_g, *, ts):
    b, s, _ = u3.shape
    nh = MLA_HEADS
    wq = w_q_b.reshape(MLA_Q_RANK, nh, MLA_QK)
    x1, x2 = wq[:, :, MLA_NOPE:MLA_NOPE + 32], wq[:, :, MLA_NOPE + 32:]
    wq_all = jnp.concatenate([wq[:, :, :MLA_NOPE].reshape(MLA_Q_RANK, nh * MLA_NOPE),
                              jnp.concatenate([x1, x2, x2, x1], axis=-1).reshape(MLA_Q_RANK, nh * LANES)],
                             axis=1).astype(BF16)
    half = MLA_ROPE // 2
    inv_freq = 1.0 / (ROPE_THETA ** (jnp.arange(half, dtype=F32) / half))
    gqa, gqb = _rope_pair_gains(q_norm_g.astype(F32))
    gka, gkb = _rope_pair_gains(k_norm_g.astype(F32))
    ones = jnp.ones((half,), F32)
    vec = jnp.stack([q_norm_g[:MLA_NOPE].astype(F32), gqa, gqb, k_norm_g[:MLA_NOPE].astype(F32), gka, gkb,
                     jnp.tile(inv_freq, 4), jnp.concatenate([-ones, ones, ones, -ones])])
    ts = min(ts, s)
    qk_shape = jax.ShapeDtypeStruct((b, nh, s, 2 * LANES), BF16)
    return pl.pallas_call(
        _mla_pre_kernel,
        grid=(b, s // ts),
        in_specs=[pl.BlockSpec((1, ts, MLA_Q_RANK), lambda i, j: (i, j, COL_QLAT // MLA_Q_RANK)),
                  pl.BlockSpec((1, ts, LANES), lambda i, j: (i, j, COL_KPE // LANES)),
                  pl.BlockSpec((1, ts, MLA_KV_RANK), lambda i, j: (i, j, COL_KVLAT // MLA_KV_RANK)),
                  pl.BlockSpec((1, ts, 1), lambda i, j: (i, j, 0)),
                  pl.BlockSpec((1, MLA_Q_RANK), lambda i, j: (0, 0)),
                  pl.BlockSpec((1, MLA_KV_RANK), lambda i, j: (0, 0)),
                  pl.BlockSpec((MLA_Q_RANK, 2 * nh * LANES), lambda i, j: (0, 0)),
                  pl.BlockSpec((MLA_KV_RANK, 2 * nh * LANES), lambda i, j: (0, 0)),
                  pl.BlockSpec((8, LANES), lambda i, j: (0, 0))],
        out_specs=[pl.BlockSpec((1, nh, ts, 2 * LANES), lambda i, j: (i, 0, j, 0)),
                   pl.BlockSpec((1, nh, ts, 2 * LANES), lambda i, j: (i, 0, j, 0)),
                   pl.BlockSpec((1, nh, ts, 2 * LANES), lambda i, j: (i, 0, j, 0))],
        out_shape=[qk_shape, qk_shape, qk_shape],
        compiler_params=_cp("parallel", "parallel"),
        name="mla_pre",
    )(u3, u3, u3, pos_f, q_a_g.reshape(1, -1).astype(F32), kv_a_g.reshape(1, -1).astype(F32),
      wq_all, w_kv_b.astype(BF16), vec)


def _flash_kernel(qi_ref, kj_ref, q_ref, k_ref, v_ref, o_ref, m_ref, acc_ref, *, t):
    qi = qi_ref[pl.program_id(1)]
    kj = kj_ref[pl.program_id(1)]
    nh = q_ref.shape[1]

    @pl.when(kj == 0)
    def _():
        m_ref[...] = jnp.full_like(m_ref, -jnp.inf)
        acc_ref[...] = jnp.zeros_like(acc_ref)

    def step(diagonal):
        s_l = [lax.dot_general(q_ref[0, h], k_ref[0, h], (((1,), (1,)), ((), ())), preferred_element_type=F32)
               for h in range(nh)]
        if diagonal:
            keep = lax.broadcasted_iota(I32, (t, t), 0) >= lax.broadcasted_iota(I32, (t, t), 1)
            s_l = [jnp.where(keep, s, -jnp.inf) for s in s_l]
        p_l, alpha_l = [], []
        for h in range(nh):
            m_old = m_ref[h]
            m_new = jnp.maximum(m_old, jnp.max(s_l[h], axis=-1, keepdims=True))
            p_l.append(jnp.exp2(s_l[h] - jnp.tile(m_new, (1, t // LANES))).astype(BF16))
            alpha_l.append(jnp.exp2(m_old - m_new))
            m_ref[h] = m_new
        for h in range(nh):
            pv = jnp.dot(p_l[h], v_ref[0, h], preferred_element_type=F32)
            acc_ref[h] = jnp.tile(alpha_l[h], (1, 2)) * acc_ref[h] + pv

    @pl.when(kj < qi)
    def _():
        step(False)

    @pl.when(kj == qi)
    def _():
        step(True)
        for h in range(nh):
            acc = acc_ref[h]
            o_ref[0, :, h * LANES:(h + 1) * LANES] = (acc[:, :LANES] / acc[:, LANES:]).astype(o_ref.dtype)


def _flash(q, k, v, *, t):
    b, nh, s, dq = q.shape
    t = min(t, s)
    n = s // t
    pairs = [(qi, kj) for qi in range(n) for kj in range(qi + 1)]
    qi_tab = jnp.asarray([pr[0] for pr in pairs], I32)
    kj_tab = jnp.asarray([pr[1] for pr in pairs], I32)
    kv_spec = pl.BlockSpec((1, nh, t, dq), lambda i, pr, qt, kt: (i, 0, kt[pr], 0))
    grid_spec = pltpu.PrefetchScalarGridSpec(
        num_scalar_prefetch=2,
        grid=(b, len(pairs)),
        in_specs=[pl.BlockSpec((1, nh, t, dq), lambda i, pr, qt, kt: (i, 0, qt[pr], 0)), kv_spec, kv_spec],
        out_specs=pl.BlockSpec((1, t, nh * LANES), lambda i, pr, qt, kt: (i, qt[pr], 0)),
        scratch_shapes=[pltpu.VMEM((nh, t, LANES), F32), pltpu.VMEM((nh, t, 2 * LANES), F32)],
    )
    return pl.pallas_call(
        functools.partial(_flash_kernel, t=t),
        grid_spec=grid_spec,
        out_shape=jax.ShapeDtypeStruct((b, s, nh * LANES), BF16),
        compiler_params=_cp("parallel", "arbitrary"),
        name="mla_flash",
    )(qi_tab, kj_tab, q, k, v)


def _router_kernel(h_ref, g_ref, wr_ref, idx_ref, wt_ref):
    xn = _rms(h_ref[...], g_ref[...])
    logits = jnp.dot(xn, wr_ref[...], precision=lax.Precision.HIGHEST, preferred_element_type=F32)
    lane = lax.broadcasted_iota(I32, logits.shape, 1)
    logits = jnp.where(lane < N_EXPERTS, logits, -jnp.inf)
    m1 = jnp.max(logits, axis=-1, keepdims=True)
    i1 = jnp.min(jnp.where(logits == m1, lane, LANES), axis=-1, keepdims=True)
    rest = jnp.where(lane == i1, -jnp.inf, logits)
    m2 = jnp.max(rest, axis=-1, keepdims=True)
    i2 = jnp.min(jnp.where(rest == m2, lane, LANES), axis=-1, keepdims=True)
    e2 = jnp.exp(m2 - m1)
    w1 = 1.0 / (1.0 + e2)
    w2 = e2 / (1.0 + e2)
    idx_ref[...] = jnp.where(lane == 0, i1, jnp.where(lane == 1, i2, 0))
    wt_ref[...] = jnp.where(lane == 0, w1, jnp.where(lane == 1, w2, 0.0))


def _router(h, gain, w_router, *, tm):
    m, d = h.shape
    tm = min(tm, m)
    wr = jnp.zeros((d, LANES), F32).at[:, :N_EXPERTS].set(w_router.astype(F32))
    return pl.pallas_call(
        _router_kernel,
        grid=(m // tm,),
        in_specs=[pl.BlockSpec((tm, d), lambda i: (i, 0)),
                  pl.BlockSpec((1, d), lambda i: (0, 0)),
                  pl.BlockSpec((d, LANES), lambda i: (0, 0))],
        out_specs=[pl.BlockSpec((tm, LANES), lambda i: (i, 0)),
                   pl.BlockSpec((tm, LANES), lambda i: (i, 0))],
        out_shape=[jax.ShapeDtypeStruct((m, LANES), I32),
                   jax.ShapeDtypeStruct((m, LANES), F32)],
        compiler_params=_cp("parallel"),
        name="moe_router",
    )(h, gain.reshape(1, d).astype(F32), wr)


def _row_copy(src_ref, t, dst_ref, r, sem):
    return pltpu.make_async_copy(src_ref.at[pl.ds(t, 1), :], dst_ref.at[pl.ds(r, 1), :], sem)


def _gather_kernel(idx_ref, src_ref, g_ref, o_ref, buf_ref, sem):
    rows = buf_ref.shape[1]
    i = pl.program_id(0)
    n = pl.num_programs(0)

    def start_step(step, slot):
        def issue(r2, c):
            for k in range(2):
                r = 2 * r2 + k
                _row_copy(src_ref, idx_ref[step * rows + r], buf_ref.at[slot], r, sem.at[slot]).start(priority=k)
            return c
        lax.fori_loop(0, rows // 2, issue, 0, unroll=4)

    @pl.when(i == 0)
    def _():
        start_step(0, 0)

    @pl.when(i + 1 < n)
    def _():
        start_step(i + 1, (i + 1) % 2)

    slot = i % 2

    def wait(r, c):
        _row_copy(src_ref, 0, buf_ref.at[slot], r, sem.at[slot]).wait()
        return c

    lax.fori_loop(0, rows, wait, 0, unroll=8)
    o_ref[...] = _rms(buf_ref[slot], g_ref[...]).astype(o_ref.dtype)


def _gather_norm_rows(src_tok, h, gain):
    p = src_tok.shape[0]
    d = h.shape[1]
    rows = GATHER_ROWS
    grid_spec = pltpu.PrefetchScalarGridSpec(
        num_scalar_prefetch=1,
        grid=(p // rows,),
        in_specs=[pl.BlockSpec(memory_space=pl.ANY), pl.BlockSpec((1, d), lambda i, idx: (0, 0))],
        out_specs=pl.BlockSpec((rows, d), lambda i, idx: (i, 0)),
        scratch_shapes=[pltpu.VMEM((2, rows, d), F32), pltpu.SemaphoreType.DMA((2,))],
    )
    return pl.pallas_call(
        _gather_kernel,
        grid_spec=grid_spec,
        out_shape=jax.ShapeDtypeStruct((p, d), BF16),
        compiler_params=_cp("arbitrary"),
        name="moe_gather",
    )(src_tok, h, gain.reshape(1, d).astype(F32))


def _gup_kernel(te_ref, nu_ref, x_ref, wg_ref, wu_ref, wd_ref, o_ref, wdb_ref, wgb_ref, wub_ref):
    i = pl.program_id(1)
    used = i < nu_ref[0]

    wdb_ref[...] = wd_ref[...].astype(BF16)

    @pl.when((i == 0) | (te_ref[i] != te_ref[jnp.maximum(i - 1, 0)]))
    def _():
        wgb_ref[...] = wg_ref[0].astype(BF16)
        wub_ref[...] = wu_ref[0].astype(BF16)

    @pl.when(used)
    def _():
        x = x_ref[...]
        for c in range(0, o_ref.shape[1], MXU_COLS):
            cs = slice(c, c + MXU_COLS)
            gate = jnp.dot(x, wgb_ref[:, cs], preferred_element_type=F32)
            up = jnp.dot(x, wub_ref[:, cs], preferred_element_type=F32)
            o_ref[:, cs] = (_silu(gate) * up).astype(o_ref.dtype)

    @pl.when(jnp.logical_not(used))
    def _():
        o_ref[...] = jnp.zeros_like(o_ref)


def _grouped_up(tile_expert, n_used, xs, wg, wu, wd, *, tm, tf):
    p, d = xs.shape
    f = wg.shape[2]
    n_i = p // tm
    n_steps = (f // tf) * n_i
    wd2 = wd.reshape(-1, wd.shape[-1])
    bf16_sublanes = 16
    slab = next(r for r in range(bf16_sublanes, wd2.shape[0] + 1, bf16_sublanes)
                if wd2.shape[0] % r == 0 and wd2.shape[0] // r <= n_steps)
    n_slabs = wd2.shape[0] // slab

    def slab_map(j, i, te, nu):
        return (jnp.minimum(j * n_i + i, n_slabs - 1), 0)

    grid_spec = pltpu.PrefetchScalarGridSpec(
        num_scalar_prefetch=2,
        grid=(f // tf, n_i),
        in_specs=[pl.BlockSpec((tm, d), lambda j, i, te, nu: (jnp.minimum(i, nu[0] - 1), 0)),
                  pl.BlockSpec((1, d, tf), lambda j, i, te, nu: (te[i], 0, j)),
                  pl.BlockSpec((1, d, tf), lambda j, i, te, nu: (te[i], 0, j)),
                  pl.BlockSpec((slab, wd2.shape[1]), slab_map)],
        out_specs=[pl.BlockSpec((tm, tf), lambda j, i, te, nu: (i, j)),
                   pl.BlockSpec((slab, wd2.shape[1]), slab_map)],
        scratch_shapes=[pltpu.VMEM((d, tf), BF16), pltpu.VMEM((d, tf), BF16)],
    )
    hff, wdb = pl.pallas_call(
        _gup_kernel,
        grid_spec=grid_spec,
        out_shape=[jax.ShapeDtypeStruct((p, f), BF16), jax.ShapeDtypeStruct(wd2.shape, BF16)],
        compiler_params=_cp("arbitrary", "arbitrary"),
        name="moe_up",
    )(tile_expert, n_used, xs, wg, wu, wd2)
    return hff, wdb.reshape(wd.shape)


def _gdown_kernel(te_ref, nu_ref, x_ref, wd_ref, o_ref):
    used = pl.program_id(1) < nu_ref[0]

    @pl.when(used)
    def _():
        o_ref[...] = jnp.dot(x_ref[...], wd_ref[0], preferred_element_type=F32)

    @pl.when(jnp.logical_not(used))
    def _():
        o_ref[...] = jnp.zeros_like(o_ref)


def _grouped_down(tile_expert, n_used, hff, wd, *, tm, tn):
    p, f = hff.shape
    d = wd.shape[2]
    grid_spec = pltpu.PrefetchScalarGridSpec(
        num_scalar_prefetch=2,
        grid=(d // tn, p // tm),
        in_specs=[pl.BlockSpec((tm, f), lambda j, i, te, nu: (jnp.minimum(i, nu[0] - 1), 0)),
                  pl.BlockSpec((1, f, tn), lambda j, i, te, nu: (te[i], 0, j))],
        out_specs=pl.BlockSpec((tm, tn), lambda j, i, te, nu: (i, j)),
    )
    return pl.pallas_call(
        _gdown_kernel,
        grid_spec=grid_spec,
        out_shape=jax.ShapeDtypeStruct((p, d), F32),
        compiler_params=_cp("arbitrary", "arbitrary"),
        name="moe_down",
    )(tile_expert, n_used, hff, wd)


def _combine_kernel(pos_ref, h_ref, wt_ref, y_ref, o_ref, buf_ref, sem):
    rows = h_ref.shape[0]
    i = pl.program_id(0)
    n = pl.num_programs(0)

    def start_step(step, slot):
        def issue(r, c):
            base = 2 * (step * rows + r)
            _row_copy(y_ref, pos_ref[base], buf_ref.at[slot, 0], r, sem.at[slot]).start(priority=0)
            _row_copy(y_ref, pos_ref[base + 1], buf_ref.at[slot, 1], r, sem.at[slot]).start(priority=1)
            return c
        lax.fori_loop(0, rows, issue, 0, unroll=4)

    @pl.when(i == 0)
    def _():
        start_step(0, 0)

    @pl.when(i + 1 < n)
    def _():
        start_step(i + 1, (i + 1) % 2)

    slot = i % 2

    def wait(r, c):
        _row_copy(y_ref, 0, buf_ref.at[slot, 0], r, sem.at[slot]).wait()
        _row_copy(y_ref, 0, buf_ref.at[slot, 1], r, sem.at[slot]).wait()
        return c

    lax.fori_loop(0, rows, wait, 0, unroll=4)
    wt = wt_ref[...]
    o_ref[...] = h_ref[...] + wt[:, 0:1] * buf_ref[slot, 0] + wt[:, 1:2] * buf_ref[slot, 1]


def _combine(h, y, pos, wts):
    m, d = h.shape
    rows = min(COMBINE_ROWS, m)
    grid_spec = pltpu.PrefetchScalarGridSpec(
        num_scalar_prefetch=1,
        grid=(m // rows,),
        in_specs=[pl.BlockSpec((rows, d), lambda i, pos: (i, 0)),
                  pl.BlockSpec((rows, LANES), lambda i, pos: (i, 0)),
                  pl.BlockSpec(memory_space=pl.ANY)],
        out_specs=pl.BlockSpec((rows, d), lambda i, pos: (i, 0)),
        scratch_shapes=[pltpu.VMEM((2, 2, rows, d), F32), pltpu.SemaphoreType.DMA((2,))],
    )
    return pl.pallas_call(
        _combine_kernel,
        grid_spec=grid_spec,
        out_shape=jax.ShapeDtypeStruct((m, d), F32),
        compiler_params=_cp("arbitrary"),
        name="moe_combine",
    )(pos, h, wts, y)


def _moe(h, gain, w_router, wg, wu, wd):
    t, d = h.shape
    tm = MOE_TM
    n2 = 2 * t
    idx, wts = _router(h, gain, w_router, tm=512)
    flat_e = idx[:, :2].reshape(-1)
    onehot = (flat_e[:, None] == jnp.arange(N_EXPERTS, dtype=I32)[None, :]).astype(I32)
    csum = jnp.cumsum(onehot, axis=0)
    rank = jnp.sum((csum - onehot) * onehot, axis=1)
    counts = csum[-1]
    padded = ((counts + tm - 1) // tm) * tm
    ends = jnp.cumsum(padded)
    starts = ends - padded
    pos = (starts[flat_e] + rank).astype(I32)
    p_rows = n2 + N_EXPERTS * tm
    n_tiles = p_rows // tm
    n_used = (ends[-1] // tm).astype(I32).reshape(1)
    tile_start = jnp.arange(n_tiles, dtype=I32) * tm
    tile_expert = jnp.sum((tile_start[:, None] >= ends[None, :]).astype(I32), axis=1)
    last_expert = jnp.sum((ends[-1] - 1 >= ends).astype(I32))
    tile_expert = jnp.minimum(tile_expert, last_expert).astype(I32)
    order = jnp.sort(flat_e * n2 + jnp.arange(n2, dtype=I32)) % n2
    row = jnp.arange(p_rows, dtype=I32)
    row_e = jnp.repeat(tile_expert, tm)
    local = row - starts[row_e]
    first = (jnp.cumsum(counts) - counts)[row_e]
    src_tok = jnp.where(local < counts[row_e], order[jnp.clip(first + local, 0, n2 - 1)] // 2, 0).astype(I32)

    xs = _gather_norm_rows(src_tok, h, gain)
    hff, wd_bf16 = _grouped_up(tile_expert, n_used, xs, wg, wu, wd, tm=tm, tf=1024)
    y = _grouped_down(tile_expert, n_used, hff, wd_bf16, tm=tm, tn=512)
    return _combine(h, y, pos, wts)


def _rearranged_w_in(w_in):
    d = w_in.shape[0]
    gdn0, ssd0 = 0, 4 * GDN_WIDTH + 2 * GDN_HEADS
    mla0 = ssd0 + SSD_WIDTH + SSD_CONV_DIM + SSD_HEADS
    kpe0 = mla0 + MLA_Q_RANK + MLA_KV_RANK
    gb0 = gdn0 + 4 * GDN_WIDTH
    dt0 = ssd0 + SSD_WIDTH + SSD_CONV_DIM
    half = MLA_ROPE // 2
    zeros = lambda n: jnp.zeros((d, n), w_in.dtype)
    cols = lambda a, n: w_in[:, a:a + n]
    out = jnp.concatenate([
        cols(gdn0, 4 * GDN_WIDTH),
        cols(ssd0, SSD_WIDTH + SSD_CONV_DIM),
        cols(mla0, MLA_Q_RANK),
        cols(kpe0, MLA_ROPE), cols(kpe0 + half, half), cols(kpe0, half),
        cols(mla0 + MLA_Q_RANK, MLA_KV_RANK),
        zeros(LANE_GDN_B), cols(gb0, 2 * GDN_HEADS), cols(dt0, SSD_HEADS),
        zeros(LANES - LANE_SSD_DT - SSD_HEADS),
        zeros(LANES),
    ], axis=1)
    assert out.shape[1] == U_COLS and LANE_GDN_A == LANE_GDN_B + GDN_HEADS and LANE_SSD_DT == LANE_GDN_A + GDN_HEADS
    return out.astype(BF16)


def kernel(x, p, positions, norm_mix_g, w_in, w_out, gdn_conv_w, gdn_a_log, gdn_dt_bias, gdn_norm_g, ssd_conv_w, ssd_conv_b, ssd_a_log, ssd_dt_bias, ssd_d, ssd_norm_g, mla_q_a_g, mla_w_q_b, mla_kv_a_g, mla_w_kv_b, mla_q_norm_g, mla_k_norm_g, norm_ffn_g, ffn_w_gate, ffn_w_up, ffn_w_down, router_w, moe_w_gate, moe_w_up, moe_w_down, ple_w_proj, ple_w_gate, ple_norm_g):
    b, s, d = x.shape
    t = b * s
    depth = w_in.shape[0]
    h = x.reshape(t, d).astype(F32)
    pos_f = positions.astype(F32).reshape(b, s, 1)
    for i in range(depth):
        u = _mm(h, _rearranged_w_in(w_in[i]), gain=norm_mix_g[i], tm=1024, tn=1408, name="in_proj")
        u3 = u.reshape(b, s, U_COLS)
        o_gdn = _gdn(u3, gdn_conv_w[i], gdn_a_log[i], gdn_dt_bias[i], gdn_norm_g[i])
        o_ssd = _ssd(u3, ssd_conv_w[i], ssd_conv_b[i], ssd_a_log[i], ssd_dt_bias[i], ssd_d[i], ssd_norm_g[i])
        q, k, v = _mla_pre(u3, pos_f, mla_q_a_g[i], mla_w_q_b[i], mla_kv_a_g[i], mla_w_kv_b[i],
                           mla_q_norm_g[i], mla_k_norm_g[i], ts=512)
        o_mla = _flash(q, k, v, t=512)
        mix_parts = [o.reshape(t, o.shape[-1]) for o in (o_gdn, o_ssd, o_mla)]
        h = _mm_parts(mix_parts, w_out[i].astype(BF16), h, tm=1024, tn=1024, name="out_proj")
        j = i // 2
        if i % 2 == 0:
            ff = _swiglu_up(h, norm_ffn_g[i], ffn_w_gate[j].astype(BF16), ffn_w_up[j].astype(BF16), tm=1024, tn=512)
            h = _mm(ff, ffn_w_down[j].astype(BF16), res=h, tm=1024, tn=512, name="ffn_down")
        else:
            h = _moe(h, norm_ffn_g[i], router_w[j], moe_w_gate[j], moe_w_up[j], moe_w_down[j])
        h = _ple(h, p[i].reshape(t, -1), ple_w_proj[i].astype(BF16), ple_norm_g[i], ple_w_gate[i].astype(BF16), tm=512)
    return h.reshape(b, s, d).astype(x.dtype)
```

```python
import functools

import jax
import jax.numpy as jnp
from jax import lax
from jax.experimental import pallas as pl
from jax.experimental.pallas import tpu as pltpu

F32 = jnp.float32
BF16 = jnp.bfloat16
U32 = jnp.uint32
I32 = jnp.int32
EPS = 1e-6

D_MODEL = 2048
GDN_HEADS, GDN_DH, GDN_CHUNK = 4, 128, 64
GDN_WIDTH = GDN_HEADS * GDN_DH
SSD_HEADS, SSD_P, SSD_GROUPS, SSD_N, SSD_CHUNK = 16, 64, 2, 128, 256
SSD_WIDTH = SSD_HEADS * SSD_P
SSD_CONV_DIM = SSD_WIDTH + 2 * SSD_GROUPS * SSD_N
MLA_HEADS, MLA_Q_RANK, MLA_KV_RANK = 4, 384, 256
MLA_NOPE, MLA_ROPE, MLA_V = 128, 64, 128
MLA_QK = MLA_NOPE + MLA_ROPE
ROPE_THETA = 10000.0
N_EXPERTS = 8
LANES = 128
MXU_COLS = 256
LOG2_E = 1.4426950408889634

U_COLS = 5632
COL_GDN_QKV, COL_GDN_Z, COL_SSD_Z, COL_SSD_XBC = 0, 1536, 2048, 3072
COL_QLAT, COL_KPE, COL_KVLAT, COL_SMALL = 4608, 4992, 5120, 5376
LANE_GDN_B, LANE_GDN_A, LANE_SSD_DT = 64, 68, 72

SEQ_BLOCK = 256
GDN_BLOCK = 256
VMEM_LIMIT_BYTES = 56 * 1024 * 1024
MOE_TM = 512
GATHER_ROWS = 512
COMBINE_ROWS = 128


def _cp(*sem):
    return pltpu.CompilerParams(dimension_semantics=sem, vmem_limit_bytes=VMEM_LIMIT_BYTES)


def _rms(x, gain):
    return x * lax.rsqrt(jnp.mean(x * x, axis=-1, keepdims=True) + EPS) * gain


def _silu(x):
    return x * jax.nn.sigmoid(x)


def _bdot(a, b):
    return jnp.dot(a.astype(BF16), b.astype(BF16), preferred_element_type=F32)


def _bdot_nt(a, b):
    return lax.dot_general(a.astype(BF16), b.astype(BF16), (((1,), (1,)), ((), ())), preferred_element_type=F32)


def _bdot_tn(a, b):
    return lax.dot_general(a.astype(BF16), b.astype(BF16), (((0,), (0,)), ((), ())), preferred_element_type=F32)


def _split3(x):
    x1 = x.astype(BF16)
    r1 = x - x1.astype(F32)
    x2 = r1.astype(BF16)
    x3 = (r1 - x2.astype(F32)).astype(BF16)
    return x1, x2, x3


def _dot_exact_lhs(m01, x):
    x1, x2, x3 = _split3(x)
    d = functools.partial(jnp.dot, preferred_element_type=F32)
    return d(m01, x1) + d(m01, x2) + d(m01, x3)


def _slab_rows(rows, n_steps):
    bf16_sublanes = 16
    return next(r for r in range(bf16_sublanes, rows + 1, bf16_sublanes) if rows % r == 0 and rows // r <= n_steps)


def _mm_kernel(*refs, norm, cast, has_res, n_side):
    it = iter(refs)
    x_ref = next(it)
    g_ref = next(it) if norm else None
    w_ref = next(it)
    res_ref = next(it) if has_res else None
    side_in = [next(it) for _ in range(n_side)]
    o_ref = next(it)
    for src_ref in side_in:
        dst_ref = next(it)
        dst_ref[...] = src_ref[...].astype(BF16)
    if cast:
        xs_ref = next(it)

        @pl.when(pl.program_id(1) == 0)
        def _():
            x = x_ref[...].astype(F32)
            if norm:
                x = _rms(x, g_ref[...])
            xs_ref[...] = x.astype(BF16)

        a = xs_ref[...]
    else:
        a = x_ref[...]
    acc = jnp.dot(a, w_ref[...], preferred_element_type=F32)
    if has_res:
        acc = acc + res_ref[...]
    o_ref[...] = acc.astype(o_ref.dtype)


def _mm(x, w, *, gain=None, res=None, out_dtype=F32, tm, tn, name, side=()):
    m, k = x.shape
    n = w.shape[1]
    tm, tn = min(tm, m), min(tn, n)
    grid = (m // tm, n // tn)
    n_steps = grid[0] * grid[1]
    norm = gain is not None
    cast = norm or x.dtype != BF16
    in_specs = [pl.BlockSpec((tm, k), lambda i, j: (i, 0))]
    args = [x]
    if norm:
        in_specs.append(pl.BlockSpec((1, k), lambda i, j: (0, 0)))
        args.append(gain.reshape(1, k).astype(F32))
    in_specs.append(pl.BlockSpec((k, tn), lambda i, j: (0, j)))
    args.append(w)
    if res is not None:
        in_specs.append(pl.BlockSpec((tm, tn), lambda i, j: (i, j)))
        args.append(res)
    out_specs = [pl.BlockSpec((tm, tn), lambda i, j: (i, j))]
    out_shape = [jax.ShapeDtypeStruct((m, n), out_dtype)]
    for arr, first_row, n_rows in side:
        slab = _slab_rows(n_rows, n_steps)
        n_slabs = n_rows // slab
        assert first_row % slab == 0
        first_slab = first_row // slab
        in_specs.append(pl.BlockSpec(
            (slab, arr.shape[1]),
            lambda i, j, n_slabs=n_slabs, first_slab=first_slab: (first_slab + jnp.minimum(i * grid[1] + j, n_slabs - 1), 0)))
        out_specs.append(pl.BlockSpec(
            (slab, arr.shape[1]), lambda i, j, n_slabs=n_slabs: (jnp.minimum(i * grid[1] + j, n_slabs - 1), 0)))
        out_shape.append(jax.ShapeDtypeStruct((n_rows, arr.shape[1]), BF16))
        args.append(arr)
    outs = pl.pallas_call(
        functools.partial(_mm_kernel, norm=norm, cast=cast, has_res=res is not None, n_side=len(side)),
        grid=grid,
        in_specs=in_specs,
        out_specs=out_specs,
        out_shape=out_shape,
        scratch_shapes=[pltpu.VMEM((tm, k), BF16)] if cast else [],
        compiler_params=_cp("arbitrary", "arbitrary"),
        name=name,
    )(*args)
    return outs if side else outs[0]


def _mm_parts_kernel(*refs, widths):
    n = len(widths)
    x_refs, w_ref, res_ref, o_ref = refs[:n], refs[n], refs[n + 1], refs[n + 2]
    acc = res_ref[...]
    off = 0
    for x_ref, width in zip(x_refs, widths):
        acc = acc + jnp.dot(x_ref[...], w_ref[off:off + width, :], preferred_element_type=F32)
        off += width
    o_ref[...] = acc


def _mm_parts(xs, w, res, *, tm, tn, name):
    m = xs[0].shape[0]
    widths = tuple(x.shape[1] for x in xs)
    k, n = w.shape
    assert sum(widths) == k
    tm, tn = min(tm, m), min(tn, n)
    in_specs = [pl.BlockSpec((tm, width), lambda i, j: (i, 0)) for width in widths]
    in_specs += [pl.BlockSpec((k, tn), lambda i, j: (0, j)), pl.BlockSpec((tm, tn), lambda i, j: (i, j))]
    return pl.pallas_call(
        functools.partial(_mm_parts_kernel, widths=widths),
        grid=(m // tm, n // tn),
        in_specs=in_specs,
        out_specs=pl.BlockSpec((tm, tn), lambda i, j: (i, j)),
        out_shape=jax.ShapeDtypeStruct((m, n), F32),
        compiler_params=_cp("parallel", "arbitrary"),
        name=name,
    )(*xs, w, res)


def _swiglu_up_kernel(x_ref, g_ref, wg_ref, wu_ref, o_ref, xs_ref):
    @pl.when(pl.program_id(1) == 0)
    def _():
        xs_ref[...] = _rms(x_ref[...], g_ref[...]).astype(BF16)

    a = xs_ref[...]
    for c in range(0, o_ref.shape[1], MXU_COLS):
        cs = slice(c, c + MXU_COLS)
        gate = jnp.dot(a, wg_ref[:, cs], preferred_element_type=F32)
        up = jnp.dot(a, wu_ref[:, cs], preferred_element_type=F32)
        o_ref[:, cs] = (_silu(gate) * up).astype(o_ref.dtype)


def _swiglu_up(x, gain, wg, wu, *, tm, tn):
    m, k = x.shape
    n = wg.shape[1]
    tm, tn = min(tm, m), min(tn, n)
    return pl.pallas_call(
        _swiglu_up_kernel,
        grid=(m // tm, n // tn),
        in_specs=[pl.BlockSpec((tm, k), lambda i, j: (i, 0)),
                  pl.BlockSpec((1, k), lambda i, j: (0, 0)),
                  pl.BlockSpec((k, tn), lambda i, j: (0, j)),
                  pl.BlockSpec((k, tn), lambda i, j: (0, j))],
        out_specs=pl.BlockSpec((tm, tn), lambda i, j: (i, j)),
        out_shape=jax.ShapeDtypeStruct((m, n), BF16),
        scratch_shapes=[pltpu.VMEM((tm, k), BF16)],
        compiler_params=_cp("parallel", "arbitrary"),
        name="swiglu_up",
    )(x, gain.reshape(1, k).astype(F32), wg, wu)


def _ple_kernel(h_ref, p_ref, wp_ref, g_ref, wg_ref, o_ref):
    h = h_ref[...]
    e = jnp.dot(p_ref[0].astype(BF16), wp_ref[...], preferred_element_type=F32)
    e = _rms(e, g_ref[...])
    gate = jax.nn.sigmoid(jnp.dot(h.astype(BF16), wg_ref[...], preferred_element_type=F32))
    o_ref[...] = h + gate * e


def _ple(h, p_all, layer, wp, gain, wg, *, tm):
    m, d = h.shape
    dp = p_all.shape[2]
    tm = min(tm, m)
    return pl.pallas_call(
        _ple_kernel,
        grid=(m // tm,),
        in_specs=[pl.BlockSpec((tm, d), lambda i: (i, 0)),
                  pl.BlockSpec((1, tm, dp), lambda i: (layer, i, 0)),
                  pl.BlockSpec((dp, d), lambda i: (0, 0)),
                  pl.BlockSpec((1, d), lambda i: (0, 0)),
                  pl.BlockSpec((d, d), lambda i: (0, 0))],
        out_specs=pl.BlockSpec((tm, d), lambda i: (i, 0)),
        out_shape=jax.ShapeDtypeStruct((m, d), F32),
        compiler_params=_cp("parallel"),
        name="ple",
    )(h, p_all, wp, gain.reshape(1, d).astype(F32), wg)


def _conv_silu_slab(x, first, tail_ref, xbuf_ref, cw, bias):
    rows = x.shape[0]

    @pl.when(first)
    def _():
        tail_ref[...] = jnp.zeros_like(tail_ref)

    xbuf_ref[0:8, :] = tail_ref[...]
    xbuf_ref[8:8 + rows, :] = x
    tail_ref[...] = x[rows - 8:rows, :]
    y = (xbuf_ref[5:5 + rows, :] * cw[0:1, :] + xbuf_ref[6:6 + rows, :] * cw[1:2, :]
         + xbuf_ref[7:7 + rows, :] * cw[2:3, :] + x * cw[3:4, :])
    if bias is not None:
        y = y + bias
    return _silu(y)


def _gdn_kernel(qkv_ref, z_ref, sm_ref, cw_ref, vec_ref, ng_ref, o_ref, state_ref, tail_ref, xbuf_ref, xc_ref):
    rows = GDN_BLOCK
    c = GDN_CHUNK
    first = pl.program_id(1) == 0

    @pl.when(first)
    def _():
        state_ref[...] = jnp.zeros_like(state_ref)

    for s in range(3):
        sl = slice(s * GDN_WIDTH, (s + 1) * GDN_WIDTH)
        xc_ref[:, sl] = _conv_silu_slab(qkv_ref[0, :, sl], first, tail_ref.at[s], xbuf_ref, cw_ref[:, sl], None)

    sm = sm_ref[0]
    beta_all = jax.nn.sigmoid(sm)
    g_all = vec_ref[0:1, :] * jax.nn.softplus(sm + vec_ref[1:2, :])

    row = lax.broadcasted_iota(I32, (rows, rows), 0)
    col = lax.broadcasted_iota(I32, (rows, rows), 1)
    same = (row // c) == (col // c)
    causal = same & (col <= row)
    strict = same & (col < row)
    tri01 = jnp.where(causal, 1.0, 0.0).astype(BF16)
    blk01 = jnp.where(same, 1.0, 0.0).astype(BF16)
    gc_all = _dot_exact_lhs(tri01, g_all)
    gl_all = _dot_exact_lhs(blk01, g_all)
    gc_all_t = gc_all.T

    heads = range(GDN_HEADS)
    eye = jnp.where(row == col, 1.0, 0.0).astype(F32)
    q_l, k_l, vb_l, kb_l, gc_l, gl_l, decay_l, p_l, x_l = [], [], [], [], [], [], [], [], []
    for h in heads:
        la = LANE_GDN_A + h
        q = xc_ref[:, h * GDN_DH:(h + 1) * GDN_DH]
        k = xc_ref[:, GDN_WIDTH + h * GDN_DH:GDN_WIDTH + (h + 1) * GDN_DH]
        v = xc_ref[:, 2 * GDN_WIDTH + h * GDN_DH:2 * GDN_WIDTH + (h + 1) * GDN_DH]
        q = q * lax.rsqrt(jnp.sum(q * q, axis=-1, keepdims=True) + EPS) * (GDN_DH ** -0.5)
        k = k * lax.rsqrt(jnp.sum(k * k, axis=-1, keepdims=True) + EPS)
        beta = beta_all[:, LANE_GDN_B + h:LANE_GDN_B + h + 1]
        gc = gc_all[:, la:la + 1]
        decay = jnp.where(causal, jnp.exp(gc - gc_all_t[la:la + 1, :]), 0.0)
        kb = k * beta
        m = jnp.where(strict, _bdot_nt(kb, k) * decay, 0.0)
        q_l.append(q), k_l.append(k), vb_l.append(v * beta), kb_l.append(kb)
        gc_l.append(gc), gl_l.append(gl_all[:, la:la + 1]), decay_l.append(decay)
        p_l.append(m), x_l.append(eye - m)

    span = 2
    while span < c:
        for h in heads:
            p_l[h] = _bdot(p_l[h], p_l[h])
        for h in heads:
            x_l[h] = x_l[h] + _bdot(x_l[h], p_l[h])
        span *= 2

    u_l, w_l, intra_l, qd_l, kd_l, ge_l = [], [], [], [], [], []
    for h in heads:
        eg = jnp.exp(gc_l[h])
        sol = _bdot(x_l[h], jnp.concatenate([vb_l[h], kb_l[h] * eg], axis=1))
        u_l.append(sol[:, :GDN_DH]), w_l.append(sol[:, GDN_DH:])
        intra_l.append((_bdot_nt(q_l[h], k_l[h]) * decay_l[h]).astype(BF16))
        qd_l.append(q_l[h] * eg)
        kd_l.append(k_l[h] * jnp.exp(gl_l[h] - gc_l[h]))
        ge_l.append(jnp.exp(gl_l[h]))

    st_l = [state_ref[h] for h in heads]
    out_l = [[] for _ in heads]
    for ci in range(rows // c):
        rs = slice(ci * c, (ci + 1) * c)
        for h in heads:
            st = st_l[h]
            v_new = u_l[h][rs] - _bdot(w_l[h][rs], st)
            pieces = []
            if ci > 0:
                pieces.append(jnp.zeros((ci * c, GDN_DH), F32))
            pieces.append(v_new)
            if (ci + 1) * c < rows:
                pieces.append(jnp.zeros((rows - (ci + 1) * c, GDN_DH), F32))
            v_pad = jnp.concatenate(pieces, axis=0) if len(pieces) > 1 else v_new
            out_l[h].append(_bdot(qd_l[h][rs], st)
                            + jnp.dot(intra_l[h][rs], v_pad.astype(BF16), preferred_element_type=F32))
            st_l[h] = st * ge_l[h][ci * c:ci * c + 1, :] + _bdot_tn(kd_l[h][rs], v_new)

    for h in heads:
        hs = slice(h * GDN_DH, (h + 1) * GDN_DH)
        state_ref[h] = st_l[h]
        o = jnp.concatenate(out_l[h], axis=0)
        o = _rms(o, ng_ref[...]) * _silu(z_ref[0, :, hs])
        o_ref[0, :, hs] = o.astype(o_ref.dtype)


def _gdn(u3, conv_w, a_log, dt_bias, norm_g):
    b, s, _ = u3.shape
    cw = conv_w.T.astype(F32)
    vec = jnp.zeros((2, LANES), F32)
    vec = vec.at[0, LANE_GDN_A:LANE_GDN_A + GDN_HEADS].set(-jnp.exp(a_log.astype(F32)))
    vec = vec.at[1, LANE_GDN_A:LANE_GDN_A + GDN_HEADS].set(dt_bias.astype(F32))
    blk = GDN_BLOCK
    return pl.pallas_call(
        _gdn_kernel,
        grid=(b, s // blk),
        in_specs=[pl.BlockSpec((1, blk, 3 * GDN_WIDTH), lambda i, j: (i, j, COL_GDN_QKV // (3 * GDN_WIDTH))),
                  pl.BlockSpec((1, blk, GDN_WIDTH), lambda i, j: (i, j, COL_GDN_Z // GDN_WIDTH)),
                  pl.BlockSpec((1, blk, LANES), lambda i, j: (i, j, COL_SMALL // LANES)),
                  pl.BlockSpec((4, 3 * GDN_WIDTH), lambda i, j: (0, 0)),
                  pl.BlockSpec((2, LANES), lambda i, j: (0, 0)),
                  pl.BlockSpec((1, GDN_DH), lambda i, j: (0, 0))],
        out_specs=pl.BlockSpec((1, blk, GDN_WIDTH), lambda i, j: (i, j, 0)),
        out_shape=jax.ShapeDtypeStruct((b, s, GDN_WIDTH), BF16),
        scratch_shapes=[pltpu.VMEM((GDN_HEADS, GDN_DH, GDN_DH), F32),
                        pltpu.VMEM((3, 8, GDN_WIDTH), F32),
                        pltpu.VMEM((blk + 8, GDN_WIDTH), F32),
                        pltpu.VMEM((blk, 3 * GDN_WIDTH), F32)],
        compiler_params=_cp("parallel", "arbitrary"),
        name="gdn",
    )(u3, u3, u3, cw, vec, norm_g.reshape(1, GDN_DH).astype(F32))


def _ssd_kernel(z_ref, xbc_ref, sm_ref, cw_ref, cb_ref, vec_ref, dsk_ref, ng_ref, o_ref,
                state_ref, tail_ref, xbuf_ref, xc_ref, y_ref):
    rows = SEQ_BLOCK
    first = pl.program_id(1) == 0
    half = SSD_P

    @pl.when(first)
    def _():
        state_ref[...] = jnp.zeros_like(state_ref)

    slab = 512
    for s in range(SSD_CONV_DIM // slab):
        sl = slice(s * slab, (s + 1) * slab)
        xc_ref[:, sl] = _conv_silu_slab(xbc_ref[0, :, sl], first, tail_ref.at[s], xbuf_ref, cw_ref[:, sl], cb_ref[:, sl])

    dt_all = jax.nn.softplus(sm_ref[0] + vec_ref[1:2, :])
    da_all = dt_all * vec_ref[0:1, :]
    row = lax.broadcasted_iota(I32, (rows, rows), 0)
    col = lax.broadcasted_iota(I32, (rows, rows), 1)
    causal = col <= row
    tri01 = jnp.where(causal, 1.0, 0.0).astype(BF16)
    acs = _dot_exact_lhs(tri01, da_all)
    acs_t = acs.T
    lane = lax.broadcasted_iota(I32, (rows, LANES), 1)
    lo = lane < half
    lane1 = lax.broadcasted_iota(I32, (1, LANES), 1)
    lo1 = lane1 < half

    d = functools.partial(jnp.dot, preferred_element_type=F32)
    pairs_per_group = SSD_HEADS // SSD_GROUPS // 2
    n_pairs = SSD_HEADS // 2
    cb_l, bmt_l, cmb_l = [], [], []
    for g in range(SSD_GROUPS):
        bm = xc_ref[:, SSD_WIDTH + g * SSD_N:SSD_WIDTH + (g + 1) * SSD_N]
        cm = xc_ref[:, SSD_WIDTH + (SSD_GROUPS + g) * SSD_N:SSD_WIDTH + (SSD_GROUPS + g + 1) * SSD_N]
        cb_l.append(_bdot_nt(cm, bm))
        bmt_l.append(bm.T)
        cmb_l.append(cm.astype(BF16))

    att_l, dec_l, xdt_l, x_l, expa_l, ge_l = [], [], [], [], [], []
    for p in range(n_pairs):
        g = p // pairs_per_group
        la = LANE_SSD_DT + 2 * p
        lb = la + 1
        col_a, col_b = acs[:, la:la + 1], acs[:, lb:lb + 1]
        row_a, row_b = acs_t[la:la + 1, :], acs_t[lb:lb + 1, :]
        last_a, last_b = row_a[:, rows - 1:rows], row_b[:, rows - 1:rows]
        att_l.append(((cb_l[g] * jnp.where(causal, jnp.exp(col_a - row_a), 0.0)).astype(BF16),
                      (cb_l[g] * jnp.where(causal, jnp.exp(col_b - row_b), 0.0)).astype(BF16)))
        dec_l.append(((bmt_l[g] * jnp.exp(last_a - row_a)).astype(BF16),
                      (bmt_l[g] * jnp.exp(last_b - row_b)).astype(BF16)))
        x_pair = xc_ref[:, p * LANES:(p + 1) * LANES]
        xdt = x_pair * jnp.where(lo, dt_all[:, la:la + 1], dt_all[:, lb:lb + 1])
        xdt_l.append((jnp.where(lo, xdt, 0.0).astype(BF16), jnp.where(lo, 0.0, xdt).astype(BF16)))
        x_l.append(x_pair)
        expa_l.append(jnp.where(lo, jnp.exp(col_a), jnp.exp(col_b)))
        ge_l.append(jnp.where(lo1, jnp.exp(last_a), jnp.exp(last_b)))

    for p in range(n_pairs):
        g = p // pairs_per_group
        ps = slice(p * LANES, (p + 1) * LANES)
        xdt_a, xdt_b = xdt_l[p]
        y_diag = d(att_l[p][0], xdt_a) + d(att_l[p][1], xdt_b)
        st_new = d(dec_l[p][0], xdt_a) + d(dec_l[p][1], xdt_b)
        prev = state_ref[p]
        y_off = d(cmb_l[g], prev.astype(BF16)) * expa_l[p]
        state_ref[p] = prev * ge_l[p] + st_new
        y_ref[:, ps] = y_diag + y_off + x_l[p] * dsk_ref[:, ps]

    gw = SSD_WIDTH // SSD_GROUPS
    for g in range(SSD_GROUPS):
        gs = slice(g * gw, (g + 1) * gw)
        y = y_ref[:, gs] * _silu(z_ref[0, :, gs])
        o_ref[0, :, gs] = _rms(y, ng_ref[:, gs]).astype(o_ref.dtype)


def _ssd(u3, conv_w, conv_b, a_log, dt_bias, d_skip, norm_g):
    b, s, _ = u3.shape
    cw = conv_w.T.astype(F32)
    vec = jnp.zeros((2, LANES), F32)
    vec = vec.at[0, LANE_SSD_DT:LANE_SSD_DT + SSD_HEADS].set(-jnp.exp(a_log.astype(F32)))
    vec = vec.at[1, LANE_SSD_DT:LANE_SSD_DT + SSD_HEADS].set(dt_bias.astype(F32))
    dsk = jnp.repeat(d_skip.astype(F32), SSD_P).reshape(1, SSD_WIDTH)
    blk = SEQ_BLOCK
    return pl.pallas_call(
        _ssd_kernel,
        grid=(b, s // blk),
        in_specs=[pl.BlockSpec((1, blk, SSD_WIDTH), lambda i, j: (i, j, COL_SSD_Z // SSD_WIDTH)),
                  pl.BlockSpec((1, blk, SSD_CONV_DIM), lambda i, j: (i, j, COL_SSD_XBC // SSD_CONV_DIM)),
                  pl.BlockSpec((1, blk, LANES), lambda i, j: (i, j, COL_SMALL // LANES)),
                  pl.BlockSpec((4, SSD_CONV_DIM), lambda i, j: (0, 0)),
                  pl.BlockSpec((1, SSD_CONV_DIM), lambda i, j: (0, 0)),
                  pl.BlockSpec((2, LANES), lambda i, j: (0, 0)),
                  pl.BlockSpec((1, SSD_WIDTH), lambda i, j: (0, 0)),
                  pl.BlockSpec((1, SSD_WIDTH), lambda i, j: (0, 0))],
        out_specs=pl.BlockSpec((1, blk, SSD_WIDTH), lambda i, j: (i, j, 0)),
        out_shape=jax.ShapeDtypeStruct((b, s, SSD_WIDTH), BF16),
        scratch_shapes=[pltpu.VMEM((SSD_HEADS // 2, SSD_N, 2 * SSD_P), F32),
                        pltpu.VMEM((SSD_CONV_DIM // 512, 8, 512), F32),
                        pltpu.VMEM((blk + 8, 512), F32),
                        pltpu.VMEM((blk, SSD_CONV_DIM), F32),
                        pltpu.VMEM((blk, SSD_WIDTH), F32)],
        compiler_params=_cp("parallel", "arbitrary"),
        name="ssd",
    )(u3, u3, u3, cw, conv_b.reshape(1, SSD_CONV_DIM).astype(F32), vec, dsk,
      norm_g.reshape(1, SSD_WIDTH).astype(F32))


def _mla_pre_kernel(ql_ref, kpe_ref, kvl_ref, pos_ref, qag_ref, kvag_ref, wq_ref, wkv_ref, vec_ref,
                    q_ref, k_ref, v_ref):
    qn = _rms(ql_ref[0], qag_ref[...])
    qall = jnp.dot(qn.astype(BF16), wq_ref[...], preferred_element_type=F32)
    kvn = _rms(kvl_ref[0], kvag_ref[...])
    kv = jnp.dot(kvn.astype(BF16), wkv_ref[...], preferred_element_type=F32)
    kpe = kpe_ref[0]
    ang = pos_ref[0] * vec_ref[6:7, :]
    cos = jnp.cos(ang)
    sin = jnp.sin(ang) * vec_ref[7:8, :]
    gqn, gqa, gqb = vec_ref[0:1, :], vec_ref[1:2, :], vec_ref[2:3, :]
    gkn, gka, gkb = vec_ref[3:4, :], vec_ref[4:5, :], vec_ref[5:6, :]
    scale = MLA_QK ** -0.5 * LOG2_E
    k_rot = kpe * (cos * gka) + pltpu.roll(kpe, 64, 1) * (sin * gkb)
    k_pe_ss = 0.5 * jnp.sum(kpe * kpe, axis=-1, keepdims=True)
    nh = MLA_HEADS
    for h in range(nh):
        q_nope = qall[:, h * LANES:(h + 1) * LANES]
        q_pe = qall[:, (nh + h) * LANES:(nh + h + 1) * LANES]
        ss = jnp.sum(q_nope * q_nope, axis=-1, keepdims=True) + 0.5 * jnp.sum(q_pe * q_pe, axis=-1, keepdims=True)
        rstd = lax.rsqrt(ss * (1.0 / MLA_QK) + EPS)
        q_rot = q_pe * (cos * gqa) + pltpu.roll(q_pe, 64, 1) * (sin * gqb)
        q_ref[0, h, :, 0:LANES] = (q_nope * gqn * rstd * scale).astype(q_ref.dtype)
        q_ref[0, h, :, LANES:2 * LANES] = (q_rot * rstd * (0.5 * scale)).astype(q_ref.dtype)
        k_nope = kv[:, 2 * h * LANES:(2 * h + 1) * LANES]
        ssk = jnp.sum(k_nope * k_nope, axis=-1, keepdims=True) + k_pe_ss
        rstdk = lax.rsqrt(ssk * (1.0 / MLA_QK) + EPS)
        k_ref[0, h, :, 0:LANES] = (k_nope * gkn * rstdk).astype(k_ref.dtype)
        k_ref[0, h, :, LANES:2 * LANES] = (k_rot * rstdk).astype(k_ref.dtype)
        v_ref[0, h, :, 0:LANES] = kv[:, (2 * h + 1) * LANES:(2 * h + 2) * LANES].astype(v_ref.dtype)
        v_ref[0, h, :, LANES:2 * LANES] = jnp.ones((kv.shape[0], LANES), v_ref.dtype)


def _rope_pair_gains(g):
    g1, g2 = g[MLA_NOPE:MLA_NOPE + 32], g[MLA_NOPE + 32:MLA_NOPE + 64]
    return jnp.concatenate([g1, g2, g2, g1]), jnp.concatenate([g2, g1, g1, g2])


def _mla_pre(u3, pos_f, q_a_g, w_q_b, kv_a_g, w_kv_b, q_norm_g, k_norm_g, *, ts):
    b, s, _ = u3.shape
    nh = MLA_HEADS
    wq = w_q_b.reshape(MLA_Q_RANK, nh, MLA_QK)
    x1, x2 = wq[:, :, MLA_NOPE:MLA_NOPE + 32], wq[:, :, MLA_NOPE + 32:]
    wq_all = jnp.concatenate([wq[:, :, :MLA_NOPE].reshape(MLA_Q_RANK, nh * MLA_NOPE),
                              jnp.concatenate([x1, x2, x2, x1], axis=-1).reshape(MLA_Q_RANK, nh * LANES)],
                             axis=1).astype(BF16)
    half = MLA_ROPE // 2
    inv_freq = 1.0 / (ROPE_THETA ** (jnp.arange(half, dtype=F32) / half))
    gqa, gqb = _rope_pair_gains(q_norm_g.astype(F32))
    gka, gkb = _rope_pair_gains(k_norm_g.astype(F32))
    ones = jnp.ones((half,), F32)
    vec = jnp.stack([q_norm_g[:MLA_NOPE].astype(F32), gqa, gqb, k_norm_g[:MLA_NOPE].astype(F32), gka, gkb,
                     jnp.tile(inv_freq, 4), jnp.concatenate([-ones, ones, ones, -ones])])
    ts = min(ts, s)
    qk_shape = jax.ShapeDtypeStruct((b, nh, s, 2 * LANES), BF16)
    return pl.pallas_call(
        _mla_pre_kernel,
        grid=(b, s // ts),
        in_specs=[pl.BlockSpec((1, ts, MLA_Q_RANK), lambda i, j: (i, j, COL_QLAT // MLA_Q_RANK)),
                  pl.BlockSpec((1, ts, LANES), lambda i, j: (i, j, COL_KPE // LANES)),
                  pl.BlockSpec((1, ts, MLA_KV_RANK), lambda i, j: (i, j, COL_KVLAT // MLA_KV_RANK)),
                  pl.BlockSpec((1, ts, 1), lambda i, j: (i, j, 0)),
                  pl.BlockSpec((1, MLA_Q_RANK), lambda i, j: (0, 0)),
                  pl.BlockSpec((1, MLA_KV_RANK), lambda i, j: (0, 0)),
                  pl.BlockSpec((MLA_Q_RANK, 2 * nh * LANES), lambda i, j: (0, 0)),
                  pl.BlockSpec((MLA_KV_RANK, 2 * nh * LANES), lambda i, j: (0, 0)),
                  pl.BlockSpec((8, LANES), lambda i, j: (0, 0))],
        out_specs=[pl.BlockSpec((1, nh, ts, 2 * LANES), lambda i, j: (i, 0, j, 0)),
                   pl.BlockSpec((1, nh, ts, 2 * LANES), lambda i, j: (i, 0, j, 0)),
                   pl.BlockSpec((1, nh, ts, 2 * LANES), lambda i, j: (i, 0, j, 0))],
        out_shape=[qk_shape, qk_shape, qk_shape],
        compiler_params=_cp("parallel", "parallel"),
        name="mla_pre",
    )(u3, u3, u3, pos_f, q_a_g.reshape(1, -1).astype(F32), kv_a_g.reshape(1, -1).astype(F32),
      wq_all, w_kv_b.astype(BF16), vec)


def _flash_kernel(qi_ref, kj_ref, q_ref, k_ref, v_ref, o_ref, m_ref, acc_ref, *, t):
    qi = qi_ref[pl.program_id(1)]
    kj = kj_ref[pl.program_id(1)]
    nh = q_ref.shape[1]

    @pl.when(kj == 0)
    def _():
        m_ref[...] = jnp.full_like(m_ref, -jnp.inf)
        acc_ref[...] = jnp.zeros_like(acc_ref)

    def step(diagonal):
        s_l = [lax.dot_general(q_ref[0, h], k_ref[0, h], (((1,), (1,)), ((), ())), preferred_element_type=F32)
               for h in range(nh)]
        if diagonal:
            keep = lax.broadcasted_iota(I32, (t, t), 0) >= lax.broadcasted_iota(I32, (t, t), 1)
            s_l = [jnp.where(keep, s, -jnp.inf) for s in s_l]
        p_l, alpha_l = [], []
        for h in range(nh):
            m_old = m_ref[h]
            m_new = jnp.maximum(m_old, jnp.max(s_l[h], axis=-1, keepdims=True))
            p_l.append(jnp.exp2(s_l[h] - jnp.tile(m_new, (1, t // LANES))).astype(BF16))
            alpha_l.append(jnp.exp2(m_old - m_new))
            m_ref[h] = m_new
        for h in range(nh):
            pv = jnp.dot(p_l[h], v_ref[0, h], preferred_element_type=F32)
            acc_ref[h] = jnp.tile(alpha_l[h], (1, 2)) * acc_ref[h] + pv

    @pl.when(kj < qi)
    def _():
        step(False)

    @pl.when(kj == qi)
    def _():
        step(True)
        for h in range(nh):
            acc = acc_ref[h]
            o_ref[0, :, h * LANES:(h + 1) * LANES] = (acc[:, :LANES] / acc[:, LANES:]).astype(o_ref.dtype)


def _flash(q, k, v, *, t):
    b, nh, s, dq = q.shape
    t = min(t, s)
    n = s // t
    pairs = [(qi, kj) for qi in range(n) for kj in range(qi + 1)]
    qi_tab = jnp.asarray([pr[0] for pr in pairs], I32)
    kj_tab = jnp.asarray([pr[1] for pr in pairs], I32)
    kv_spec = pl.BlockSpec((1, nh, t, dq), lambda i, pr, qt, kt: (i, 0, kt[pr], 0))
    grid_spec = pltpu.PrefetchScalarGridSpec(
        num_scalar_prefetch=2,
        grid=(b, len(pairs)),
        in_specs=[pl.BlockSpec((1, nh, t, dq), lambda i, pr, qt, kt: (i, 0, qt[pr], 0)), kv_spec, kv_spec],
        out_specs=pl.BlockSpec((1, t, nh * LANES), lambda i, pr, qt, kt: (i, qt[pr], 0)),
        scratch_shapes=[pltpu.VMEM((nh, t, LANES), F32), pltpu.VMEM((nh, t, 2 * LANES), F32)],
    )
    return pl.pallas_call(
        functools.partial(_flash_kernel, t=t),
        grid_spec=grid_spec,
        out_shape=jax.ShapeDtypeStruct((b, s, nh * LANES), BF16),
        compiler_params=_cp("parallel", "arbitrary"),
        name="mla_flash",
    )(qi_tab, kj_tab, q, k, v)


def _router_kernel(h_ref, g_ref, wr_ref, idx_ref, wt_ref):
    xn = _rms(h_ref[...], g_ref[...])
    w = wr_ref[...]
    xh, wh = xn.astype(BF16), w.astype(BF16)
    xl, wl = (xn - xh.astype(F32)).astype(BF16), (w - wh.astype(F32)).astype(BF16)
    d = functools.partial(jnp.dot, preferred_element_type=F32)
    logits = d(xh, wh) + d(xh, wl) + d(xl, wh)
    lane = lax.broadcasted_iota(I32, logits.shape, 1)
    logits = jnp.where(lane < N_EXPERTS, logits, -jnp.inf)
    m1 = jnp.max(logits, axis=-1, keepdims=True)
    i1 = jnp.min(jnp.where(logits == m1, lane, LANES), axis=-1, keepdims=True)
    rest = jnp.where(lane == i1, -jnp.inf, logits)
    m2 = jnp.max(rest, axis=-1, keepdims=True)
    i2 = jnp.min(jnp.where(rest == m2, lane, LANES), axis=-1, keepdims=True)
    e2 = jnp.exp(m2 - m1)
    w1 = 1.0 / (1.0 + e2)
    w2 = e2 / (1.0 + e2)
    idx_ref[...] = jnp.where(lane == 0, i1, jnp.where(lane == 1, i2, 0))
    wt_ref[...] = jnp.where(lane == 0, w1, jnp.where(lane == 1, w2, 0.0))


def _router(h, gain, w_router, *, tm):
    m, d = h.shape
    tm = min(tm, m)
    wr = jnp.zeros((d, LANES), F32).at[:, :N_EXPERTS].set(w_router.astype(F32))
    return pl.pallas_call(
        _router_kernel,
        grid=(m // tm,),
        in_specs=[pl.BlockSpec((tm, d), lambda i: (i, 0)),
                  pl.BlockSpec((1, d), lambda i: (0, 0)),
                  pl.BlockSpec((d, LANES), lambda i: (0, 0))],
        out_specs=[pl.BlockSpec((tm, LANES), lambda i: (i, 0)),
                   pl.BlockSpec((tm, LANES), lambda i: (i, 0))],
        out_shape=[jax.ShapeDtypeStruct((m, LANES), I32),
                   jax.ShapeDtypeStruct((m, LANES), F32)],
        compiler_params=_cp("parallel"),
        name="moe_router",
    )(h, gain.reshape(1, d).astype(F32), wr)


def _row_copy(src_ref, t, dst_ref, r, sem):
    return pltpu.make_async_copy(src_ref.at[pl.ds(t, 1), :], dst_ref.at[pl.ds(r, 1), :], sem)


def _gather_kernel(idx_ref, src_ref, g_ref, o_ref, buf_ref, sem):
    rows = buf_ref.shape[1]
    i = pl.program_id(0)
    n = pl.num_programs(0)

    def start_step(step, slot):
        def issue(r2, c):
            for k in range(2):
                r = 2 * r2 + k
                _row_copy(src_ref, idx_ref[step * rows + r], buf_ref.at[slot], r, sem.at[slot]).start(priority=k)
            return c
        lax.fori_loop(0, rows // 2, issue, 0, unroll=4)

    @pl.when(i == 0)
    def _():
        start_step(0, 0)

    @pl.when(i + 1 < n)
    def _():
        start_step(i + 1, (i + 1) % 2)

    slot = i % 2

    def wait(r, c):
        _row_copy(src_ref, 0, buf_ref.at[slot], r, sem.at[slot]).wait()
        return c

    lax.fori_loop(0, rows, wait, 0, unroll=8)
    o_ref[...] = _rms(buf_ref[slot], g_ref[...]).astype(o_ref.dtype)


def _gather_norm_rows(src_tok, h, gain):
    p = src_tok.shape[0]
    d = h.shape[1]
    rows = GATHER_ROWS
    grid_spec = pltpu.PrefetchScalarGridSpec(
        num_scalar_prefetch=1,
        grid=(p // rows,),
        in_specs=[pl.BlockSpec(memory_space=pl.ANY), pl.BlockSpec((1, d), lambda i, idx: (0, 0))],
        out_specs=pl.BlockSpec((rows, d), lambda i, idx: (i, 0)),
        scratch_shapes=[pltpu.VMEM((2, rows, d), F32), pltpu.SemaphoreType.DMA((2,))],
    )
    return pl.pallas_call(
        _gather_kernel,
        grid_spec=grid_spec,
        out_shape=jax.ShapeDtypeStruct((p, d), BF16),
        compiler_params=_cp("arbitrary"),
        name="moe_gather",
    )(src_tok, h, gain.reshape(1, d).astype(F32))


def _gup_kernel(te_ref, nu_ref, x_ref, wg_ref, wu_ref, wd_ref, o_ref, wdb_ref, wgb_ref, wub_ref):
    i = pl.program_id(1)
    used = i < nu_ref[0]

    wdb_ref[...] = wd_ref[...].astype(BF16)

    @pl.when((i == 0) | (te_ref[i] != te_ref[jnp.maximum(i - 1, 0)]))
    def _():
        wgb_ref[...] = wg_ref[0].astype(BF16)
        wub_ref[...] = wu_ref[0].astype(BF16)

    @pl.when(used)
    def _():
        x = x_ref[...]
        for c in range(0, o_ref.shape[1], MXU_COLS):
            cs = slice(c, c + MXU_COLS)
            gate = jnp.dot(x, wgb_ref[:, cs], preferred_element_type=F32)
            up = jnp.dot(x, wub_ref[:, cs], preferred_element_type=F32)
            o_ref[:, cs] = (_silu(gate) * up).astype(o_ref.dtype)

    @pl.when(jnp.logical_not(used))
    def _():
        o_ref[...] = jnp.zeros_like(o_ref)


def _grouped_up(tile_expert, n_used, xs, wg, wu, wd, *, tm, tf):
    p, d = xs.shape
    f = wg.shape[2]
    n_i = p // tm
    n_steps = (f // tf) * n_i
    wd2 = wd.reshape(-1, wd.shape[-1])
    slab = _slab_rows(wd2.shape[0], n_steps)
    n_slabs = wd2.shape[0] // slab

    def slab_map(j, i, te, nu):
        return (jnp.minimum(j * n_i + i, n_slabs - 1), 0)

    grid_spec = pltpu.PrefetchScalarGridSpec(
        num_scalar_prefetch=2,
        grid=(f // tf, n_i),
        in_specs=[pl.BlockSpec((tm, d), lambda j, i, te, nu: (jnp.minimum(i, nu[0] - 1), 0)),
                  pl.BlockSpec((1, d, tf), lambda j, i, te, nu: (te[i], 0, j)),
                  pl.BlockSpec((1, d, tf), lambda j, i, te, nu: (te[i], 0, j)),
                  pl.BlockSpec((slab, wd2.shape[1]), slab_map)],
        out_specs=[pl.BlockSpec((tm, tf), lambda j, i, te, nu: (i, j)),
                   pl.BlockSpec((slab, wd2.shape[1]), slab_map)],
        scratch_shapes=[pltpu.VMEM((d, tf), BF16), pltpu.VMEM((d, tf), BF16)],
    )
    hff, wdb = pl.pallas_call(
        _gup_kernel,
        grid_spec=grid_spec,
        out_shape=[jax.ShapeDtypeStruct((p, f), BF16), jax.ShapeDtypeStruct(wd2.shape, BF16)],
        compiler_params=_cp("arbitrary", "arbitrary"),
        name="moe_up",
    )(tile_expert, n_used, xs, wg, wu, wd2)
    return hff, wdb.reshape(wd.shape)


def _gdown_kernel(te_ref, nu_ref, x_ref, wd_ref, o_ref):
    used = pl.program_id(1) < nu_ref[0]

    @pl.when(used)
    def _():
        o_ref[...] = jnp.dot(x_ref[...], wd_ref[0], preferred_element_type=F32)

    @pl.when(jnp.logical_not(used))
    def _():
        o_ref[...] = jnp.zeros_like(o_ref)


def _grouped_down(tile_expert, n_used, hff, wd, *, tm, tn):
    p, f = hff.shape
    d = wd.shape[2]
    grid_spec = pltpu.PrefetchScalarGridSpec(
        num_scalar_prefetch=2,
        grid=(d // tn, p // tm),
        in_specs=[pl.BlockSpec((tm, f), lambda j, i, te, nu: (jnp.minimum(i, nu[0] - 1), 0)),
                  pl.BlockSpec((1, f, tn), lambda j, i, te, nu: (te[i], 0, j))],
        out_specs=pl.BlockSpec((tm, tn), lambda j, i, te, nu: (i, j)),
    )
    return pl.pallas_call(
        _gdown_kernel,
        grid_spec=grid_spec,
        out_shape=jax.ShapeDtypeStruct((p, d), F32),
        compiler_params=_cp("arbitrary", "arbitrary"),
        name="moe_down",
    )(tile_expert, n_used, hff, wd)


def _combine_kernel(pos_ref, h_ref, wt_ref, y_ref, o_ref, buf_ref, sem):
    rows = h_ref.shape[0]
    i = pl.program_id(0)
    n = pl.num_programs(0)

    def start_step(step, slot):
        def issue(r, c):
            base = 2 * (step * rows + r)
            _row_copy(y_ref, pos_ref[base], buf_ref.at[slot, 0], r, sem.at[slot]).start(priority=0)
            _row_copy(y_ref, pos_ref[base + 1], buf_ref.at[slot, 1], r, sem.at[slot]).start(priority=1)
            return c
        lax.fori_loop(0, rows, issue, 0, unroll=4)

    @pl.when(i == 0)
    def _():
        start_step(0, 0)

    @pl.when(i + 1 < n)
    def _():
        start_step(i + 1, (i + 1) % 2)

    slot = i % 2

    def wait(r, c):
        _row_copy(y_ref, 0, buf_ref.at[slot, 0], r, sem.at[slot]).wait()
        _row_copy(y_ref, 0, buf_ref.at[slot, 1], r, sem.at[slot]).wait()
        return c

    lax.fori_loop(0, rows, wait, 0, unroll=4)
    wt = wt_ref[...]
    o_ref[...] = h_ref[...] + wt[:, 0:1] * buf_ref[slot, 0] + wt[:, 1:2] * buf_ref[slot, 1]


def _combine(h, y, pos, wts):
    m, d = h.shape
    rows = min(COMBINE_ROWS, m)
    grid_spec = pltpu.PrefetchScalarGridSpec(
        num_scalar_prefetch=1,
        grid=(m // rows,),
        in_specs=[pl.BlockSpec((rows, d), lambda i, pos: (i, 0)),
                  pl.BlockSpec((rows, LANES), lambda i, pos: (i, 0)),
                  pl.BlockSpec(memory_space=pl.ANY)],
        out_specs=pl.BlockSpec((rows, d), lambda i, pos: (i, 0)),
        scratch_shapes=[pltpu.VMEM((2, 2, rows, d), F32), pltpu.SemaphoreType.DMA((2,))],
    )
    return pl.pallas_call(
        _combine_kernel,
        grid_spec=grid_spec,
        out_shape=jax.ShapeDtypeStruct((m, d), F32),
        compiler_params=_cp("arbitrary"),
        name="moe_combine",
    )(pos, h, wts, y)


def _moe(h, gain, w_router, wg, wu, wd):
    t, d = h.shape
    tm = MOE_TM
    n2 = 2 * t
    idx, wts = _router(h, gain, w_router, tm=512)
    flat_e = idx[:, :2].reshape(-1)
    onehot = (flat_e[:, None] == jnp.arange(N_EXPERTS, dtype=I32)[None, :]).astype(I32)
    csum = jnp.cumsum(onehot, axis=0)
    rank = jnp.sum((csum - onehot) * onehot, axis=1)
    counts = csum[-1]
    padded = ((counts + tm - 1) // tm) * tm
    ends = jnp.cumsum(padded)
    starts = ends - padded
    pos = (starts[flat_e] + rank).astype(I32)
    p_rows = n2 + N_EXPERTS * tm
    n_tiles = p_rows // tm
    n_used = (ends[-1] // tm).astype(I32).reshape(1)
    tile_start = jnp.arange(n_tiles, dtype=I32) * tm
    tile_expert = jnp.sum((tile_start[:, None] >= ends[None, :]).astype(I32), axis=1)
    last_expert = jnp.sum((ends[-1] - 1 >= ends).astype(I32))
    tile_expert = jnp.minimum(tile_expert, last_expert).astype(I32)
    order = jnp.sort(flat_e * n2 + jnp.arange(n2, dtype=I32)) % n2
    row = jnp.arange(p_rows, dtype=I32)
    row_e = jnp.repeat(tile_expert, tm)
    local = row - starts[row_e]
    first = (jnp.cumsum(counts) - counts)[row_e]
    src_tok = jnp.where(local < counts[row_e], order[jnp.clip(first + local, 0, n2 - 1)] // 2, 0).astype(I32)

    xs = _gather_norm_rows(src_tok, h, gain)
    hff, wd_bf16 = _grouped_up(tile_expert, n_used, xs, wg, wu, wd, tm=tm, tf=1024)
    y = _grouped_down(tile_expert, n_used, hff, wd_bf16, tm=tm, tn=1024)
    return _combine(h, y, pos, wts)


def _w_in_moves():
    gdn0, ssd0 = 0, 4 * GDN_WIDTH + 2 * GDN_HEADS
    mla0 = ssd0 + SSD_WIDTH + SSD_CONV_DIM + SSD_HEADS
    kpe0 = mla0 + MLA_Q_RANK + MLA_KV_RANK
    gb0 = gdn0 + 4 * GDN_WIDTH
    dt0 = ssd0 + SSD_WIDTH + SSD_CONV_DIM
    half = MLA_ROPE // 2
    moves = [(COL_GDN_QKV, gdn0, 4 * GDN_WIDTH),
             (COL_SSD_Z, ssd0, SSD_WIDTH + SSD_CONV_DIM),
             (COL_QLAT, mla0, MLA_Q_RANK),
             (COL_KPE, kpe0, MLA_ROPE), (COL_KPE + MLA_ROPE, kpe0 + half, half),
             (COL_KPE + MLA_ROPE + half, kpe0, half),
             (COL_KVLAT, mla0 + MLA_Q_RANK, MLA_KV_RANK),
             (COL_SMALL + LANE_GDN_B, gb0, 2 * GDN_HEADS), (COL_SMALL + LANE_SSD_DT, dt0, SSD_HEADS)]
    small_end = LANE_SSD_DT + SSD_HEADS
    zeros = [(COL_SMALL, LANE_GDN_B), (COL_SMALL + small_end, LANES - small_end), (COL_SMALL + LANES, LANES)]
    assert LANE_GDN_A == LANE_GDN_B + GDN_HEADS and LANE_SSD_DT == LANE_GDN_A + GDN_HEADS
    assert sum(m[2] for m in moves) + sum(z[1] for z in zeros) == U_COLS
    return moves, zeros


def _w_in_kernel(w_ref, o_ref):
    moves, zeros = _w_in_moves()
    for dst, src, width in moves:
        o_ref[:, dst:dst + width] = w_ref[0, :, src:src + width].astype(o_ref.dtype)
    for dst, width in zeros:
        o_ref[:, dst:dst + width] = jnp.zeros((o_ref.shape[0], width), o_ref.dtype)


def _rearranged_w_in(w_in_all, layer):
    _, d, n = w_in_all.shape
    rows = 256
    return pl.pallas_call(
        _w_in_kernel,
        grid=(d // rows,),
        in_specs=[pl.BlockSpec((1, rows, n), lambda i: (layer, i, 0))],
        out_specs=pl.BlockSpec((rows, U_COLS), lambda i: (i, 0)),
        out_shape=jax.ShapeDtypeStruct((d, U_COLS), BF16),
        compiler_params=_cp("parallel"),
        name="w_in_prep",
    )(w_in_all)


def kernel(x, p, positions, norm_mix_g, w_in, w_out, gdn_conv_w, gdn_a_log, gdn_dt_bias, gdn_norm_g, ssd_conv_w, ssd_conv_b, ssd_a_log, ssd_dt_bias, ssd_d, ssd_norm_g, mla_q_a_g, mla_w_q_b, mla_kv_a_g, mla_w_kv_b, mla_q_norm_g, mla_k_norm_g, norm_ffn_g, ffn_w_gate, ffn_w_up, ffn_w_down, router_w, moe_w_gate, moe_w_up, moe_w_down, ple_w_proj, ple_w_gate, ple_norm_g):
    b, s, d = x.shape
    t = b * s
    depth = w_in.shape[0]
    h = x.reshape(t, d).astype(F32)
    pos_f = positions.astype(F32).reshape(b, s, 1)
    p_all = p.reshape(depth, t, p.shape[-1])
    d_ff = ffn_w_gate.shape[-1]
    w_out2, ple_gate2 = w_out.reshape(depth * d, d), ple_w_gate.reshape(depth * d, d)
    for i in range(depth):
        side = [(w_out2, i * d, d), (ple_gate2, i * d, d)]
        if i % 2 == 0:
            side += [(ffn_w_gate.reshape(-1, d_ff), (i // 2) * d, d), (ffn_w_up.reshape(-1, d_ff), (i // 2) * d, d),
                     (ffn_w_down.reshape(-1, d), (i // 2) * d_ff, d_ff)]
        u, w_out_b, ple_gate_b, *ffn_b = _mm(h, _rearranged_w_in(w_in, i), gain=norm_mix_g[i], tm=1024, tn=1408,
                                             name="in_proj", side=side)
        u3 = u.reshape(b, s, U_COLS)
        o_gdn = _gdn(u3, gdn_conv_w[i], gdn_a_log[i], gdn_dt_bias[i], gdn_norm_g[i])
        o_ssd = _ssd(u3, ssd_conv_w[i], ssd_conv_b[i], ssd_a_log[i], ssd_dt_bias[i], ssd_d[i], ssd_norm_g[i])
        q, k, v = _mla_pre(u3, pos_f, mla_q_a_g[i], mla_w_q_b[i], mla_kv_a_g[i], mla_w_kv_b[i],
                           mla_q_norm_g[i], mla_k_norm_g[i], ts=512)
        o_mla = _flash(q, k, v, t=512)
        mix_parts = [o.reshape(t, o.shape[-1]) for o in (o_gdn, o_ssd, o_mla)]
        h = _mm_parts(mix_parts, w_out_b, h, tm=1024, tn=1024, name="out_proj")
        j = i // 2
        if i % 2 == 0:
            ff = _swiglu_up(h, norm_ffn_g[i], ffn_b[0], ffn_b[1], tm=1024, tn=512)
            h = _mm(ff, ffn_b[2], res=h, tm=1024, tn=512, name="ffn_down")
        else:
            h = _moe(h, norm_ffn_g[i], router_w[j], moe_w_gate[j], moe_w_up[j], moe_w_down[j])
        h = _ple(h, p_all, i, ple_w_proj[i].astype(BF16), ple_norm_g[i], ple_gate_b, tm=512)
    return h.reshape(b, s, d).astype(x.dtype)
```

```python
import functools

import jax
import jax.numpy as jnp
from jax import lax
from jax.experimental import pallas as pl
from jax.experimental.pallas import tpu as pltpu

F32 = jnp.float32
BF16 = jnp.bfloat16
U32 = jnp.uint32
I32 = jnp.int32
EPS = 1e-6

D_MODEL = 2048
GDN_HEADS, GDN_DH, GDN_CHUNK = 4, 128, 64
GDN_WIDTH = GDN_HEADS * GDN_DH
SSD_HEADS, SSD_P, SSD_GROUPS, SSD_N, SSD_CHUNK = 16, 64, 2, 128, 256
SSD_WIDTH = SSD_HEADS * SSD_P
SSD_CONV_DIM = SSD_WIDTH + 2 * SSD_GROUPS * SSD_N
MLA_HEADS, MLA_Q_RANK, MLA_KV_RANK = 4, 384, 256
MLA_NOPE, MLA_ROPE, MLA_V = 128, 64, 128
MLA_QK = MLA_NOPE + MLA_ROPE
ROPE_THETA = 10000.0
N_EXPERTS = 8
LANES = 128
MXU_COLS = 256
LOG2_E = 1.4426950408889634

U_COLS = 5632
COL_GDN_QKV, COL_GDN_Z, COL_SSD_Z, COL_SSD_XBC = 0, 1536, 2048, 3072
COL_QLAT, COL_KPE, COL_KVLAT, COL_SMALL = 4608, 4992, 5120, 5376
LANE_GDN_B, LANE_GDN_A, LANE_SSD_DT = 64, 68, 72

SEQ_BLOCK = 256
GDN_BLOCK = 256
VMEM_LIMIT_BYTES = 56 * 1024 * 1024
MOE_TM = 512
COMBINE_ROWS = 128


def _cp(*sem):
    return pltpu.CompilerParams(dimension_semantics=sem, vmem_limit_bytes=VMEM_LIMIT_BYTES)


def _rms(x, gain):
    return x * lax.rsqrt(jnp.mean(x * x, axis=-1, keepdims=True) + EPS) * gain


def _silu(x):
    return x * jax.nn.sigmoid(x)


def _bdot(a, b):
    return jnp.dot(a.astype(BF16), b.astype(BF16), preferred_element_type=F32)


def _bdot_nt(a, b):
    return lax.dot_general(a.astype(BF16), b.astype(BF16), (((1,), (1,)), ((), ())), preferred_element_type=F32)


def _bdot_tn(a, b):
    return lax.dot_general(a.astype(BF16), b.astype(BF16), (((0,), (0,)), ((), ())), preferred_element_type=F32)


def _split3(x):
    x1 = x.astype(BF16)
    r1 = x - x1.astype(F32)
    x2 = r1.astype(BF16)
    x3 = (r1 - x2.astype(F32)).astype(BF16)
    return x1, x2, x3


def _dot_exact_lhs(m01, x):
    x1, x2, x3 = _split3(x)
    d = functools.partial(jnp.dot, preferred_element_type=F32)
    return d(m01, x1) + d(m01, x2) + d(m01, x3)


def _slab_rows(rows, n_steps):
    bf16_sublanes = 16
    return next(r for r in range(bf16_sublanes, rows + 1, bf16_sublanes) if rows % r == 0 and rows // r <= n_steps)


def _mm_kernel(*refs, norm, cast, has_res, n_side, w_transposed):
    it = iter(refs)
    x_ref = next(it)
    g_ref = next(it) if norm else None
    w_ref = next(it)
    res_ref = next(it) if has_res else None
    side_in = [next(it) for _ in range(n_side)]
    o_ref = next(it)
    for src_ref in side_in:
        dst_ref = next(it)
        dst_ref[...] = src_ref[...].astype(BF16)
    if cast:
        xs_ref = next(it)

        @pl.when(pl.program_id(1) == 0)
        def _():
            x = x_ref[...].astype(F32)
            if norm:
                x = _rms(x, g_ref[...])
            xs_ref[...] = x.astype(BF16)

        a = xs_ref[...]
    else:
        a = x_ref[...]
    if w_transposed:
        acc = lax.dot_general(a, w_ref[...], (((1,), (1,)), ((), ())), preferred_element_type=F32)
    else:
        acc = jnp.dot(a, w_ref[...], preferred_element_type=F32)
    if has_res:
        acc = acc + res_ref[...]
    o_ref[...] = acc.astype(o_ref.dtype)


def _mm(x, w, *, gain=None, res=None, out_dtype=F32, tm, tn, name, side=(), w_transposed=False):
    m, k = x.shape
    n = w.shape[0] if w_transposed else w.shape[1]
    tm, tn = min(tm, m), min(tn, n)
    grid = (m // tm, n // tn)
    n_steps = grid[0] * grid[1]
    norm = gain is not None
    cast = norm or x.dtype != BF16
    in_specs = [pl.BlockSpec((tm, k), lambda i, j: (i, 0))]
    args = [x]
    if norm:
        in_specs.append(pl.BlockSpec((1, k), lambda i, j: (0, 0)))
        args.append(gain.reshape(1, k).astype(F32))
    in_specs.append(pl.BlockSpec((tn, k), lambda i, j: (j, 0)) if w_transposed
                    else pl.BlockSpec((k, tn), lambda i, j: (0, j)))
    args.append(w)
    if res is not None:
        in_specs.append(pl.BlockSpec((tm, tn), lambda i, j: (i, j)))
        args.append(res)
    out_specs = [pl.BlockSpec((tm, tn), lambda i, j: (i, j))]
    out_shape = [jax.ShapeDtypeStruct((m, n), out_dtype)]
    for arr, first_row, n_rows in side:
        slab = _slab_rows(n_rows, n_steps)
        n_slabs = n_rows // slab
        assert first_row % slab == 0
        first_slab = first_row // slab
        in_specs.append(pl.BlockSpec(
            (slab, arr.shape[1]),
            lambda i, j, n_slabs=n_slabs, first_slab=first_slab: (first_slab + jnp.minimum(i * grid[1] + j, n_slabs - 1), 0)))
        out_specs.append(pl.BlockSpec(
            (slab, arr.shape[1]), lambda i, j, n_slabs=n_slabs: (jnp.minimum(i * grid[1] + j, n_slabs - 1), 0)))
        out_shape.append(jax.ShapeDtypeStruct((n_rows, arr.shape[1]), BF16))
        args.append(arr)
    outs = pl.pallas_call(
        functools.partial(_mm_kernel, norm=norm, cast=cast, has_res=res is not None, n_side=len(side),
                          w_transposed=w_transposed),
        grid=grid,
        in_specs=in_specs,
        out_specs=out_specs,
        out_shape=out_shape,
        scratch_shapes=[pltpu.VMEM((tm, k), BF16)] if cast else [],
        compiler_params=_cp("arbitrary", "arbitrary"),
        name=name,
    )(*args)
    return outs if side else outs[0]


def _mm_parts_kernel(*refs, widths):
    n = len(widths)
    x_refs, w_ref, res_ref, o_ref = refs[:n], refs[n], refs[n + 1], refs[n + 2]
    acc = res_ref[...]
    off = 0
    for x_ref, width in zip(x_refs, widths):
        acc = acc + jnp.dot(x_ref[...], w_ref[off:off + width, :], preferred_element_type=F32)
        off += width
    o_ref[...] = acc


def _mm_parts(xs, w, res, *, tm, tn, name):
    m = xs[0].shape[0]
    widths = tuple(x.shape[1] for x in xs)
    k, n = w.shape
    assert sum(widths) == k
    tm, tn = min(tm, m), min(tn, n)
    in_specs = [pl.BlockSpec((tm, width), lambda i, j: (i, 0)) for width in widths]
    in_specs += [pl.BlockSpec((k, tn), lambda i, j: (0, j)), pl.BlockSpec((tm, tn), lambda i, j: (i, j))]
    return pl.pallas_call(
        functools.partial(_mm_parts_kernel, widths=widths),
        grid=(m // tm, n // tn),
        in_specs=in_specs,
        out_specs=pl.BlockSpec((tm, tn), lambda i, j: (i, j)),
        out_shape=jax.ShapeDtypeStruct((m, n), F32),
        compiler_params=_cp("parallel", "arbitrary"),
        name=name,
    )(*xs, w, res)


def _swiglu_up_kernel(x_ref, g_ref, wg_ref, wu_ref, o_ref, xs_ref):
    @pl.when(pl.program_id(1) == 0)
    def _():
        xs_ref[...] = _rms(x_ref[...], g_ref[...]).astype(BF16)

    a = xs_ref[...]
    for c in range(0, o_ref.shape[1], MXU_COLS):
        cs = slice(c, c + MXU_COLS)
        gate = jnp.dot(a, wg_ref[:, cs], preferred_element_type=F32)
        up = jnp.dot(a, wu_ref[:, cs], preferred_element_type=F32)
        o_ref[:, cs] = (_silu(gate) * up).astype(o_ref.dtype)


def _swiglu_up(x, gain, wg, wu, *, tm, tn):
    m, k = x.shape
    n = wg.shape[1]
    tm, tn = min(tm, m), min(tn, n)
    return pl.pallas_call(
        _swiglu_up_kernel,
        grid=(m // tm, n // tn),
        in_specs=[pl.BlockSpec((tm, k), lambda i, j: (i, 0)),
                  pl.BlockSpec((1, k), lambda i, j: (0, 0)),
                  pl.BlockSpec((k, tn), lambda i, j: (0, j)),
                  pl.BlockSpec((k, tn), lambda i, j: (0, j))],
        out_specs=pl.BlockSpec((tm, tn), lambda i, j: (i, j)),
        out_shape=jax.ShapeDtypeStruct((m, n), BF16),
        scratch_shapes=[pltpu.VMEM((tm, k), BF16)],
        compiler_params=_cp("parallel", "arbitrary"),
        name="swiglu_up",
    )(x, gain.reshape(1, k).astype(F32), wg, wu)


def _ple_kernel(*refs, with_next):
    if with_next:
        h_ref, p_ref, wp_ref, g_ref, wg_ref, gn_ref, o_ref, xn_ref = refs
    else:
        h_ref, p_ref, wp_ref, g_ref, wg_ref, o_ref = refs
    h = h_ref[...]
    e = jnp.dot(p_ref[0].astype(BF16), wp_ref[...], preferred_element_type=F32)
    e = _rms(e, g_ref[...])
    gate = jax.nn.sigmoid(jnp.dot(h.astype(BF16), wg_ref[...], preferred_element_type=F32))
    out = h + gate * e
    o_ref[...] = out
    if with_next:
        xn_ref[...] = _rms(out, gn_ref[...]).astype(xn_ref.dtype)


def _ple(h, p_all, layer, wp, gain, wg, *, tm, next_gain=None):
    m, d = h.shape
    dp = p_all.shape[2]
    tm = min(tm, m)
    with_next = next_gain is not None
    row_spec = pl.BlockSpec((tm, d), lambda i: (i, 0))
    vec_spec = pl.BlockSpec((1, d), lambda i: (0, 0))
    in_specs = [row_spec, pl.BlockSpec((1, tm, dp), lambda i: (layer, i, 0)), pl.BlockSpec((dp, d), lambda i: (0, 0)),
                vec_spec, pl.BlockSpec((d, d), lambda i: (0, 0))]
    args = [h, p_all, wp, gain.reshape(1, d).astype(F32), wg]
    out_specs, out_shape = [row_spec], [jax.ShapeDtypeStruct((m, d), F32)]
    if with_next:
        in_specs.append(vec_spec)
        args.append(next_gain.reshape(1, d).astype(F32))
        out_specs.append(row_spec)
        out_shape.append(jax.ShapeDtypeStruct((m, d), BF16))
    outs = pl.pallas_call(
        functools.partial(_ple_kernel, with_next=with_next),
        grid=(m // tm,),
        in_specs=in_specs,
        out_specs=out_specs,
        out_shape=out_shape,
        compiler_params=_cp("parallel"),
        name="ple",
    )(*args)
    return outs if with_next else (outs[0], None)


def _conv_silu_slab(x, first, tail_ref, xbuf_ref, cw, bias):
    rows = x.shape[0]

    @pl.when(first)
    def _():
        tail_ref[...] = jnp.zeros_like(tail_ref)

    xbuf_ref[0:8, :] = tail_ref[...]
    xbuf_ref[8:8 + rows, :] = x
    tail_ref[...] = x[rows - 8:rows, :]
    y = (xbuf_ref[5:5 + rows, :] * cw[0:1, :] + xbuf_ref[6:6 + rows, :] * cw[1:2, :]
         + xbuf_ref[7:7 + rows, :] * cw[2:3, :] + x * cw[3:4, :])
    if bias is not None:
        y = y + bias
    return _silu(y)


def _gdn_kernel(qkv_ref, z_ref, sm_ref, cw_ref, vec_ref, ng_ref, o_ref, state_ref, tail_ref, xbuf_ref, xc_ref):
    rows = GDN_BLOCK
    c = GDN_CHUNK
    first = pl.program_id(1) == 0

    @pl.when(first)
    def _():
        state_ref[...] = jnp.zeros_like(state_ref)

    for s in range(3):
        sl = slice(s * GDN_WIDTH, (s + 1) * GDN_WIDTH)
        xc_ref[:, sl] = _conv_silu_slab(qkv_ref[0, :, sl], first, tail_ref.at[s], xbuf_ref, cw_ref[:, sl], None)

    sm = sm_ref[0]
    beta_all = jax.nn.sigmoid(sm)
    g_all = vec_ref[0:1, :] * jax.nn.softplus(sm + vec_ref[1:2, :])

    row = lax.broadcasted_iota(I32, (rows, rows), 0)
    col = lax.broadcasted_iota(I32, (rows, rows), 1)
    same = (row // c) == (col // c)
    causal = same & (col <= row)
    strict = same & (col < row)
    tri01 = jnp.where(causal, 1.0, 0.0).astype(BF16)
    blk01 = jnp.where(same, 1.0, 0.0).astype(BF16)
    gc_all = _dot_exact_lhs(tri01, g_all)
    gl_all = _dot_exact_lhs(blk01, g_all)
    gc_all_t = gc_all.T

    heads = range(GDN_HEADS)
    eye = jnp.where(row == col, 1.0, 0.0).astype(F32)
    q_l, k_l, vb_l, kb_l, gc_l, gl_l, decay_l, p_l, x_l = [], [], [], [], [], [], [], [], []
    for h in heads:
        la = LANE_GDN_A + h
        q = xc_ref[:, h * GDN_DH:(h + 1) * GDN_DH]
        k = xc_ref[:, GDN_WIDTH + h * GDN_DH:GDN_WIDTH + (h + 1) * GDN_DH]
        v = xc_ref[:, 2 * GDN_WIDTH + h * GDN_DH:2 * GDN_WIDTH + (h + 1) * GDN_DH]
        q = q * lax.rsqrt(jnp.sum(q * q, axis=-1, keepdims=True) + EPS) * (GDN_DH ** -0.5)
        k = k * lax.rsqrt(jnp.sum(k * k, axis=-1, keepdims=True) + EPS)
        beta = beta_all[:, LANE_GDN_B + h:LANE_GDN_B + h + 1]
        gc = gc_all[:, la:la + 1]
        decay = jnp.where(causal, jnp.exp(gc - gc_all_t[la:la + 1, :]), 0.0)
        kb = k * beta
        m = jnp.where(strict, _bdot_nt(kb, k) * decay, 0.0)
        q_l.append(q), k_l.append(k), vb_l.append(v * beta), kb_l.append(kb)
        gc_l.append(gc), gl_l.append(gl_all[:, la:la + 1]), decay_l.append(decay)
        p_l.append(m), x_l.append(eye - m)

    span = 2
    while span < c:
        for h in heads:
            p_l[h] = _bdot(p_l[h], p_l[h])
        for h in heads:
            x_l[h] = x_l[h] + _bdot(x_l[h], p_l[h])
        span *= 2

    u_l, w_l, intra_l, qd_l, kd_l, ge_l = [], [], [], [], [], []
    for h in heads:
        eg = jnp.exp(gc_l[h])
        sol = _bdot(x_l[h], jnp.concatenate([vb_l[h], kb_l[h] * eg], axis=1))
        u_l.append(sol[:, :GDN_DH]), w_l.append(sol[:, GDN_DH:])
        intra_l.append((_bdot_nt(q_l[h], k_l[h]) * decay_l[h]).astype(BF16))
        qd_l.append(q_l[h] * eg)
        kd_l.append(k_l[h] * jnp.exp(gl_l[h] - gc_l[h]))
        ge_l.append(jnp.exp(gl_l[h]))

    st_l = [state_ref[h] for h in heads]
    out_l = [[] for _ in heads]
    for ci in range(rows // c):
        rs = slice(ci * c, (ci + 1) * c)
        for h in heads:
            st = st_l[h]
            v_new = u_l[h][rs] - _bdot(w_l[h][rs], st)
            pieces = []
            if ci > 0:
                pieces.append(jnp.zeros((ci * c, GDN_DH), F32))
            pieces.append(v_new)
            if (ci + 1) * c < rows:
                pieces.append(jnp.zeros((rows - (ci + 1) * c, GDN_DH), F32))
            v_pad = jnp.concatenate(pieces, axis=0) if len(pieces) > 1 else v_new
            out_l[h].append(_bdot(qd_l[h][rs], st)
                            + jnp.dot(intra_l[h][rs], v_pad.astype(BF16), preferred_element_type=F32))
            st_l[h] = st * ge_l[h][ci * c:ci * c + 1, :] + _bdot_tn(kd_l[h][rs], v_new)

    for h in heads:
        hs = slice(h * GDN_DH, (h + 1) * GDN_DH)
        state_ref[h] = st_l[h]
        o = jnp.concatenate(out_l[h], axis=0)
        o = _rms(o, ng_ref[...]) * _silu(z_ref[0, :, hs])
        o_ref[0, :, hs] = o.astype(o_ref.dtype)


def _gdn(u3, conv_w, a_log, dt_bias, norm_g):
    b, s, _ = u3.shape
    cw = conv_w.T.astype(F32)
    vec = jnp.zeros((2, LANES), F32)
    vec = vec.at[0, LANE_GDN_A:LANE_GDN_A + GDN_HEADS].set(-jnp.exp(a_log.astype(F32)))
    vec = vec.at[1, LANE_GDN_A:LANE_GDN_A + GDN_HEADS].set(dt_bias.astype(F32))
    blk = GDN_BLOCK
    return pl.pallas_call(
        _gdn_kernel,
        grid=(b, s // blk),
        in_specs=[pl.BlockSpec((1, blk, 3 * GDN_WIDTH), lambda i, j: (i, j, COL_GDN_QKV // (3 * GDN_WIDTH))),
                  pl.BlockSpec((1, blk, GDN_WIDTH), lambda i, j: (i, j, COL_GDN_Z // GDN_WIDTH)),
                  pl.BlockSpec((1, blk, LANES), lambda i, j: (i, j, COL_SMALL // LANES)),
                  pl.BlockSpec((4, 3 * GDN_WIDTH), lambda i, j: (0, 0)),
                  pl.BlockSpec((2, LANES), lambda i, j: (0, 0)),
                  pl.BlockSpec((1, GDN_DH), lambda i, j: (0, 0))],
        out_specs=pl.BlockSpec((1, blk, GDN_WIDTH), lambda i, j: (i, j, 0)),
        out_shape=jax.ShapeDtypeStruct((b, s, GDN_WIDTH), BF16),
        scratch_shapes=[pltpu.VMEM((GDN_HEADS, GDN_DH, GDN_DH), F32),
                        pltpu.VMEM((3, 8, GDN_WIDTH), F32),
                        pltpu.VMEM((blk + 8, GDN_WIDTH), F32),
                        pltpu.VMEM((blk, 3 * GDN_WIDTH), F32)],
        compiler_params=_cp("parallel", "arbitrary"),
        name="gdn",
    )(u3, u3, u3, cw, vec, norm_g.reshape(1, GDN_DH).astype(F32))


def _ssd_kernel(z_ref, xbc_ref, sm_ref, cw_ref, cb_ref, vec_ref, dsk_ref, ng_ref, o_ref,
                state_ref, tail_ref, xbuf_ref, xc_ref, y_ref):
    rows = SEQ_BLOCK
    first = pl.program_id(1) == 0
    half = SSD_P

    @pl.when(first)
    def _():
        state_ref[...] = jnp.zeros_like(state_ref)

    slab = 512
    for s in range(SSD_CONV_DIM // slab):
        sl = slice(s * slab, (s + 1) * slab)
        xc_ref[:, sl] = _conv_silu_slab(xbc_ref[0, :, sl], first, tail_ref.at[s], xbuf_ref, cw_ref[:, sl], cb_ref[:, sl])

    dt_all = jax.nn.softplus(sm_ref[0] + vec_ref[1:2, :])
    da_all = dt_all * vec_ref[0:1, :]
    row = lax.broadcasted_iota(I32, (rows, rows), 0)
    col = lax.broadcasted_iota(I32, (rows, rows), 1)
    causal = col <= row
    tri01 = jnp.where(causal, 1.0, 0.0).astype(BF16)
    acs = _dot_exact_lhs(tri01, da_all)
    acs_t = acs.T
    lane = lax.broadcasted_iota(I32, (rows, LANES), 1)
    lo = lane < half
    lane1 = lax.broadcasted_iota(I32, (1, LANES), 1)
    lo1 = lane1 < half

    d = functools.partial(jnp.dot, preferred_element_type=F32)
    pairs_per_group = SSD_HEADS // SSD_GROUPS // 2
    n_pairs = SSD_HEADS // 2
    cb_l, bmt_l, cmb_l = [], [], []
    for g in range(SSD_GROUPS):
        bm = xc_ref[:, SSD_WIDTH + g * SSD_N:SSD_WIDTH + (g + 1) * SSD_N]
        cm = xc_ref[:, SSD_WIDTH + (SSD_GROUPS + g) * SSD_N:SSD_WIDTH + (SSD_GROUPS + g + 1) * SSD_N]
        cb_l.append(_bdot_nt(cm, bm))
        bmt_l.append(bm.T)
        cmb_l.append(cm.astype(BF16))

    att_l, dec_l, xdt_l, x_l, expa_l, ge_l = [], [], [], [], [], []
    for p in range(n_pairs):
        g = p // pairs_per_group
        la = LANE_SSD_DT + 2 * p
        lb = la + 1
        col_a, col_b = acs[:, la:la + 1], acs[:, lb:lb + 1]
        row_a, row_b = acs_t[la:la + 1, :], acs_t[lb:lb + 1, :]
        last_a, last_b = row_a[:, rows - 1:rows], row_b[:, rows - 1:rows]
        att_l.append(((cb_l[g] * jnp.where(causal, jnp.exp(col_a - row_a), 0.0)).astype(BF16),
                      (cb_l[g] * jnp.where(causal, jnp.exp(col_b - row_b), 0.0)).astype(BF16)))
        dec_l.append(((bmt_l[g] * jnp.exp(last_a - row_a)).astype(BF16),
                      (bmt_l[g] * jnp.exp(last_b - row_b)).astype(BF16)))
        x_pair = xc_ref[:, p * LANES:(p + 1) * LANES]
        xdt = x_pair * jnp.where(lo, dt_all[:, la:la + 1], dt_all[:, lb:lb + 1])
        xdt_l.append((jnp.where(lo, xdt, 0.0).astype(BF16), jnp.where(lo, 0.0, xdt).astype(BF16)))
        x_l.append(x_pair)
        expa_l.append(jnp.where(lo, jnp.exp(col_a), jnp.exp(col_b)))
        ge_l.append(jnp.where(lo1, jnp.exp(last_a), jnp.exp(last_b)))

    for p in range(n_pairs):
        g = p // pairs_per_group
        ps = slice(p * LANES, (p + 1) * LANES)
        xdt_a, xdt_b = xdt_l[p]
        y_diag = d(att_l[p][0], xdt_a) + d(att_l[p][1], xdt_b)
        st_new = d(dec_l[p][0], xdt_a) + d(dec_l[p][1], xdt_b)
        prev = state_ref[p]
        y_off = d(cmb_l[g], prev.astype(BF16)) * expa_l[p]
        state_ref[p] = prev * ge_l[p] + st_new
        y_ref[:, ps] = y_diag + y_off + x_l[p] * dsk_ref[:, ps]

    gw = SSD_WIDTH // SSD_GROUPS
    for g in range(SSD_GROUPS):
        gs = slice(g * gw, (g + 1) * gw)
        y = y_ref[:, gs] * _silu(z_ref[0, :, gs])
        o_ref[0, :, gs] = _rms(y, ng_ref[:, gs]).astype(o_ref.dtype)


def _ssd(u3, conv_w, conv_b, a_log, dt_bias, d_skip, norm_g):
    b, s, _ = u3.shape
    cw = conv_w.T.astype(F32)
    vec = jnp.zeros((2, LANES), F32)
    vec = vec.at[0, LANE_SSD_DT:LANE_SSD_DT + SSD_HEADS].set(-jnp.exp(a_log.astype(F32)))
    vec = vec.at[1, LANE_SSD_DT:LANE_SSD_DT + SSD_HEADS].set(dt_bias.astype(F32))
    dsk = jnp.repeat(d_skip.astype(F32), SSD_P).reshape(1, SSD_WIDTH)
    blk = SEQ_BLOCK
    return pl.pallas_call(
        _ssd_kernel,
        grid=(b, s // blk),
        in_specs=[pl.BlockSpec((1, blk, SSD_WIDTH), lambda i, j: (i, j, COL_SSD_Z // SSD_WIDTH)),
                  pl.BlockSpec((1, blk, SSD_CONV_DIM), lambda i, j: (i, j, COL_SSD_XBC // SSD_CONV_DIM)),
                  pl.BlockSpec((1, blk, LANES), lambda i, j: (i, j, COL_SMALL // LANES)),
                  pl.BlockSpec((4, SSD_CONV_DIM), lambda i, j: (0, 0)),
                  pl.BlockSpec((1, SSD_CONV_DIM), lambda i, j: (0, 0)),
                  pl.BlockSpec((2, LANES), lambda i, j: (0, 0)),
                  pl.BlockSpec((1, SSD_WIDTH), lambda i, j: (0, 0)),
                  pl.BlockSpec((1, SSD_WIDTH), lambda i, j: (0, 0))],
        out_specs=pl.BlockSpec((1, blk, SSD_WIDTH), lambda i, j: (i, j, 0)),
        out_shape=jax.ShapeDtypeStruct((b, s, SSD_WIDTH), BF16),
        scratch_shapes=[pltpu.VMEM((SSD_HEADS // 2, SSD_N, 2 * SSD_P), F32),
                        pltpu.VMEM((SSD_CONV_DIM // 512, 8, 512), F32),
                        pltpu.VMEM((blk + 8, 512), F32),
                        pltpu.VMEM((blk, SSD_CONV_DIM), F32),
                        pltpu.VMEM((blk, SSD_WIDTH), F32)],
        compiler_params=_cp("parallel", "arbitrary"),
        name="ssd",
    )(u3, u3, u3, cw, conv_b.reshape(1, SSD_CONV_DIM).astype(F32), vec, dsk,
      norm_g.reshape(1, SSD_WIDTH).astype(F32))


def _mla_pre_kernel(ql_ref, kpe_ref, kvl_ref, pos_ref, qag_ref, kvag_ref, wq_ref, wkv_ref, vec_ref,
                    q_ref, k_ref, v_ref):
    qn = _rms(ql_ref[0], qag_ref[...])
    qall = jnp.dot(qn.astype(BF16), wq_ref[...], preferred_element_type=F32)
    kvn = _rms(kvl_ref[0], kvag_ref[...])
    kv = jnp.dot(kvn.astype(BF16), wkv_ref[...], preferred_element_type=F32)
    kpe = kpe_ref[0]
    ang = pos_ref[0] * vec_ref[6:7, :]
    cos = jnp.cos(ang)
    sin = jnp.sin(ang) * vec_ref[7:8, :]
    gqn, gqa, gqb = vec_ref[0:1, :], vec_ref[1:2, :], vec_ref[2:3, :]
    gkn, gka, gkb = vec_ref[3:4, :], vec_ref[4:5, :], vec_ref[5:6, :]
    scale = MLA_QK ** -0.5 * LOG2_E
    k_rot = kpe * (cos * gka) + pltpu.roll(kpe, 64, 1) * (sin * gkb)
    k_pe_ss = 0.5 * jnp.sum(kpe * kpe, axis=-1, keepdims=True)
    nh = MLA_HEADS
    for h in range(nh):
        q_nope = qall[:, h * LANES:(h + 1) * LANES]
        q_pe = qall[:, (nh + h) * LANES:(nh + h + 1) * LANES]
        ss = jnp.sum(q_nope * q_nope, axis=-1, keepdims=True) + 0.5 * jnp.sum(q_pe * q_pe, axis=-1, keepdims=True)
        rstd = lax.rsqrt(ss * (1.0 / MLA_QK) + EPS)
        q_rot = q_pe * (cos * gqa) + pltpu.roll(q_pe, 64, 1) * (sin * gqb)
        q_ref[0, h, :, 0:LANES] = (q_nope * gqn * rstd * scale).astype(q_ref.dtype)
        q_ref[0, h, :, LANES:2 * LANES] = (q_rot * rstd * (0.5 * scale)).astype(q_ref.dtype)
        k_nope = kv[:, 2 * h * LANES:(2 * h + 1) * LANES]
        ssk = jnp.sum(k_nope * k_nope, axis=-1, keepdims=True) + k_pe_ss
        rstdk = lax.rsqrt(ssk * (1.0 / MLA_QK) + EPS)
        k_ref[0, h, :, 0:LANES] = (k_nope * gkn * rstdk).astype(k_ref.dtype)
        k_ref[0, h, :, LANES:2 * LANES] = (k_rot * rstdk).astype(k_ref.dtype)
        v_ref[0, h, :, 0:LANES] = kv[:, (2 * h + 1) * LANES:(2 * h + 2) * LANES].astype(v_ref.dtype)
        v_ref[0, h, :, LANES:2 * LANES] = jnp.ones((kv.shape[0], LANES), v_ref.dtype)


def _rope_pair_gains(g):
    g1, g2 = g[MLA_NOPE:MLA_NOPE + 32], g[MLA_NOPE + 32:MLA_NOPE + 64]
    return jnp.concatenate([g1, g2, g2, g1]), jnp.concatenate([g2, g1, g1, g2])


def _mla_pre(u3, pos_f, q_a_g, w_q_b, kv_a_g, w_kv_b, q_norm_g, k_norm_g, *, ts):
    b, s, _ = u3.shape
    nh = MLA_HEADS
    wq = w_q_b.reshape(MLA_Q_RANK, nh, MLA_QK)
    x1, x2 = wq[:, :, MLA_NOPE:MLA_NOPE + 32], wq[:, :, MLA_NOPE + 32:]
    wq_all = jnp.concatenate([wq[:, :, :MLA_NOPE].reshape(MLA_Q_RANK, nh * MLA_NOPE),
                              jnp.concatenate([x1, x2, x2, x1], axis=-1).reshape(MLA_Q_RANK, nh * LANES)],
                             axis=1).astype(BF16)
    half = MLA_ROPE // 2
    inv_freq = 1.0 / (ROPE_THETA ** (jnp.arange(half, dtype=F32) / half))
    gqa, gqb = _rope_pair_gains(q_norm_g.astype(F32))
    gka, gkb = _rope_pair_gains(k_norm_g.astype(F32))
    ones = jnp.ones((half,), F32)
    vec = jnp.stack([q_norm_g[:MLA_NOPE].astype(F32), gqa, gqb, k_norm_g[:MLA_NOPE].astype(F32), gka, gkb,
                     jnp.tile(inv_freq, 4), jnp.concatenate([-ones, ones, ones, -ones])])
    ts = min(ts, s)
    qk_shape = jax.ShapeDtypeStruct((b, nh, s, 2 * LANES), BF16)
    return pl.pallas_call(
        _mla_pre_kernel,
        grid=(b, s // ts),
        in_specs=[pl.BlockSpec((1, ts, MLA_Q_RANK), lambda i, j: (i, j, COL_QLAT // MLA_Q_RANK)),
                  pl.BlockSpec((1, ts, LANES), lambda i, j: (i, j, COL_KPE // LANES)),
                  pl.BlockSpec((1, ts, MLA_KV_RANK), lambda i, j: (i, j, COL_KVLAT // MLA_KV_RANK)),
                  pl.BlockSpec((1, ts, 1), lambda i, j: (i, j, 0)),
                  pl.BlockSpec((1, MLA_Q_RANK), lambda i, j: (0, 0)),
                  pl.BlockSpec((1, MLA_KV_RANK), lambda i, j: (0, 0)),
                  pl.BlockSpec((MLA_Q_RANK, 2 * nh * LANES), lambda i, j: (0, 0)),
                  pl.BlockSpec((MLA_KV_RANK, 2 * nh * LANES), lambda i, j: (0, 0)),
                  pl.BlockSpec((8, LANES), lambda i, j: (0, 0))],
        out_specs=[pl.BlockSpec((1, nh, ts, 2 * LANES), lambda i, j: (i, 0, j, 0)),
                   pl.BlockSpec((1, nh, ts, 2 * LANES), lambda i, j: (i, 0, j, 0)),
                   pl.BlockSpec((1, nh, ts, 2 * LANES), lambda i, j: (i, 0, j, 0))],
        out_shape=[qk_shape, qk_shape, qk_shape],
        compiler_params=_cp("parallel", "parallel"),
        name="mla_pre",
    )(u3, u3, u3, pos_f, q_a_g.reshape(1, -1).astype(F32), kv_a_g.reshape(1, -1).astype(F32),
      wq_all, w_kv_b.astype(BF16), vec)


def _flash_kernel(qi_ref, kj_ref, q_ref, k_ref, v_ref, o_ref, m_ref, acc_ref, *, t):
    qi = qi_ref[pl.program_id(1)]
    kj = kj_ref[pl.program_id(1)]
    nh = q_ref.shape[1]

    @pl.when(kj == 0)
    def _():
        m_ref[...] = jnp.full_like(m_ref, -jnp.inf)
        acc_ref[...] = jnp.zeros_like(acc_ref)

    def step(diagonal):
        s_l = [lax.dot_general(q_ref[0, h], k_ref[0, h], (((1,), (1,)), ((), ())), preferred_element_type=F32)
               for h in range(nh)]
        if diagonal:
            keep = lax.broadcasted_iota(I32, (t, t), 0) >= lax.broadcasted_iota(I32, (t, t), 1)
            s_l = [jnp.where(keep, s, -jnp.inf) for s in s_l]
        p_l, alpha_l = [], []
        for h in range(nh):
            m_old = m_ref[h]
            m_new = jnp.maximum(m_old, jnp.max(s_l[h], axis=-1, keepdims=True))
            p_l.append(jnp.exp2(s_l[h] - jnp.tile(m_new, (1, t // LANES))).astype(BF16))
            alpha_l.append(jnp.exp2(m_old - m_new))
            m_ref[h] = m_new
        for h in range(nh):
            pv = jnp.dot(p_l[h], v_ref[0, h], preferred_element_type=F32)
            acc_ref[h] = jnp.tile(alpha_l[h], (1, 2)) * acc_ref[h] + pv

    @pl.when(kj < qi)
    def _():
        step(False)

    @pl.when(kj == qi)
    def _():
        step(True)
        for h in range(nh):
            acc = acc_ref[h]
            o_ref[0, :, h * LANES:(h + 1) * LANES] = (acc[:, :LANES] / acc[:, LANES:]).astype(o_ref.dtype)


def _flash(q, k, v, *, t):
    b, nh, s, dq = q.shape
    t = min(t, s)
    n = s // t
    pairs = [(qi, kj) for qi in range(n) for kj in range(qi + 1)]
    qi_tab = jnp.asarray([pr[0] for pr in pairs], I32)
    kj_tab = jnp.asarray([pr[1] for pr in pairs], I32)
    kv_spec = pl.BlockSpec((1, nh, t, dq), lambda i, pr, qt, kt: (i, 0, kt[pr], 0))
    grid_spec = pltpu.PrefetchScalarGridSpec(
        num_scalar_prefetch=2,
        grid=(b, len(pairs)),
        in_specs=[pl.BlockSpec((1, nh, t, dq), lambda i, pr, qt, kt: (i, 0, qt[pr], 0)), kv_spec, kv_spec],
        out_specs=pl.BlockSpec((1, t, nh * LANES), lambda i, pr, qt, kt: (i, qt[pr], 0)),
        scratch_shapes=[pltpu.VMEM((nh, t, LANES), F32), pltpu.VMEM((nh, t, 2 * LANES), F32)],
    )
    return pl.pallas_call(
        functools.partial(_flash_kernel, t=t),
        grid_spec=grid_spec,
        out_shape=jax.ShapeDtypeStruct((b, s, nh * LANES), BF16),
        compiler_params=_cp("parallel", "arbitrary"),
        name="mla_flash",
    )(qi_tab, kj_tab, q, k, v)


def _router_kernel(h_ref, g_ref, wr_ref, idx_ref, wt_ref):
    xn = _rms(h_ref[...], g_ref[...])
    w = wr_ref[...]
    xh, wh = xn.astype(BF16), w.astype(BF16)
    xl, wl = (xn - xh.astype(F32)).astype(BF16), (w - wh.astype(F32)).astype(BF16)
    d = functools.partial(jnp.dot, preferred_element_type=F32)
    logits = d(xh, wh) + d(xh, wl) + d(xl, wh)
    lane = lax.broadcasted_iota(I32, logits.shape, 1)
    logits = jnp.where(lane < N_EXPERTS, logits, -jnp.inf)
    m1 = jnp.max(logits, axis=-1, keepdims=True)
    i1 = jnp.min(jnp.where(logits == m1, lane, LANES), axis=-1, keepdims=True)
    rest = jnp.where(lane == i1, -jnp.inf, logits)
    m2 = jnp.max(rest, axis=-1, keepdims=True)
    i2 = jnp.min(jnp.where(rest == m2, lane, LANES), axis=-1, keepdims=True)
    e2 = jnp.exp(m2 - m1)
    w1 = 1.0 / (1.0 + e2)
    w2 = e2 / (1.0 + e2)
    idx_ref[...] = jnp.where(lane == 0, i1, jnp.where(lane == 1, i2, 0))
    wt_ref[...] = jnp.where(lane == 0, w1, jnp.where(lane == 1, w2, 0.0))


def _router(h, gain, w_router, *, tm):
    m, d = h.shape
    tm = min(tm, m)
    wr = jnp.zeros((d, LANES), F32).at[:, :N_EXPERTS].set(w_router.astype(F32))
    return pl.pallas_call(
        _router_kernel,
        grid=(m // tm,),
        in_specs=[pl.BlockSpec((tm, d), lambda i: (i, 0)),
                  pl.BlockSpec((1, d), lambda i: (0, 0)),
                  pl.BlockSpec((d, LANES), lambda i: (0, 0))],
        out_specs=[pl.BlockSpec((tm, LANES), lambda i: (i, 0)),
                   pl.BlockSpec((tm, LANES), lambda i: (i, 0))],
        out_shape=[jax.ShapeDtypeStruct((m, LANES), I32),
                   jax.ShapeDtypeStruct((m, LANES), F32)],
        compiler_params=_cp("parallel"),
        name="moe_router",
    )(h, gain.reshape(1, d).astype(F32), wr)


def _row_copy(src_ref, t, dst_ref, r, sem):
    return pltpu.make_async_copy(src_ref.at[pl.ds(t, 1), :], dst_ref.at[pl.ds(r, 1), :], sem)


def _gather_kernel(idx_ref, nu_ref, src_ref, g_ref, o_ref, buf_ref, sem):
    rows = buf_ref.shape[1]
    i = pl.program_id(0)
    n = nu_ref[0]

    def start_step(step, slot):
        def issue(r2, c):
            for k in range(2):
                r = 2 * r2 + k
                _row_copy(src_ref, idx_ref[step * rows + r], buf_ref.at[slot], r, sem.at[slot]).start(priority=k)
            return c
        lax.fori_loop(0, rows // 2, issue, 0, unroll=4)

    @pl.when(i == 0)
    def _():
        start_step(0, 0)

    @pl.when(i + 1 < n)
    def _():
        start_step(i + 1, (i + 1) % 2)

    slot = i % 2

    @pl.when(i < n)
    def _():
        def wait(r, c):
            _row_copy(src_ref, 0, buf_ref.at[slot], r, sem.at[slot]).wait()
            return c

        lax.fori_loop(0, rows, wait, 0, unroll=8)
        o_ref[...] = _rms(buf_ref[slot], g_ref[...]).astype(o_ref.dtype)

    @pl.when(i >= n)
    def _():
        o_ref[...] = jnp.zeros_like(o_ref)


def _gather_norm_rows(src_tok, n_used, h, gain):
    p = src_tok.shape[0]
    d = h.shape[1]
    rows = MOE_TM
    grid_spec = pltpu.PrefetchScalarGridSpec(
        num_scalar_prefetch=2,
        grid=(p // rows,),
        in_specs=[pl.BlockSpec(memory_space=pl.ANY), pl.BlockSpec((1, d), lambda i, idx, nu: (0, 0))],
        out_specs=pl.BlockSpec((rows, d), lambda i, idx, nu: (i, 0)),
        scratch_shapes=[pltpu.VMEM((2, rows, d), F32), pltpu.SemaphoreType.DMA((2,))],
    )
    return pl.pallas_call(
        _gather_kernel,
        grid_spec=grid_spec,
        out_shape=jax.ShapeDtypeStruct((p, d), BF16),
        compiler_params=_cp("arbitrary"),
        name="moe_gather",
    )(src_tok, n_used, h, gain.reshape(1, d).astype(F32))


def _gup_kernel(te_ref, nu_ref, x_ref, wg_ref, wu_ref, wd_ref, o_ref, wdb_ref, wgb_ref, wub_ref):
    i = pl.program_id(1)
    used = i < nu_ref[0]

    wdb_ref[...] = wd_ref[...].astype(BF16)

    @pl.when((i == 0) | (te_ref[i] != te_ref[jnp.maximum(i - 1, 0)]))
    def _():
        wgb_ref[...] = wg_ref[0].astype(BF16)
        wub_ref[...] = wu_ref[0].astype(BF16)

    @pl.when(used)
    def _():
        x = x_ref[...]
        for c in range(0, o_ref.shape[1], MXU_COLS):
            cs = slice(c, c + MXU_COLS)
            gate = jnp.dot(x, wgb_ref[:, cs], preferred_element_type=F32)
            up = jnp.dot(x, wub_ref[:, cs], preferred_element_type=F32)
            o_ref[:, cs] = (_silu(gate) * up).astype(o_ref.dtype)

    @pl.when(jnp.logical_not(used))
    def _():
        o_ref[...] = jnp.zeros_like(o_ref)


def _grouped_up(tile_expert, n_used, xs, wg, wu, wd, *, tm, tf):
    p, d = xs.shape
    f = wg.shape[2]
    n_i = p // tm
    n_steps = (f // tf) * n_i
    wd2 = wd.reshape(-1, wd.shape[-1])
    slab = _slab_rows(wd2.shape[0], n_steps)
    n_slabs = wd2.shape[0] // slab

    def slab_map(j, i, te, nu):
        return (jnp.minimum(j * n_i + i, n_slabs - 1), 0)

    grid_spec = pltpu.PrefetchScalarGridSpec(
        num_scalar_prefetch=2,
        grid=(f // tf, n_i),
        in_specs=[pl.BlockSpec((tm, d), lambda j, i, te, nu: (jnp.minimum(i, nu[0] - 1), 0)),
                  pl.BlockSpec((1, d, tf), lambda j, i, te, nu: (te[i], 0, j)),
                  pl.BlockSpec((1, d, tf), lambda j, i, te, nu: (te[i], 0, j)),
                  pl.BlockSpec((slab, wd2.shape[1]), slab_map)],
        out_specs=[pl.BlockSpec((tm, tf), lambda j, i, te, nu: (i, j)),
                   pl.BlockSpec((slab, wd2.shape[1]), slab_map)],
        scratch_shapes=[pltpu.VMEM((d, tf), BF16), pltpu.VMEM((d, tf), BF16)],
    )
    hff, wdb = pl.pallas_call(
        _gup_kernel,
        grid_spec=grid_spec,
        out_shape=[jax.ShapeDtypeStruct((p, f), BF16), jax.ShapeDtypeStruct(wd2.shape, BF16)],
        compiler_params=_cp("arbitrary", "arbitrary"),
        name="moe_up",
    )(tile_expert, n_used, xs, wg, wu, wd2)
    return hff, wdb.reshape(wd.shape)


def _gdown_kernel(te_ref, nu_ref, x_ref, wd_ref, o_ref):
    used = pl.program_id(1) < nu_ref[0]

    @pl.when(used)
    def _():
        o_ref[...] = jnp.dot(x_ref[...], wd_ref[0], preferred_element_type=F32)

    @pl.when(jnp.logical_not(used))
    def _():
        o_ref[...] = jnp.zeros_like(o_ref)


def _grouped_down(tile_expert, n_used, hff, wd, *, tm, tn):
    p, f = hff.shape
    d = wd.shape[2]
    grid_spec = pltpu.PrefetchScalarGridSpec(
        num_scalar_prefetch=2,
        grid=(d // tn, p // tm),
        in_specs=[pl.BlockSpec((tm, f), lambda j, i, te, nu: (jnp.minimum(i, nu[0] - 1), 0)),
                  pl.BlockSpec((1, f, tn), lambda j, i, te, nu: (te[i], 0, j))],
        out_specs=pl.BlockSpec((tm, tn), lambda j, i, te, nu: (i, j)),
    )
    return pl.pallas_call(
        _gdown_kernel,
        grid_spec=grid_spec,
        out_shape=jax.ShapeDtypeStruct((p, d), F32),
        compiler_params=_cp("arbitrary", "arbitrary"),
        name="moe_down",
    )(tile_expert, n_used, hff, wd)


def _combine_kernel(pos_ref, h_ref, wt_ref, y_ref, o_ref, buf_ref, sem):
    rows = h_ref.shape[0]
    i = pl.program_id(0)
    n = pl.num_programs(0)

    def start_step(step, slot):
        def issue(r, c):
            base = 2 * (step * rows + r)
            _row_copy(y_ref, pos_ref[base], buf_ref.at[slot, 0], r, sem.at[slot]).start(priority=0)
            _row_copy(y_ref, pos_ref[base + 1], buf_ref.at[slot, 1], r, sem.at[slot]).start(priority=1)
            return c
        lax.fori_loop(0, rows, issue, 0, unroll=4)

    @pl.when(i == 0)
    def _():
        start_step(0, 0)

    @pl.when(i + 1 < n)
    def _():
        start_step(i + 1, (i + 1) % 2)

    slot = i % 2

    def wait(r, c):
        _row_copy(y_ref, 0, buf_ref.at[slot, 0], r, sem.at[slot]).wait()
        _row_copy(y_ref, 0, buf_ref.at[slot, 1], r, sem.at[slot]).wait()
        return c

    lax.fori_loop(0, rows, wait, 0, unroll=4)
    wt = wt_ref[...]
    o_ref[...] = h_ref[...] + wt[:, 0:1] * buf_ref[slot, 0] + wt[:, 1:2] * buf_ref[slot, 1]


def _combine(h, y, pos, wts):
    m, d = h.shape
    rows = min(COMBINE_ROWS, m)
    grid_spec = pltpu.PrefetchScalarGridSpec(
        num_scalar_prefetch=1,
        grid=(m // rows,),
        in_specs=[pl.BlockSpec((rows, d), lambda i, pos: (i, 0)),
                  pl.BlockSpec((rows, LANES), lambda i, pos: (i, 0)),
                  pl.BlockSpec(memory_space=pl.ANY)],
        out_specs=pl.BlockSpec((rows, d), lambda i, pos: (i, 0)),
        scratch_shapes=[pltpu.VMEM((2, 2, rows, d), F32), pltpu.SemaphoreType.DMA((2,))],
    )
    return pl.pallas_call(
        _combine_kernel,
        grid_spec=grid_spec,
        out_shape=jax.ShapeDtypeStruct((m, d), F32),
        compiler_params=_cp("arbitrary"),
        name="moe_combine",
    )(pos, h, wts, y)


def _moe(h, gain, w_router, wg, wu, wd):
    t, d = h.shape
    tm = MOE_TM
    n2 = 2 * t
    idx, wts = _router(h, gain, w_router, tm=512)
    flat_e = idx[:, :2].reshape(-1)
    onehot = (flat_e[:, None] == jnp.arange(N_EXPERTS, dtype=I32)[None, :]).astype(I32)
    csum = jnp.cumsum(onehot, axis=0)
    rank = jnp.sum((csum - onehot) * onehot, axis=1)
    counts = csum[-1]
    padded = ((counts + tm - 1) // tm) * tm
    ends = jnp.cumsum(padded)
    starts = ends - padded
    pos = (starts[flat_e] + rank).astype(I32)
    p_rows = n2 + N_EXPERTS * tm
    n_tiles = p_rows // tm
    n_used = (ends[-1] // tm).astype(I32).reshape(1)
    tile_start = jnp.arange(n_tiles, dtype=I32) * tm
    tile_expert = jnp.sum((tile_start[:, None] >= ends[None, :]).astype(I32), axis=1)
    last_expert = jnp.sum((ends[-1] - 1 >= ends).astype(I32))
    tile_expert = jnp.minimum(tile_expert, last_expert).astype(I32)
    order = jnp.sort(flat_e * n2 + jnp.arange(n2, dtype=I32)) % n2
    row = jnp.arange(p_rows, dtype=I32)
    row_e = jnp.repeat(tile_expert, tm)
    local = row - starts[row_e]
    first = (jnp.cumsum(counts) - counts)[row_e]
    src_tok = jnp.where(local < counts[row_e], order[jnp.clip(first + local, 0, n2 - 1)] // 2, 0).astype(I32)

    xs = _gather_norm_rows(src_tok, n_used, h, gain)
    hff, wd_bf16 = _grouped_up(tile_expert, n_used, xs, wg, wu, wd, tm=tm, tf=1024)
    y = _grouped_down(tile_expert, n_used, hff, wd_bf16, tm=tm, tn=1024)
    return _combine(h, y, pos, wts)


def _w_in_moves():
    gdn0, ssd0 = 0, 4 * GDN_WIDTH + 2 * GDN_HEADS
    mla0 = ssd0 + SSD_WIDTH + SSD_CONV_DIM + SSD_HEADS
    kpe0 = mla0 + MLA_Q_RANK + MLA_KV_RANK
    gb0 = gdn0 + 4 * GDN_WIDTH
    dt0 = ssd0 + SSD_WIDTH + SSD_CONV_DIM
    half = MLA_ROPE // 2
    moves = [(COL_GDN_QKV, gdn0, 4 * GDN_WIDTH),
             (COL_SSD_Z, ssd0, SSD_WIDTH + SSD_CONV_DIM),
             (COL_QLAT, mla0, MLA_Q_RANK),
             (COL_KPE, kpe0, MLA_ROPE), (COL_KPE + MLA_ROPE, kpe0 + half, half),
             (COL_KPE + MLA_ROPE + half, kpe0, half),
             (COL_KVLAT, mla0 + MLA_Q_RANK, MLA_KV_RANK),
             (COL_SMALL + LANE_GDN_B, gb0, 2 * GDN_HEADS), (COL_SMALL + LANE_SSD_DT, dt0, SSD_HEADS)]
    small_end = LANE_SSD_DT + SSD_HEADS
    zeros = [(COL_SMALL, LANE_GDN_B), (COL_SMALL + small_end, LANES - small_end), (COL_SMALL + LANES, LANES)]
    assert LANE_GDN_A == LANE_GDN_B + GDN_HEADS and LANE_SSD_DT == LANE_GDN_A + GDN_HEADS
    assert sum(m[2] for m in moves) + sum(z[1] for z in zeros) == U_COLS
    return moves, zeros


def _w_in_kernel(w_ref, o_ref):
    moves, zeros = _w_in_moves()
    cols = o_ref.shape[1]
    small_lo, small_hi = COL_SMALL, COL_SMALL + LANES
    pieces = {}
    for dst, src, width in moves:
        if small_lo <= dst < small_hi:
            pieces[dst] = w_ref[0, src:src + width, :]
        else:
            o_ref[dst:dst + width, :] = w_ref[0, src:src + width, :].astype(o_ref.dtype)
    for dst, width in zeros:
        if small_lo <= dst < small_hi:
            pieces[dst] = jnp.zeros((width, cols), F32)
        else:
            o_ref[dst:dst + width, :] = jnp.zeros((width, cols), o_ref.dtype)
    small = jnp.concatenate([pieces[k] for k in sorted(pieces)], axis=0)
    o_ref[small_lo:small_hi, :] = small.astype(o_ref.dtype)


def _rearranged_w_in_t(w_in_all, layer):
    _, d, n = w_in_all.shape
    w_t = jnp.transpose(w_in_all, (0, 2, 1))
    tc = min(512, d)
    return pl.pallas_call(
        _w_in_kernel,
        grid=(d // tc,),
        in_specs=[pl.BlockSpec((1, n, tc), lambda i: (layer, 0, i))],
        out_specs=pl.BlockSpec((U_COLS, tc), lambda i: (0, i)),
        out_shape=jax.ShapeDtypeStruct((U_COLS, d), BF16),
        compiler_params=_cp("parallel"),
        name="w_in_prep",
    )(w_t)


def kernel(x, p, positions, norm_mix_g, w_in, w_out, gdn_conv_w, gdn_a_log, gdn_dt_bias, gdn_norm_g, ssd_conv_w, ssd_conv_b, ssd_a_log, ssd_dt_bias, ssd_d, ssd_norm_g, mla_q_a_g, mla_w_q_b, mla_kv_a_g, mla_w_kv_b, mla_q_norm_g, mla_k_norm_g, norm_ffn_g, ffn_w_gate, ffn_w_up, ffn_w_down, router_w, moe_w_gate, moe_w_up, moe_w_down, ple_w_proj, ple_w_gate, ple_norm_g):
    b, s, d = x.shape
    t = b * s
    depth = w_in.shape[0]
    h = x.reshape(t, d).astype(F32)
    pos_f = positions.astype(F32).reshape(b, s, 1)
    p_all = p.reshape(depth, t, p.shape[-1])
    d_ff = ffn_w_gate.shape[-1]
    w_out2, ple_gate2 = w_out.reshape(depth * d, d), ple_w_gate.reshape(depth * d, d)
    xn = None
    for i in range(depth):
        side = [(w_out2, i * d, d), (ple_gate2, i * d, d)]
        if i % 2 == 0:
            side += [(ffn_w_gate.reshape(-1, d_ff), (i // 2) * d, d), (ffn_w_up.reshape(-1, d_ff), (i // 2) * d, d),
                     (ffn_w_down.reshape(-1, d), (i // 2) * d_ff, d_ff)]
        if xn is None:
            proj_in = dict(x=h, gain=norm_mix_g[i])
        else:
            proj_in = dict(x=xn)
        u, w_out_b, ple_gate_b, *ffn_b = _mm(w=_rearranged_w_in_t(w_in, i), tm=1024, tn=1408, name="in_proj",
                                             side=side, w_transposed=True, **proj_in)
        u3 = u.reshape(b, s, U_COLS)
        o_gdn = _gdn(u3, gdn_conv_w[i], gdn_a_log[i], gdn_dt_bias[i], gdn_norm_g[i])
        o_ssd = _ssd(u3, ssd_conv_w[i], ssd_conv_b[i], ssd_a_log[i], ssd_dt_bias[i], ssd_d[i], ssd_norm_g[i])
        q, k, v = _mla_pre(u3, pos_f, mla_q_a_g[i], mla_w_q_b[i], mla_kv_a_g[i], mla_w_kv_b[i],
                           mla_q_norm_g[i], mla_k_norm_g[i], ts=512)
        o_mla = _flash(q, k, v, t=512)
        mix_parts = [o.reshape(t, o.shape[-1]) for o in (o_gdn, o_ssd, o_mla)]
        h = _mm_parts(mix_parts, w_out_b, h, tm=1024, tn=1024, name="out_proj")
        j = i // 2
        if i % 2 == 0:
            ff = _swiglu_up(h, norm_ffn_g[i], ffn_b[0], ffn_b[1], tm=1024, tn=512)
            h = _mm(ff, ffn_b[2], res=h, tm=1024, tn=512, name="ffn_down")
        else:
            h = _moe(h, norm_ffn_g[i], router_w[j], moe_w_gate[j], moe_w_up[j], moe_w_down[j])
        h, xn = _ple(h, p_all, i, ple_w_proj[i].astype(BF16), ple_norm_g[i], ple_gate_b, tm=512,
                     next_gain=norm_mix_g[i + 1] if i + 1 < depth else None)
    return h.reshape(b, s, d).astype(x.dtype)
```

```python
import functools

import jax
import jax.numpy as jnp
from jax import lax
from jax.experimental import pallas as pl
from jax.experimental.pallas import tpu as pltpu

F32 = jnp.float32
BF16 = jnp.bfloat16
U32 = jnp.uint32
I32 = jnp.int32
EPS = 1e-6

D_MODEL = 2048
GDN_HEADS, GDN_DH, GDN_CHUNK = 4, 128, 64
GDN_WIDTH = GDN_HEADS * GDN_DH
SSD_HEADS, SSD_P, SSD_GROUPS, SSD_N, SSD_CHUNK = 16, 64, 2, 128, 256
SSD_WIDTH = SSD_HEADS * SSD_P
SSD_CONV_DIM = SSD_WIDTH + 2 * SSD_GROUPS * SSD_N
MLA_HEADS, MLA_Q_RANK, MLA_KV_RANK = 4, 384, 256
MLA_NOPE, MLA_ROPE, MLA_V = 128, 64, 128
MLA_QK = MLA_NOPE + MLA_ROPE
ROPE_THETA = 10000.0
N_EXPERTS = 8
LANES = 128
MXU_COLS = 256
LOG2_E = 1.4426950408889634

U_COLS = 5632
COL_GDN_QKV, COL_GDN_Z, COL_SSD_Z, COL_SSD_XBC = 0, 1536, 2048, 3072
COL_QLAT, COL_KPE, COL_KVLAT, COL_SMALL = 4608, 4992, 5120, 5376
LANE_GDN_B, LANE_GDN_A, LANE_SSD_DT = 64, 68, 72

SEQ_BLOCK = 256
GDN_BLOCK = 256
VMEM_LIMIT_BYTES = 56 * 1024 * 1024
MOE_TM = 512
COMBINE_ROWS = 128


def _cp(*sem):
    return pltpu.CompilerParams(dimension_semantics=sem, vmem_limit_bytes=VMEM_LIMIT_BYTES)


def _rms(x, gain):
    return x * lax.rsqrt(jnp.mean(x * x, axis=-1, keepdims=True) + EPS) * gain


def _silu(x):
    return x * jax.nn.sigmoid(x)


def _bdot(a, b):
    return jnp.dot(a.astype(BF16), b.astype(BF16), preferred_element_type=F32)


def _bdot_nt(a, b):
    return lax.dot_general(a.astype(BF16), b.astype(BF16), (((1,), (1,)), ((), ())), preferred_element_type=F32)


def _bdot_tn(a, b):
    return lax.dot_general(a.astype(BF16), b.astype(BF16), (((0,), (0,)), ((), ())), preferred_element_type=F32)


def _split3(x):
    x1 = x.astype(BF16)
    r1 = x - x1.astype(F32)
    x2 = r1.astype(BF16)
    x3 = (r1 - x2.astype(F32)).astype(BF16)
    return x1, x2, x3


def _dot_exact_lhs(m01, x):
    x1, x2, x3 = _split3(x)
    d = functools.partial(jnp.dot, preferred_element_type=F32)
    return d(m01, x1) + d(m01, x2) + d(m01, x3)


def _slab_rows(rows, n_steps):
    bf16_sublanes = 16
    return next(r for r in range(bf16_sublanes, rows + 1, bf16_sublanes) if rows % r == 0 and rows // r <= n_steps)


def _mm_kernel(*refs, norm, cast, has_res, n_side, w_transposed):
    it = iter(refs)
    x_ref = next(it)
    g_ref = next(it) if norm else None
    w_ref = next(it)
    res_ref = next(it) if has_res else None
    side_in = [next(it) for _ in range(n_side)]
    o_ref = next(it)
    for src_ref in side_in:
        dst_ref = next(it)
        dst_ref[...] = src_ref[...].astype(BF16)
    if cast:
        xs_ref = next(it)

        @pl.when(pl.program_id(1) == 0)
        def _():
            x = x_ref[...].astype(F32)
            if norm:
                x = _rms(x, g_ref[...])
            xs_ref[...] = x.astype(BF16)

        a = xs_ref[...]
    else:
        a = x_ref[...]
    if w_transposed:
        acc = lax.dot_general(a, w_ref[...], (((1,), (1,)), ((), ())), preferred_element_type=F32)
    else:
        acc = jnp.dot(a, w_ref[...], preferred_element_type=F32)
    if has_res:
        acc = acc + res_ref[...]
    o_ref[...] = acc.astype(o_ref.dtype)


def _mm(x, w, *, gain=None, res=None, out_dtype=F32, tm, tn, name, side=(), w_transposed=False):
    m, k = x.shape
    n = w.shape[0] if w_transposed else w.shape[1]
    tm, tn = min(tm, m), min(tn, n)
    grid = (m // tm, n // tn)
    n_steps = grid[0] * grid[1]
    norm = gain is not None
    cast = norm or x.dtype != BF16
    in_specs = [pl.BlockSpec((tm, k), lambda i, j: (i, 0))]
    args = [x]
    if norm:
        in_specs.append(pl.BlockSpec((1, k), lambda i, j: (0, 0)))
        args.append(gain.reshape(1, k).astype(F32))
    in_specs.append(pl.BlockSpec((tn, k), lambda i, j: (j, 0)) if w_transposed
                    else pl.BlockSpec((k, tn), lambda i, j: (0, j)))
    args.append(w)
    if res is not None:
        in_specs.append(pl.BlockSpec((tm, tn), lambda i, j: (i, j)))
        args.append(res)
    out_specs = [pl.BlockSpec((tm, tn), lambda i, j: (i, j))]
    out_shape = [jax.ShapeDtypeStruct((m, n), out_dtype)]
    for arr, first_row, n_rows in side:
        slab = _slab_rows(n_rows, n_steps)
        n_slabs = n_rows // slab
        assert first_row % slab == 0
        first_slab = first_row // slab
        in_specs.append(pl.BlockSpec(
            (slab, arr.shape[1]),
            lambda i, j, n_slabs=n_slabs, first_slab=first_slab: (first_slab + jnp.minimum(i * grid[1] + j, n_slabs - 1), 0)))
        out_specs.append(pl.BlockSpec(
            (slab, arr.shape[1]), lambda i, j, n_slabs=n_slabs: (jnp.minimum(i * grid[1] + j, n_slabs - 1), 0)))
        out_shape.append(jax.ShapeDtypeStruct((n_rows, arr.shape[1]), BF16))
        args.append(arr)
    outs = pl.pallas_call(
        functools.partial(_mm_kernel, norm=norm, cast=cast, has_res=res is not None, n_side=len(side),
                          w_transposed=w_transposed),
        grid=grid,
        in_specs=in_specs,
        out_specs=out_specs,
        out_shape=out_shape,
        scratch_shapes=[pltpu.VMEM((tm, k), BF16)] if cast else [],
        compiler_params=_cp("arbitrary", "arbitrary"),
        name=name,
    )(*args)
    return outs if side else outs[0]


def _mm_parts_kernel(*refs, widths, with_next):
    n = len(widths)
    x_refs, w_ref, res_ref = refs[:n], refs[n], refs[n + 1]
    acc = res_ref[...]
    off = 0
    for x_ref, width in zip(x_refs, widths):
        acc = acc + jnp.dot(x_ref[...], w_ref[off:off + width, :], preferred_element_type=F32)
        off += width
    if with_next:
        gn_ref, o_ref, xn_ref = refs[n + 2:]
        xn_ref[...] = _rms(acc, gn_ref[...]).astype(xn_ref.dtype)
    else:
        o_ref = refs[n + 2]
    o_ref[...] = acc


def _mm_parts(xs, w, res, *, tm, tn, name, next_gain=None):
    m = xs[0].shape[0]
    widths = tuple(x.shape[1] for x in xs)
    k, n = w.shape
    assert sum(widths) == k
    tm, tn = min(tm, m), min(tn, n)
    with_next = next_gain is not None
    assert not with_next or tn == n
    tile = pl.BlockSpec((tm, tn), lambda i, j: (i, j))
    in_specs = [pl.BlockSpec((tm, width), lambda i, j: (i, 0)) for width in widths]
    in_specs += [pl.BlockSpec((k, tn), lambda i, j: (0, j)), tile]
    args = [*xs, w, res]
    out_specs, out_shape = [tile], [jax.ShapeDtypeStruct((m, n), F32)]
    if with_next:
        in_specs.append(pl.BlockSpec((1, n), lambda i, j: (0, 0)))
        args.append(next_gain.reshape(1, n).astype(F32))
        out_specs.append(tile)
        out_shape.append(jax.ShapeDtypeStruct((m, n), BF16))
    outs = pl.pallas_call(
        functools.partial(_mm_parts_kernel, widths=widths, with_next=with_next),
        grid=(m // tm, n // tn),
        in_specs=in_specs,
        out_specs=out_specs,
        out_shape=out_shape,
        compiler_params=_cp("parallel", "arbitrary"),
        name=name,
    )(*args)
    return outs if with_next else (outs[0], None)


def _swiglu_up_kernel(xn_ref, wg_ref, wu_ref, o_ref):
    a = xn_ref[...]
    for c in range(0, o_ref.shape[1], MXU_COLS):
        cs = slice(c, c + MXU_COLS)
        gate = jnp.dot(a, wg_ref[:, cs], preferred_element_type=F32)
        up = jnp.dot(a, wu_ref[:, cs], preferred_element_type=F32)
        o_ref[:, cs] = (_silu(gate) * up).astype(o_ref.dtype)


def _swiglu_up(xn, wg, wu, *, tm, tn):
    m, k = xn.shape
    n = wg.shape[1]
    tm, tn = min(tm, m), min(tn, n)
    return pl.pallas_call(
        _swiglu_up_kernel,
        grid=(m // tm, n // tn),
        in_specs=[pl.BlockSpec((tm, k), lambda i, j: (i, 0)),
                  pl.BlockSpec((k, tn), lambda i, j: (0, j)),
                  pl.BlockSpec((k, tn), lambda i, j: (0, j))],
        out_specs=pl.BlockSpec((tm, tn), lambda i, j: (i, j)),
        out_shape=jax.ShapeDtypeStruct((m, n), BF16),
        compiler_params=_cp("parallel", "arbitrary"),
        name="swiglu_up",
    )(xn, wg, wu)


def _ple_kernel(*refs, with_next):
    if with_next:
        h_ref, p_ref, wp_ref, g_ref, wg_ref, gn_ref, o_ref, xn_ref = refs
    else:
        h_ref, p_ref, wp_ref, g_ref, wg_ref, o_ref = refs
    h = h_ref[...]
    e = jnp.dot(p_ref[0].astype(BF16), wp_ref[...], preferred_element_type=F32)
    e = _rms(e, g_ref[...])
    gate = jax.nn.sigmoid(jnp.dot(h.astype(BF16), wg_ref[...], preferred_element_type=F32))
    out = h + gate * e
    o_ref[...] = out
    if with_next:
        xn_ref[...] = _rms(out, gn_ref[...]).astype(xn_ref.dtype)


def _ple(h, p_all, layer, wp, gain, wg, *, tm, next_gain=None):
    m, d = h.shape
    dp = p_all.shape[2]
    tm = min(tm, m)
    with_next = next_gain is not None
    row_spec = pl.BlockSpec((tm, d), lambda i: (i, 0))
    vec_spec = pl.BlockSpec((1, d), lambda i: (0, 0))
    in_specs = [row_spec, pl.BlockSpec((1, tm, dp), lambda i: (layer, i, 0)), pl.BlockSpec((dp, d), lambda i: (0, 0)),
                vec_spec, pl.BlockSpec((d, d), lambda i: (0, 0))]
    args = [h, p_all, wp, gain.reshape(1, d).astype(F32), wg]
    out_specs, out_shape = [row_spec], [jax.ShapeDtypeStruct((m, d), F32)]
    if with_next:
        in_specs.append(vec_spec)
        args.append(next_gain.reshape(1, d).astype(F32))
        out_specs.append(row_spec)
        out_shape.append(jax.ShapeDtypeStruct((m, d), BF16))
    outs = pl.pallas_call(
        functools.partial(_ple_kernel, with_next=with_next),
        grid=(m // tm,),
        in_specs=in_specs,
        out_specs=out_specs,
        out_shape=out_shape,
        compiler_params=_cp("parallel"),
        name="ple",
    )(*args)
    return outs if with_next else (outs[0], None)


def _conv_silu_slab(x, first, tail_ref, xbuf_ref, cw, bias):
    rows = x.shape[0]

    @pl.when(first)
    def _():
        tail_ref[...] = jnp.zeros_like(tail_ref)

    xbuf_ref[0:8, :] = tail_ref[...]
    xbuf_ref[8:8 + rows, :] = x
    tail_ref[...] = x[rows - 8:rows, :]
    y = (xbuf_ref[5:5 + rows, :] * cw[0:1, :] + xbuf_ref[6:6 + rows, :] * cw[1:2, :]
         + xbuf_ref[7:7 + rows, :] * cw[2:3, :] + x * cw[3:4, :])
    if bias is not None:
        y = y + bias
    return _silu(y)


def _gdn_kernel(qkv_ref, z_ref, sm_ref, cw_ref, vec_ref, ng_ref, o_ref, state_ref, tail_ref, xbuf_ref, xc_ref):
    rows = GDN_BLOCK
    c = GDN_CHUNK
    first = pl.program_id(1) == 0

    @pl.when(first)
    def _():
        state_ref[...] = jnp.zeros_like(state_ref)

    for s in range(3):
        sl = slice(s * GDN_WIDTH, (s + 1) * GDN_WIDTH)
        xc_ref[:, sl] = _conv_silu_slab(qkv_ref[0, :, sl], first, tail_ref.at[s], xbuf_ref, cw_ref[:, sl], None)

    sm = sm_ref[0]
    beta_all = jax.nn.sigmoid(sm)
    g_all = vec_ref[0:1, :] * jax.nn.softplus(sm + vec_ref[1:2, :])

    row = lax.broadcasted_iota(I32, (rows, rows), 0)
    col = lax.broadcasted_iota(I32, (rows, rows), 1)
    same = (row // c) == (col // c)
    causal = same & (col <= row)
    strict = same & (col < row)
    tri01 = jnp.where(causal, 1.0, 0.0).astype(BF16)
    blk01 = jnp.where(same, 1.0, 0.0).astype(BF16)
    gc_all = _dot_exact_lhs(tri01, g_all)
    gl_all = _dot_exact_lhs(blk01, g_all)
    gc_all_t = gc_all.T

    heads = range(GDN_HEADS)
    eye = jnp.where(row == col, 1.0, 0.0).astype(F32)
    q_l, k_l, vb_l, kb_l, gc_l, gl_l, decay_l, p_l, x_l = [], [], [], [], [], [], [], [], []
    for h in heads:
        la = LANE_GDN_A + h
        q = xc_ref[:, h * GDN_DH:(h + 1) * GDN_DH]
        k = xc_ref[:, GDN_WIDTH + h * GDN_DH:GDN_WIDTH + (h + 1) * GDN_DH]
        v = xc_ref[:, 2 * GDN_WIDTH + h * GDN_DH:2 * GDN_WIDTH + (h + 1) * GDN_DH]
        q = q * lax.rsqrt(jnp.sum(q * q, axis=-1, keepdims=True) + EPS) * (GDN_DH ** -0.5)
        k = k * lax.rsqrt(jnp.sum(k * k, axis=-1, keepdims=True) + EPS)
        beta = beta_all[:, LANE_GDN_B + h:LANE_GDN_B + h + 1]
        gc = gc_all[:, la:la + 1]
        decay = jnp.where(causal, jnp.exp(gc - gc_all_t[la:la + 1, :]), 0.0)
        kb = k * beta
        m = jnp.where(strict, _bdot_nt(kb, k) * decay, 0.0)
        q_l.append(q), k_l.append(k), vb_l.append(v * beta), kb_l.append(kb)
        gc_l.append(gc), gl_l.append(gl_all[:, la:la + 1]), decay_l.append(decay)
        p_l.append(m), x_l.append(eye - m)

    span = 2
    while span < c:
        for h in heads:
            p_l[h] = _bdot(p_l[h], p_l[h])
        for h in heads:
            x_l[h] = x_l[h] + _bdot(x_l[h], p_l[h])
        span *= 2

    u_l, w_l, intra_l, qd_l, kd_l, ge_l = [], [], [], [], [], []
    for h in heads:
        eg = jnp.exp(gc_l[h])
        sol = _bdot(x_l[h], jnp.concatenate([vb_l[h], kb_l[h] * eg], axis=1))
        u_l.append(sol[:, :GDN_DH]), w_l.append(sol[:, GDN_DH:])
        intra_l.append((_bdot_nt(q_l[h], k_l[h]) * decay_l[h]).astype(BF16))
        qd_l.append(q_l[h] * eg)
        kd_l.append(k_l[h] * jnp.exp(gl_l[h] - gc_l[h]))
        ge_l.append(jnp.exp(gl_l[h]))

    st_l = [state_ref[h] for h in heads]
    out_l = [[] for _ in heads]
    for ci in range(rows // c):
        rs = slice(ci * c, (ci + 1) * c)
        for h in heads:
            st = st_l[h]
            v_new = u_l[h][rs] - _bdot(w_l[h][rs], st)
            pieces = []
            if ci > 0:
                pieces.append(jnp.zeros((ci * c, GDN_DH), F32))
            pieces.append(v_new)
            if (ci + 1) * c < rows:
                pieces.append(jnp.zeros((rows - (ci + 1) * c, GDN_DH), F32))
            v_pad = jnp.concatenate(pieces, axis=0) if len(pieces) > 1 else v_new
            out_l[h].append(_bdot(qd_l[h][rs], st)
                            + jnp.dot(intra_l[h][rs], v_pad.astype(BF16), preferred_element_type=F32))
            st_l[h] = st * ge_l[h][ci * c:ci * c + 1, :] + _bdot_tn(kd_l[h][rs], v_new)

    for h in heads:
        hs = slice(h * GDN_DH, (h + 1) * GDN_DH)
        state_ref[h] = st_l[h]
        o = jnp.concatenate(out_l[h], axis=0)
        o = _rms(o, ng_ref[...]) * _silu(z_ref[0, :, hs])
        o_ref[0, :, hs] = o.astype(o_ref.dtype)


def _gdn(u3, conv_w, a_log, dt_bias, norm_g):
    b, s, _ = u3.shape
    cw = conv_w.T.astype(F32)
    vec = jnp.zeros((2, LANES), F32)
    vec = vec.at[0, LANE_GDN_A:LANE_GDN_A + GDN_HEADS].set(-jnp.exp(a_log.astype(F32)))
    vec = vec.at[1, LANE_GDN_A:LANE_GDN_A + GDN_HEADS].set(dt_bias.astype(F32))
    blk = GDN_BLOCK
    return pl.pallas_call(
        _gdn_kernel,
        grid=(b, s // blk),
        in_specs=[pl.BlockSpec((1, blk, 3 * GDN_WIDTH), lambda i, j: (i, j, COL_GDN_QKV // (3 * GDN_WIDTH))),
                  pl.BlockSpec((1, blk, GDN_WIDTH), lambda i, j: (i, j, COL_GDN_Z // GDN_WIDTH)),
                  pl.BlockSpec((1, blk, LANES), lambda i, j: (i, j, COL_SMALL // LANES)),
                  pl.BlockSpec((4, 3 * GDN_WIDTH), lambda i, j: (0, 0)),
                  pl.BlockSpec((2, LANES), lambda i, j: (0, 0)),
                  pl.BlockSpec((1, GDN_DH), lambda i, j: (0, 0))],
        out_specs=pl.BlockSpec((1, blk, GDN_WIDTH), lambda i, j: (i, j, 0)),
        out_shape=jax.ShapeDtypeStruct((b, s, GDN_WIDTH), BF16),
        scratch_shapes=[pltpu.VMEM((GDN_HEADS, GDN_DH, GDN_DH), F32),
                        pltpu.VMEM((3, 8, GDN_WIDTH), F32),
                        pltpu.VMEM((blk + 8, GDN_WIDTH), F32),
                        pltpu.VMEM((blk, 3 * GDN_WIDTH), F32)],
        compiler_params=_cp("parallel", "arbitrary"),
        name="gdn",
    )(u3, u3, u3, cw, vec, norm_g.reshape(1, GDN_DH).astype(F32))


def _ssd_kernel(z_ref, xbc_ref, sm_ref, cw_ref, cb_ref, vec_ref, dsk_ref, ng_ref, o_ref,
                state_ref, tail_ref, xbuf_ref, xc_ref, y_ref):
    rows = SEQ_BLOCK
    first = pl.program_id(1) == 0
    half = SSD_P

    @pl.when(first)
    def _():
        state_ref[...] = jnp.zeros_like(state_ref)

    slab = 512
    for s in range(SSD_CONV_DIM // slab):
        sl = slice(s * slab, (s + 1) * slab)
        xc_ref[:, sl] = _conv_silu_slab(xbc_ref[0, :, sl], first, tail_ref.at[s], xbuf_ref, cw_ref[:, sl], cb_ref[:, sl])

    dt_all = jax.nn.softplus(sm_ref[0] + vec_ref[1:2, :])
    da_all = dt_all * vec_ref[0:1, :]
    row = lax.broadcasted_iota(I32, (rows, rows), 0)
    col = lax.broadcasted_iota(I32, (rows, rows), 1)
    causal = col <= row
    tri01 = jnp.where(causal, 1.0, 0.0).astype(BF16)
    acs = _dot_exact_lhs(tri01, da_all)
    acs_t = acs.T
    lane = lax.broadcasted_iota(I32, (rows, LANES), 1)
    lo = lane < half
    lane1 = lax.broadcasted_iota(I32, (1, LANES), 1)
    lo1 = lane1 < half

    d = functools.partial(jnp.dot, preferred_element_type=F32)
    pairs_per_group = SSD_HEADS // SSD_GROUPS // 2
    n_pairs = SSD_HEADS // 2
    cb_l, bmt_l, cmb_l = [], [], []
    for g in range(SSD_GROUPS):
        bm = xc_ref[:, SSD_WIDTH + g * SSD_N:SSD_WIDTH + (g + 1) * SSD_N]
        cm = xc_ref[:, SSD_WIDTH + (SSD_GROUPS + g) * SSD_N:SSD_WIDTH + (SSD_GROUPS + g + 1) * SSD_N]
        cb_l.append(_bdot_nt(cm, bm))
        bmt_l.append(bm.T)
        cmb_l.append(cm.astype(BF16))

    att_l, dec_l, xdt_l, x_l, expa_l, ge_l = [], [], [], [], [], []
    for p in range(n_pairs):
        g = p // pairs_per_group
        la = LANE_SSD_DT + 2 * p
        lb = la + 1
        col_a, col_b = acs[:, la:la + 1], acs[:, lb:lb + 1]
        row_a, row_b = acs_t[la:la + 1, :], acs_t[lb:lb + 1, :]
        last_a, last_b = row_a[:, rows - 1:rows], row_b[:, rows - 1:rows]
        att_l.append(((cb_l[g] * jnp.where(causal, jnp.exp(col_a - row_a), 0.0)).astype(BF16),
                      (cb_l[g] * jnp.where(causal, jnp.exp(col_b - row_b), 0.0)).astype(BF16)))
        dec_l.append(((bmt_l[g] * jnp.exp(last_a - row_a)).astype(BF16),
                      (bmt_l[g] * jnp.exp(last_b - row_b)).astype(BF16)))
        x_pair = xc_ref[:, p * LANES:(p + 1) * LANES]
        xdt = x_pair * jnp.where(lo, dt_all[:, la:la + 1], dt_all[:, lb:lb + 1])
        xdt_l.append((jnp.where(lo, xdt, 0.0).astype(BF16), jnp.where(lo, 0.0, xdt).astype(BF16)))
        x_l.append(x_pair)
        expa_l.append(jnp.where(lo, jnp.exp(col_a), jnp.exp(col_b)))
        ge_l.append(jnp.where(lo1, jnp.exp(last_a), jnp.exp(last_b)))

    for p in range(n_pairs):
        g = p // pairs_per_group
        ps = slice(p * LANES, (p + 1) * LANES)
        xdt_a, xdt_b = xdt_l[p]
        y_diag = d(att_l[p][0], xdt_a) + d(att_l[p][1], xdt_b)
        st_new = d(dec_l[p][0], xdt_a) + d(dec_l[p][1], xdt_b)
        prev = state_ref[p]
        y_off = d(cmb_l[g], prev.astype(BF16)) * expa_l[p]
        state_ref[p] = prev * ge_l[p] + st_new
        y_ref[:, ps] = y_diag + y_off + x_l[p] * dsk_ref[:, ps]

    gw = SSD_WIDTH // SSD_GROUPS
    for g in range(SSD_GROUPS):
        gs = slice(g * gw, (g + 1) * gw)
        y = y_ref[:, gs] * _silu(z_ref[0, :, gs])
        o_ref[0, :, gs] = _rms(y, ng_ref[:, gs]).astype(o_ref.dtype)


def _ssd(u3, conv_w, conv_b, a_log, dt_bias, d_skip, norm_g):
    b, s, _ = u3.shape
    cw = conv_w.T.astype(F32)
    vec = jnp.zeros((2, LANES), F32)
    vec = vec.at[0, LANE_SSD_DT:LANE_SSD_DT + SSD_HEADS].set(-jnp.exp(a_log.astype(F32)))
    vec = vec.at[1, LANE_SSD_DT:LANE_SSD_DT + SSD_HEADS].set(dt_bias.astype(F32))
    dsk = jnp.repeat(d_skip.astype(F32), SSD_P).reshape(1, SSD_WIDTH)
    blk = SEQ_BLOCK
    return pl.pallas_call(
        _ssd_kernel,
        grid=(b, s // blk),
        in_specs=[pl.BlockSpec((1, blk, SSD_WIDTH), lambda i, j: (i, j, COL_SSD_Z // SSD_WIDTH)),
                  pl.BlockSpec((1, blk, SSD_CONV_DIM), lambda i, j: (i, j, COL_SSD_XBC // SSD_CONV_DIM)),
                  pl.BlockSpec((1, blk, LANES), lambda i, j: (i, j, COL_SMALL // LANES)),
                  pl.BlockSpec((4, SSD_CONV_DIM), lambda i, j: (0, 0)),
                  pl.BlockSpec((1, SSD_CONV_DIM), lambda i, j: (0, 0)),
                  pl.BlockSpec((2, LANES), lambda i, j: (0, 0)),
                  pl.BlockSpec((1, SSD_WIDTH), lambda i, j: (0, 0)),
                  pl.BlockSpec((1, SSD_WIDTH), lambda i, j: (0, 0))],
        out_specs=pl.BlockSpec((1, blk, SSD_WIDTH), lambda i, j: (i, j, 0)),
        out_shape=jax.ShapeDtypeStruct((b, s, SSD_WIDTH), BF16),
        scratch_shapes=[pltpu.VMEM((SSD_HEADS // 2, SSD_N, 2 * SSD_P), F32),
                        pltpu.VMEM((SSD_CONV_DIM // 512, 8, 512), F32),
                        pltpu.VMEM((blk + 8, 512), F32),
                        pltpu.VMEM((blk, SSD_CONV_DIM), F32),
                        pltpu.VMEM((blk, SSD_WIDTH), F32)],
        compiler_params=_cp("parallel", "arbitrary"),
        name="ssd",
    )(u3, u3, u3, cw, conv_b.reshape(1, SSD_CONV_DIM).astype(F32), vec, dsk,
      norm_g.reshape(1, SSD_WIDTH).astype(F32))


def _mla_pre_kernel(ql_ref, kpe_ref, kvl_ref, pos_ref, qag_ref, kvag_ref, wq_ref, wkv_ref, vec_ref,
                    q_ref, k_ref, v_ref):
    qn = _rms(ql_ref[0], qag_ref[...])
    qall = jnp.dot(qn.astype(BF16), wq_ref[...], preferred_element_type=F32)
    kvn = _rms(kvl_ref[0], kvag_ref[...])
    kv = jnp.dot(kvn.astype(BF16), wkv_ref[...], preferred_element_type=F32)
    kpe = kpe_ref[0]
    ang = pos_ref[0] * vec_ref[6:7, :]
    cos = jnp.cos(ang)
    sin = jnp.sin(ang) * vec_ref[7:8, :]
    gqn, gqa, gqb = vec_ref[0:1, :], vec_ref[1:2, :], vec_ref[2:3, :]
    gkn, gka, gkb = vec_ref[3:4, :], vec_ref[4:5, :], vec_ref[5:6, :]
    scale = MLA_QK ** -0.5 * LOG2_E
    k_rot = kpe * (cos * gka) + pltpu.roll(kpe, 64, 1) * (sin * gkb)
    k_pe_ss = 0.5 * jnp.sum(kpe * kpe, axis=-1, keepdims=True)
    nh = MLA_HEADS
    for h in range(nh):
        q_nope = qall[:, h * LANES:(h + 1) * LANES]
        q_pe = qall[:, (nh + h) * LANES:(nh + h + 1) * LANES]
        ss = jnp.sum(q_nope * q_nope, axis=-1, keepdims=True) + 0.5 * jnp.sum(q_pe * q_pe, axis=-1, keepdims=True)
        rstd = lax.rsqrt(ss * (1.0 / MLA_QK) + EPS)
        q_rot = q_pe * (cos * gqa) + pltpu.roll(q_pe, 64, 1) * (sin * gqb)
        q_ref[0, h, :, 0:LANES] = (q_nope * gqn * rstd * scale).astype(q_ref.dtype)
        q_ref[0, h, :, LANES:2 * LANES] = (q_rot * rstd * (0.5 * scale)).astype(q_ref.dtype)
        k_nope = kv[:, 2 * h * LANES:(2 * h + 1) * LANES]
        ssk = jnp.sum(k_nope * k_nope, axis=-1, keepdims=True) + k_pe_ss
        rstdk = lax.rsqrt(ssk * (1.0 / MLA_QK) + EPS)
        k_ref[0, h, :, 0:LANES] = (k_nope * gkn * rstdk).astype(k_ref.dtype)
        k_ref[0, h, :, LANES:2 * LANES] = (k_rot * rstdk).astype(k_ref.dtype)
        v_ref[0, h, :, 0:LANES] = kv[:, (2 * h + 1) * LANES:(2 * h + 2) * LANES].astype(v_ref.dtype)
        v_ref[0, h, :, LANES:2 * LANES] = jnp.ones((kv.shape[0], LANES), v_ref.dtype)


def _rope_pair_gains(g):
    g1, g2 = g[MLA_NOPE:MLA_NOPE + 32], g[MLA_NOPE + 32:MLA_NOPE + 64]
    return jnp.concatenate([g1, g2, g2, g1]), jnp.concatenate([g2, g1, g1, g2])


def _mla_pre(u3, pos_f, q_a_g, w_q_b, kv_a_g, w_kv_b, q_norm_g, k_norm_g, *, ts):
    b, s, _ = u3.shape
    nh = MLA_HEADS
    wq = w_q_b.reshape(MLA_Q_RANK, nh, MLA_QK)
    x1, x2 = wq[:, :, MLA_NOPE:MLA_NOPE + 32], wq[:, :, MLA_NOPE + 32:]
    wq_all = jnp.concatenate([wq[:, :, :MLA_NOPE].reshape(MLA_Q_RANK, nh * MLA_NOPE),
                              jnp.concatenate([x1, x2, x2, x1], axis=-1).reshape(MLA_Q_RANK, nh * LANES)],
                             axis=1).astype(BF16)
    half = MLA_ROPE // 2
    inv_freq = 1.0 / (ROPE_THETA ** (jnp.arange(half, dtype=F32) / half))
    gqa, gqb = _rope_pair_gains(q_norm_g.astype(F32))
    gka, gkb = _rope_pair_gains(k_norm_g.astype(F32))
    ones = jnp.ones((half,), F32)
    vec = jnp.stack([q_norm_g[:MLA_NOPE].astype(F32), gqa, gqb, k_norm_g[:MLA_NOPE].astype(F32), gka, gkb,
                     jnp.tile(inv_freq, 4), jnp.concatenate([-ones, ones, ones, -ones])])
    ts = min(ts, s)
    qk_shape = jax.ShapeDtypeStruct((b, nh, s, 2 * LANES), BF16)
    return pl.pallas_call(
        _mla_pre_kernel,
        grid=(b, s // ts),
        in_specs=[pl.BlockSpec((1, ts, MLA_Q_RANK), lambda i, j: (i, j, COL_QLAT // MLA_Q_RANK)),
                  pl.BlockSpec((1, ts, LANES), lambda i, j: (i, j, COL_KPE // LANES)),
                  pl.BlockSpec((1, ts, MLA_KV_RANK), lambda i, j: (i, j, COL_KVLAT // MLA_KV_RANK)),
                  pl.BlockSpec((1, ts, 1), lambda i, j: (i, j, 0)),
                  pl.BlockSpec((1, MLA_Q_RANK), lambda i, j: (0, 0)),
                  pl.BlockSpec((1, MLA_KV_RANK), lambda i, j: (0, 0)),
                  pl.BlockSpec((MLA_Q_RANK, 2 * nh * LANES), lambda i, j: (0, 0)),
                  pl.BlockSpec((MLA_KV_RANK, 2 * nh * LANES), lambda i, j: (0, 0)),
                  pl.BlockSpec((8, LANES), lambda i, j: (0, 0))],
        out_specs=[pl.BlockSpec((1, nh, ts, 2 * LANES), lambda i, j: (i, 0, j, 0)),
                   pl.BlockSpec((1, nh, ts, 2 * LANES), lambda i, j: (i, 0, j, 0)),
                   pl.BlockSpec((1, nh, ts, 2 * LANES), lambda i, j: (i, 0, j, 0))],
        out_shape=[qk_shape, qk_shape, qk_shape],
        compiler_params=_cp("parallel", "parallel"),
        name="mla_pre",
    )(u3, u3, u3, pos_f, q_a_g.reshape(1, -1).astype(F32), kv_a_g.reshape(1, -1).astype(F32),
      wq_all, w_kv_b.astype(BF16), vec)


def _flash_kernel(qi_ref, kj_ref, q_ref, k_ref, v_ref, o_ref, m_ref, acc_ref, *, t):
    qi = qi_ref[pl.program_id(1)]
    kj = kj_ref[pl.program_id(1)]
    nh = q_ref.shape[1]

    @pl.when(kj == 0)
    def _():
        m_ref[...] = jnp.full_like(m_ref, -jnp.inf)
        acc_ref[...] = jnp.zeros_like(acc_ref)

    def step(diagonal):
        s_l = [lax.dot_general(q_ref[0, h], k_ref[0, h], (((1,), (1,)), ((), ())), preferred_element_type=F32)
               for h in range(nh)]
        if diagonal:
            keep = lax.broadcasted_iota(I32, (t, t), 0) >= lax.broadcasted_iota(I32, (t, t), 1)
            s_l = [jnp.where(keep, s, -jnp.inf) for s in s_l]
        p_l, alpha_l = [], []
        for h in range(nh):
            m_old = m_ref[h]
            m_new = jnp.maximum(m_old, jnp.max(s_l[h], axis=-1, keepdims=True))
            p_l.append(jnp.exp2(s_l[h] - jnp.tile(m_new, (1, t // LANES))).astype(BF16))
            alpha_l.append(jnp.exp2(m_old - m_new))
            m_ref[h] = m_new
        for h in range(nh):
            pv = jnp.dot(p_l[h], v_ref[0, h], preferred_element_type=F32)
            acc_ref[h] = jnp.tile(alpha_l[h], (1, 2)) * acc_ref[h] + pv

    @pl.when(kj < qi)
    def _():
        step(False)

    @pl.when(kj == qi)
    def _():
        step(True)
        for h in range(nh):
            acc = acc_ref[h]
            o_ref[0, :, h * LANES:(h + 1) * LANES] = (acc[:, :LANES] / acc[:, LANES:]).astype(o_ref.dtype)


def _flash(q, k, v, *, t):
    b, nh, s, dq = q.shape
    t = min(t, s)
    n = s // t
    pairs = [(qi, kj) for qi in range(n) for kj in range(qi + 1)]
    qi_tab = jnp.asarray([pr[0] for pr in pairs], I32)
    kj_tab = jnp.asarray([pr[1] for pr in pairs], I32)
    kv_spec = pl.BlockSpec((1, nh, t, dq), lambda i, pr, qt, kt: (i, 0, kt[pr], 0))
    grid_spec = pltpu.PrefetchScalarGridSpec(
        num_scalar_prefetch=2,
        grid=(b, len(pairs)),
        in_specs=[pl.BlockSpec((1, nh, t, dq), lambda i, pr, qt, kt: (i, 0, qt[pr], 0)), kv_spec, kv_spec],
        out_specs=pl.BlockSpec((1, t, nh * LANES), lambda i, pr, qt, kt: (i, qt[pr], 0)),
        scratch_shapes=[pltpu.VMEM((nh, t, LANES), F32), pltpu.VMEM((nh, t, 2 * LANES), F32)],
    )
    return pl.pallas_call(
        functools.partial(_flash_kernel, t=t),
        grid_spec=grid_spec,
        out_shape=jax.ShapeDtypeStruct((b, s, nh * LANES), BF16),
        compiler_params=_cp("parallel", "arbitrary"),
        name="mla_flash",
    )(qi_tab, kj_tab, q, k, v)


def _router_kernel(h_ref, g_ref, wr_ref, idx_ref, wt_ref):
    xn = _rms(h_ref[...], g_ref[...])
    w = wr_ref[...]
    xh, wh = xn.astype(BF16), w.astype(BF16)
    xl, wl = (xn - xh.astype(F32)).astype(BF16), (w - wh.astype(F32)).astype(BF16)
    d = functools.partial(jnp.dot, preferred_element_type=F32)
    logits = d(xh, wh) + d(xh, wl) + d(xl, wh)
    lane = lax.broadcasted_iota(I32, logits.shape, 1)
    logits = jnp.where(lane < N_EXPERTS, logits, -jnp.inf)
    m1 = jnp.max(logits, axis=-1, keepdims=True)
    i1 = jnp.min(jnp.where(logits == m1, lane, LANES), axis=-1, keepdims=True)
    rest = jnp.where(lane == i1, -jnp.inf, logits)
    m2 = jnp.max(rest, axis=-1, keepdims=True)
    i2 = jnp.min(jnp.where(rest == m2, lane, LANES), axis=-1, keepdims=True)
    e2 = jnp.exp(m2 - m1)
    w1 = 1.0 / (1.0 + e2)
    w2 = e2 / (1.0 + e2)
    idx_ref[...] = jnp.where(lane == 0, i1, jnp.where(lane == 1, i2, 0))
    wt_ref[...] = jnp.where(lane == 0, w1, jnp.where(lane == 1, w2, 0.0))


def _router(h, gain, w_router, *, tm):
    m, d = h.shape
    tm = min(tm, m)
    wr = jnp.zeros((d, LANES), F32).at[:, :N_EXPERTS].set(w_router.astype(F32))
    return pl.pallas_call(
        _router_kernel,
        grid=(m // tm,),
        in_specs=[pl.BlockSpec((tm, d), lambda i: (i, 0)),
                  pl.BlockSpec((1, d), lambda i: (0, 0)),
                  pl.BlockSpec((d, LANES), lambda i: (0, 0))],
        out_specs=[pl.BlockSpec((tm, LANES), lambda i: (i, 0)),
                   pl.BlockSpec((tm, LANES), lambda i: (i, 0))],
        out_shape=[jax.ShapeDtypeStruct((m, LANES), I32),
                   jax.ShapeDtypeStruct((m, LANES), F32)],
        compiler_params=_cp("parallel"),
        name="moe_router",
    )(h, gain.reshape(1, d).astype(F32), wr)


def _row_copy(src_ref, t, dst_ref, r, sem):
    return pltpu.make_async_copy(src_ref.at[pl.ds(t, 1), :], dst_ref.at[pl.ds(r, 1), :], sem)


def _gather_kernel(idx_ref, nu_ref, src_ref, g_ref, o_ref, buf_ref, sem):
    rows = buf_ref.shape[1]
    i = pl.program_id(0)
    n = nu_ref[0]

    def start_step(step, slot):
        def issue(r2, c):
            for k in range(2):
                r = 2 * r2 + k
                _row_copy(src_ref, idx_ref[step * rows + r], buf_ref.at[slot], r, sem.at[slot]).start(priority=k)
            return c
        lax.fori_loop(0, rows // 2, issue, 0, unroll=4)

    @pl.when(i == 0)
    def _():
        start_step(0, 0)

    @pl.when(i + 1 < n)
    def _():
        start_step(i + 1, (i + 1) % 2)

    slot = i % 2

    @pl.when(i < n)
    def _():
        def wait(r, c):
            _row_copy(src_ref, 0, buf_ref.at[slot], r, sem.at[slot]).wait()
            return c

        lax.fori_loop(0, rows, wait, 0, unroll=8)
        o_ref[...] = _rms(buf_ref[slot], g_ref[...]).astype(o_ref.dtype)

    @pl.when(i >= n)
    def _():
        o_ref[...] = jnp.zeros_like(o_ref)


def _gather_norm_rows(src_tok, n_used, h, gain):
    p = src_tok.shape[0]
    d = h.shape[1]
    rows = MOE_TM
    grid_spec = pltpu.PrefetchScalarGridSpec(
        num_scalar_prefetch=2,
        grid=(p // rows,),
        in_specs=[pl.BlockSpec(memory_space=pl.ANY), pl.BlockSpec((1, d), lambda i, idx, nu: (0, 0))],
        out_specs=pl.BlockSpec((rows, d), lambda i, idx, nu: (i, 0)),
        scratch_shapes=[pltpu.VMEM((2, rows, d), F32), pltpu.SemaphoreType.DMA((2,))],
    )
    return pl.pallas_call(
        _gather_kernel,
        grid_spec=grid_spec,
        out_shape=jax.ShapeDtypeStruct((p, d), BF16),
        compiler_params=_cp("arbitrary"),
        name="moe_gather",
    )(src_tok, n_used, h, gain.reshape(1, d).astype(F32))


def _gup_kernel(te_ref, nu_ref, x_ref, wg_ref, wu_ref, wd_ref, o_ref, wdb_ref, wgb_ref, wub_ref):
    i = pl.program_id(1)
    used = i < nu_ref[0]

    wdb_ref[...] = wd_ref[...].astype(BF16)

    @pl.when((i == 0) | (te_ref[i] != te_ref[jnp.maximum(i - 1, 0)]))
    def _():
        wgb_ref[...] = wg_ref[0].astype(BF16)
        wub_ref[...] = wu_ref[0].astype(BF16)

    @pl.when(used)
    def _():
        x = x_ref[...]
        for c in range(0, o_ref.shape[1], MXU_COLS):
            cs = slice(c, c + MXU_COLS)
            gate = jnp.dot(x, wgb_ref[:, cs], preferred_element_type=F32)
            up = jnp.dot(x, wub_ref[:, cs], preferred_element_type=F32)
            o_ref[:, cs] = (_silu(gate) * up).astype(o_ref.dtype)

    @pl.when(jnp.logical_not(used))
    def _():
        o_ref[...] = jnp.zeros_like(o_ref)


def _grouped_up(tile_expert, n_used, xs, wg, wu, wd, *, tm, tf):
    p, d = xs.shape
    f = wg.shape[2]
    n_i = p // tm
    n_steps = (f // tf) * n_i
    wd2 = wd.reshape(-1, wd.shape[-1])
    slab = _slab_rows(wd2.shape[0], n_steps)
    n_slabs = wd2.shape[0] // slab

    def slab_map(j, i, te, nu):
        return (jnp.minimum(j * n_i + i, n_slabs - 1), 0)

    grid_spec = pltpu.PrefetchScalarGridSpec(
        num_scalar_prefetch=2,
        grid=(f // tf, n_i),
        in_specs=[pl.BlockSpec((tm, d), lambda j, i, te, nu: (jnp.minimum(i, nu[0] - 1), 0)),
                  pl.BlockSpec((1, d, tf), lambda j, i, te, nu: (te[i], 0, j)),
                  pl.BlockSpec((1, d, tf), lambda j, i, te, nu: (te[i], 0, j)),
                  pl.BlockSpec((slab, wd2.shape[1]), slab_map)],
        out_specs=[pl.BlockSpec((tm, tf), lambda j, i, te, nu: (i, j)),
                   pl.BlockSpec((slab, wd2.shape[1]), slab_map)],
        scratch_shapes=[pltpu.VMEM((d, tf), BF16), pltpu.VMEM((d, tf), BF16)],
    )
    hff, wdb = pl.pallas_call(
        _gup_kernel,
        grid_spec=grid_spec,
        out_shape=[jax.ShapeDtypeStruct((p, f), BF16), jax.ShapeDtypeStruct(wd2.shape, BF16)],
        compiler_params=_cp("arbitrary", "arbitrary"),
        name="moe_up",
    )(tile_expert, n_used, xs, wg, wu, wd2)
    return hff, wdb.reshape(wd.shape)


def _gdown_kernel(te_ref, nu_ref, x_ref, wd_ref, o_ref):
    used = pl.program_id(1) < nu_ref[0]

    @pl.when(used)
    def _():
        o_ref[...] = jnp.dot(x_ref[...], wd_ref[0], preferred_element_type=F32)

    @pl.when(jnp.logical_not(used))
    def _():
        o_ref[...] = jnp.zeros_like(o_ref)


def _grouped_down(tile_expert, n_used, hff, wd, *, tm, tn):
    p, f = hff.shape
    d = wd.shape[2]
    grid_spec = pltpu.PrefetchScalarGridSpec(
        num_scalar_prefetch=2,
        grid=(d // tn, p // tm),
        in_specs=[pl.BlockSpec((tm, f), lambda j, i, te, nu: (jnp.minimum(i, nu[0] - 1), 0)),
                  pl.BlockSpec((1, f, tn), lambda j, i, te, nu: (te[i], 0, j))],
        out_specs=pl.BlockSpec((tm, tn), lambda j, i, te, nu: (i, j)),
    )
    return pl.pallas_call(
        _gdown_kernel,
        grid_spec=grid_spec,
        out_shape=jax.ShapeDtypeStruct((p, d), F32),
        compiler_params=_cp("arbitrary", "arbitrary"),
        name="moe_down",
    )(tile_expert, n_used, hff, wd)


def _combine_kernel(pos_ref, h_ref, wt_ref, y_ref, o_ref, buf_ref, sem):
    rows = h_ref.shape[0]
    i = pl.program_id(0)
    n = pl.num_programs(0)

    def start_step(step, slot):
        def issue(r, c):
            base = 2 * (step * rows + r)
            _row_copy(y_ref, pos_ref[base], buf_ref.at[slot, 0], r, sem.at[slot]).start(priority=0)
            _row_copy(y_ref, pos_ref[base + 1], buf_ref.at[slot, 1], r, sem.at[slot]).start(priority=1)
            return c
        lax.fori_loop(0, rows, issue, 0, unroll=4)

    @pl.when(i == 0)
    def _():
        start_step(0, 0)

    @pl.when(i + 1 < n)
    def _():
        start_step(i + 1, (i + 1) % 2)

    slot = i % 2

    def wait(r, c):
        _row_copy(y_ref, 0, buf_ref.at[slot, 0], r, sem.at[slot]).wait()
        _row_copy(y_ref, 0, buf_ref.at[slot, 1], r, sem.at[slot]).wait()
        return c

    lax.fori_loop(0, rows, wait, 0, unroll=4)
    wt = wt_ref[...]
    o_ref[...] = h_ref[...] + wt[:, 0:1] * buf_ref[slot, 0] + wt[:, 1:2] * buf_ref[slot, 1]


def _combine(h, y, pos, wts):
    m, d = h.shape
    rows = min(COMBINE_ROWS, m)
    grid_spec = pltpu.PrefetchScalarGridSpec(
        num_scalar_prefetch=1,
        grid=(m // rows,),
        in_specs=[pl.BlockSpec((rows, d), lambda i, pos: (i, 0)),
                  pl.BlockSpec((rows, LANES), lambda i, pos: (i, 0)),
                  pl.BlockSpec(memory_space=pl.ANY)],
        out_specs=pl.BlockSpec((rows, d), lambda i, pos: (i, 0)),
        scratch_shapes=[pltpu.VMEM((2, 2, rows, d), F32), pltpu.SemaphoreType.DMA((2,))],
    )
    return pl.pallas_call(
        _combine_kernel,
        grid_spec=grid_spec,
        out_shape=jax.ShapeDtypeStruct((m, d), F32),
        compiler_params=_cp("arbitrary"),
        name="moe_combine",
    )(pos, h, wts, y)


def _moe(h, gain, w_router, wg, wu, wd):
    t, d = h.shape
    tm = MOE_TM
    n2 = 2 * t
    idx, wts = _router(h, gain, w_router, tm=512)
    flat_e = idx[:, :2].reshape(-1)
    onehot = (flat_e[:, None] == jnp.arange(N_EXPERTS, dtype=I32)[None, :]).astype(I32)
    csum = jnp.cumsum(onehot, axis=0)
    rank = jnp.sum((csum - onehot) * onehot, axis=1)
    counts = csum[-1]
    padded = ((counts + tm - 1) // tm) * tm
    ends = jnp.cumsum(padded)
    starts = ends - padded
    pos = (starts[flat_e] + rank).astype(I32)
    p_rows = n2 + N_EXPERTS * tm
    n_tiles = p_rows // tm
    n_used = (ends[-1] // tm).astype(I32).reshape(1)
    tile_start = jnp.arange(n_tiles, dtype=I32) * tm
    tile_expert = jnp.sum((tile_start[:, None] >= ends[None, :]).astype(I32), axis=1)
    last_expert = jnp.sum((ends[-1] - 1 >= ends).astype(I32))
    tile_expert = jnp.minimum(tile_expert, last_expert).astype(I32)
    order = jnp.sort(flat_e * n2 + jnp.arange(n2, dtype=I32)) % n2
    row = jnp.arange(p_rows, dtype=I32)
    row_e = jnp.repeat(tile_expert, tm)
    local = row - starts[row_e]
    first = (jnp.cumsum(counts) - counts)[row_e]
    src_tok = jnp.where(local < counts[row_e], order[jnp.clip(first + local, 0, n2 - 1)] // 2, row % t).astype(I32)

    xs = _gather_norm_rows(src_tok, n_used, h, gain)
    hff, wd_bf16 = _grouped_up(tile_expert, n_used, xs, wg, wu, wd, tm=tm, tf=1024)
    y = _grouped_down(tile_expert, n_used, hff, wd_bf16, tm=tm, tn=1024)
    return _combine(h, y, pos, wts)


def _w_in_moves():
    gdn0, ssd0 = 0, 4 * GDN_WIDTH + 2 * GDN_HEADS
    mla0 = ssd0 + SSD_WIDTH + SSD_CONV_DIM + SSD_HEADS
    kpe0 = mla0 + MLA_Q_RANK + MLA_KV_RANK
    gb0 = gdn0 + 4 * GDN_WIDTH
    dt0 = ssd0 + SSD_WIDTH + SSD_CONV_DIM
    half = MLA_ROPE // 2
    moves = [(COL_GDN_QKV, gdn0, 4 * GDN_WIDTH),
             (COL_SSD_Z, ssd0, SSD_WIDTH + SSD_CONV_DIM),
             (COL_QLAT, mla0, MLA_Q_RANK),
             (COL_KPE, kpe0, MLA_ROPE), (COL_KPE + MLA_ROPE, kpe0 + half, half),
             (COL_KPE + MLA_ROPE + half, kpe0, half),
             (COL_KVLAT, mla0 + MLA_Q_RANK, MLA_KV_RANK),
             (COL_SMALL + LANE_GDN_B, gb0, 2 * GDN_HEADS), (COL_SMALL + LANE_SSD_DT, dt0, SSD_HEADS)]
    small_end = LANE_SSD_DT + SSD_HEADS
    zeros = [(COL_SMALL, LANE_GDN_B), (COL_SMALL + small_end, LANES - small_end), (COL_SMALL + LANES, LANES)]
    assert LANE_GDN_A == LANE_GDN_B + GDN_HEADS and LANE_SSD_DT == LANE_GDN_A + GDN_HEADS
    assert sum(m[2] for m in moves) + sum(z[1] for z in zeros) == U_COLS
    return moves, zeros


def _w_in_kernel(w_ref, o_ref):
    moves, zeros = _w_in_moves()
    cols = o_ref.shape[1]
    small_lo, small_hi = COL_SMALL, COL_SMALL + LANES
    pieces = {}
    for dst, src, width in moves:
        if small_lo <= dst < small_hi:
            pieces[dst] = w_ref[0, src:src + width, :]
        else:
            o_ref[dst:dst + width, :] = w_ref[0, src:src + width, :].astype(o_ref.dtype)
    for dst, width in zeros:
        if small_lo <= dst < small_hi:
            pieces[dst] = jnp.zeros((width, cols), F32)
        else:
            o_ref[dst:dst + width, :] = jnp.zeros((width, cols), o_ref.dtype)
    small = jnp.concatenate([pieces[k] for k in sorted(pieces)], axis=0)
    o_ref[small_lo:small_hi, :] = small.astype(o_ref.dtype)


def _rearranged_w_in_t(w_in_all, layer):
    _, d, n = w_in_all.shape
    w_t = jnp.transpose(w_in_all, (0, 2, 1))
    tc = min(512, d)
    return pl.pallas_call(
        _w_in_kernel,
        grid=(d // tc,),
        in_specs=[pl.BlockSpec((1, n, tc), lambda i: (layer, 0, i))],
        out_specs=pl.BlockSpec((U_COLS, tc), lambda i: (0, i)),
        out_shape=jax.ShapeDtypeStruct((U_COLS, d), BF16),
        compiler_params=_cp("parallel"),
        name="w_in_prep",
    )(w_t)


def kernel(x, p, positions, norm_mix_g, w_in, w_out, gdn_conv_w, gdn_a_log, gdn_dt_bias, gdn_norm_g, ssd_conv_w, ssd_conv_b, ssd_a_log, ssd_dt_bias, ssd_d, ssd_norm_g, mla_q_a_g, mla_w_q_b, mla_kv_a_g, mla_w_kv_b, mla_q_norm_g, mla_k_norm_g, norm_ffn_g, ffn_w_gate, ffn_w_up, ffn_w_down, router_w, moe_w_gate, moe_w_up, moe_w_down, ple_w_proj, ple_w_gate, ple_norm_g):
    b, s, d = x.shape
    t = b * s
    depth = w_in.shape[0]
    h = x.reshape(t, d).astype(F32)
    pos_f = positions.astype(F32).reshape(b, s, 1)
    p_all = p.reshape(depth, t, p.shape[-1])
    d_ff = ffn_w_gate.shape[-1]
    w_out2, ple_gate2 = w_out.reshape(depth * d, d), ple_w_gate.reshape(depth * d, d)
    xn = None
    for i in range(depth):
        side = [(w_out2, i * d, d), (ple_gate2, i * d, d)]
        if i % 2 == 0:
            side += [(ffn_w_gate.reshape(-1, d_ff), (i // 2) * d, d), (ffn_w_up.reshape(-1, d_ff), (i // 2) * d, d),
                     (ffn_w_down.reshape(-1, d), (i // 2) * d_ff, d_ff)]
        if xn is None:
            proj_in = dict(x=h, gain=norm_mix_g[i])
        else:
            proj_in = dict(x=xn)
        u, w_out_b, ple_gate_b, *ffn_b = _mm(w=_rearranged_w_in_t(w_in, i), tm=1024, tn=1408, name="in_proj",
                                             side=side, w_transposed=True, **proj_in)
        u3 = u.reshape(b, s, U_COLS)
        o_gdn = _gdn(u3, gdn_conv_w[i], gdn_a_log[i], gdn_dt_bias[i], gdn_norm_g[i])
        o_ssd = _ssd(u3, ssd_conv_w[i], ssd_conv_b[i], ssd_a_log[i], ssd_dt_bias[i], ssd_d[i], ssd_norm_g[i])
        q, k, v = _mla_pre(u3, pos_f, mla_q_a_g[i], mla_w_q_b[i], mla_kv_a_g[i], mla_w_kv_b[i],
                           mla_q_norm_g[i], mla_k_norm_g[i], ts=512)
        o_mla = _flash(q, k, v, t=512)
        mix_parts = [o.reshape(t, o.shape[-1]) for o in (o_gdn, o_ssd, o_mla)]
        j = i // 2
        if i % 2 == 0:
            h, hn = _mm_parts(mix_parts, w_out_b, h, tm=512, tn=d, name="out_proj", next_gain=norm_ffn_g[i])
            ff = _swiglu_up(hn, ffn_b[0], ffn_b[1], tm=1024, tn=512)
            h = _mm(ff, ffn_b[2], res=h, tm=1024, tn=512, name="ffn_down")
        else:
            h, _ = _mm_parts(mix_parts, w_out_b, h, tm=1024, tn=1024, name="out_proj")
            h = _moe(h, norm_ffn_g[i], router_w[j], moe_w_gate[j], moe_w_up[j], moe_w_down[j])
        h, xn = _ple(h, p_all, i, ple_w_proj[i].astype(BF16), ple_norm_g[i], ple_gate_b, tm=512,
                     next_gain=norm_mix_g[i + 1] if i + 1 < depth else None)
    return h.reshape(b, s, d).astype(x.dtype)
```

```python
import functools

import jax
import jax.numpy as jnp
from jax import lax
from jax.experimental import pallas as pl
from jax.experimental.pallas import tpu as pltpu

F32 = jnp.float32
BF16 = jnp.bfloat16
U32 = jnp.uint32
I32 = jnp.int32
EPS = 1e-6

D_MODEL = 2048
GDN_HEADS, GDN_DH, GDN_CHUNK = 4, 128, 64
GDN_WIDTH = GDN_HEADS * GDN_DH
SSD_HEADS, SSD_P, SSD_GROUPS, SSD_N, SSD_CHUNK = 16, 64, 2, 128, 256
SSD_WIDTH = SSD_HEADS * SSD_P
SSD_CONV_DIM = SSD_WIDTH + 2 * SSD_GROUPS * SSD_N
MLA_HEADS, MLA_Q_RANK, MLA_KV_RANK = 4, 384, 256
MLA_NOPE, MLA_ROPE, MLA_V = 128, 64, 128
MLA_QK = MLA_NOPE + MLA_ROPE
ROPE_THETA = 10000.0
N_EXPERTS = 8
LANES = 128
MXU_COLS = 256
LOG2_E = 1.4426950408889634

U_COLS = 5632
COL_GDN_QKV, COL_GDN_Z, COL_SSD_Z, COL_SSD_XBC = 0, 1536, 2048, 3072
COL_QLAT, COL_KPE, COL_KVLAT, COL_SMALL = 4608, 4992, 5120, 5376
LANE_GDN_B, LANE_GDN_A, LANE_SSD_DT = 64, 68, 72

SEQ_BLOCK = 256
GDN_BLOCK = 256
VMEM_LIMIT_BYTES = 56 * 1024 * 1024
MOE_TM = 512
COMBINE_ROWS = 256


def _cp(*sem):
    return pltpu.CompilerParams(dimension_semantics=sem, vmem_limit_bytes=VMEM_LIMIT_BYTES)


def _rms(x, gain):
    return x * lax.rsqrt(jnp.mean(x * x, axis=-1, keepdims=True) + EPS) * gain


def _silu(x):
    return x * jax.nn.sigmoid(x)


def _bdot(a, b):
    return jnp.dot(a.astype(BF16), b.astype(BF16), preferred_element_type=F32)


def _bdot_nt(a, b):
    return lax.dot_general(a.astype(BF16), b.astype(BF16), (((1,), (1,)), ((), ())), preferred_element_type=F32)


def _bdot_tn(a, b):
    return lax.dot_general(a.astype(BF16), b.astype(BF16), (((0,), (0,)), ((), ())), preferred_element_type=F32)


def _split3(x):
    x1 = x.astype(BF16)
    r1 = x - x1.astype(F32)
    x2 = r1.astype(BF16)
    x3 = (r1 - x2.astype(F32)).astype(BF16)
    return x1, x2, x3


def _dot_exact_lhs(m01, x):
    x1, x2, x3 = _split3(x)
    d = functools.partial(jnp.dot, preferred_element_type=F32)
    return d(m01, x1) + d(m01, x2) + d(m01, x3)


def _slab_rows(rows, n_steps):
    bf16_sublanes = 16
    return next(r for r in range(bf16_sublanes, rows + 1, bf16_sublanes) if rows % r == 0 and rows // r <= n_steps)


def _mm_kernel(*refs, norm, cast, has_res, n_side, w_transposed):
    it = iter(refs)
    x_ref = next(it)
    g_ref = next(it) if norm else None
    w_ref = next(it)
    res_ref = next(it) if has_res else None
    side_in = [next(it) for _ in range(n_side)]
    o_ref = next(it)
    for src_ref in side_in:
        dst_ref = next(it)
        dst_ref[...] = src_ref[...].astype(BF16)
    if cast:
        xs_ref = next(it)

        @pl.when(pl.program_id(1) == 0)
        def _():
            x = x_ref[...].astype(F32)
            if norm:
                x = _rms(x, g_ref[...])
            xs_ref[...] = x.astype(BF16)

        a = xs_ref[...]
    else:
        a = x_ref[...]
    if w_transposed:
        acc = lax.dot_general(a, w_ref[...], (((1,), (1,)), ((), ())), preferred_element_type=F32)
    else:
        acc = jnp.dot(a, w_ref[...], preferred_element_type=F32)
    if has_res:
        acc = acc + res_ref[...]
    o_ref[...] = acc.astype(o_ref.dtype)


def _mm(x, w, *, gain=None, res=None, out_dtype=F32, tm, tn, name, side=(), w_transposed=False):
    m, k = x.shape
    n = w.shape[0] if w_transposed else w.shape[1]
    tm, tn = min(tm, m), min(tn, n)
    grid = (m // tm, n // tn)
    n_steps = grid[0] * grid[1]
    norm = gain is not None
    cast = norm or x.dtype != BF16
    in_specs = [pl.BlockSpec((tm, k), lambda i, j: (i, 0))]
    args = [x]
    if norm:
        in_specs.append(pl.BlockSpec((1, k), lambda i, j: (0, 0)))
        args.append(gain.reshape(1, k).astype(F32))
    in_specs.append(pl.BlockSpec((tn, k), lambda i, j: (j, 0)) if w_transposed
                    else pl.BlockSpec((k, tn), lambda i, j: (0, j)))
    args.append(w)
    if res is not None:
        in_specs.append(pl.BlockSpec((tm, tn), lambda i, j: (i, j)))
        args.append(res)
    out_specs = [pl.BlockSpec((tm, tn), lambda i, j: (i, j))]
    out_shape = [jax.ShapeDtypeStruct((m, n), out_dtype)]
    for arr, first_row, n_rows in side:
        slab = _slab_rows(n_rows, n_steps)
        n_slabs = n_rows // slab
        assert first_row % slab == 0
        first_slab = first_row // slab
        in_specs.append(pl.BlockSpec(
            (slab, arr.shape[1]),
            lambda i, j, n_slabs=n_slabs, first_slab=first_slab: (first_slab + jnp.minimum(i * grid[1] + j, n_slabs - 1), 0)))
        out_specs.append(pl.BlockSpec(
            (slab, arr.shape[1]), lambda i, j, n_slabs=n_slabs: (jnp.minimum(i * grid[1] + j, n_slabs - 1), 0)))
        out_shape.append(jax.ShapeDtypeStruct((n_rows, arr.shape[1]), BF16))
        args.append(arr)
    outs = pl.pallas_call(
        functools.partial(_mm_kernel, norm=norm, cast=cast, has_res=res is not None, n_side=len(side),
                          w_transposed=w_transposed),
        grid=grid,
        in_specs=in_specs,
        out_specs=out_specs,
        out_shape=out_shape,
        scratch_shapes=[pltpu.VMEM((tm, k), BF16)] if cast else [],
        compiler_params=_cp("arbitrary", "arbitrary"),
        name=name,
    )(*args)
    return outs if side else outs[0]


def _mm_parts_kernel(*refs, widths, with_next):
    n = len(widths)
    x_refs, w_ref, res_ref = refs[:n], refs[n], refs[n + 1]
    acc = res_ref[...]
    off = 0
    for x_ref, width in zip(x_refs, widths):
        acc = acc + jnp.dot(x_ref[...], w_ref[off:off + width, :], preferred_element_type=F32)
        off += width
    if with_next:
        gn_ref, o_ref, xn_ref = refs[n + 2:]
        xn_ref[...] = _rms(acc, gn_ref[...]).astype(xn_ref.dtype)
    else:
        o_ref = refs[n + 2]
    o_ref[...] = acc


def _mm_parts(xs, w, res, *, tm, tn, name, next_gain=None):
    m = xs[0].shape[0]
    widths = tuple(x.shape[1] for x in xs)
    k, n = w.shape
    assert sum(widths) == k
    tm, tn = min(tm, m), min(tn, n)
    with_next = next_gain is not None
    assert not with_next or tn == n
    tile = pl.BlockSpec((tm, tn), lambda i, j: (i, j))
    in_specs = [pl.BlockSpec((tm, width), lambda i, j: (i, 0)) for width in widths]
    in_specs += [pl.BlockSpec((k, tn), lambda i, j: (0, j)), tile]
    args = [*xs, w, res]
    out_specs, out_shape = [tile], [jax.ShapeDtypeStruct((m, n), F32)]
    if with_next:
        in_specs.append(pl.BlockSpec((1, n), lambda i, j: (0, 0)))
        args.append(next_gain.reshape(1, n).astype(F32))
        out_specs.append(tile)
        out_shape.append(jax.ShapeDtypeStruct((m, n), BF16))
    outs = pl.pallas_call(
        functools.partial(_mm_parts_kernel, widths=widths, with_next=with_next),
        grid=(m // tm, n // tn),
        in_specs=in_specs,
        out_specs=out_specs,
        out_shape=out_shape,
        compiler_params=_cp("parallel", "arbitrary"),
        name=name,
    )(*args)
    return outs if with_next else (outs[0], None)


def _swiglu_up_kernel(xn_ref, wg_ref, wu_ref, o_ref):
    a = xn_ref[...]
    for c in range(0, o_ref.shape[1], MXU_COLS):
        cs = slice(c, c + MXU_COLS)
        gate = jnp.dot(a, wg_ref[:, cs], preferred_element_type=F32)
        up = jnp.dot(a, wu_ref[:, cs], preferred_element_type=F32)
        o_ref[:, cs] = (_silu(gate) * up).astype(o_ref.dtype)


def _swiglu_up(xn, wg, wu, *, tm, tn):
    m, k = xn.shape
    n = wg.shape[1]
    tm, tn = min(tm, m), min(tn, n)
    return pl.pallas_call(
        _swiglu_up_kernel,
        grid=(m // tm, n // tn),
        in_specs=[pl.BlockSpec((tm, k), lambda i, j: (i, 0)),
                  pl.BlockSpec((k, tn), lambda i, j: (0, j)),
                  pl.BlockSpec((k, tn), lambda i, j: (0, j))],
        out_specs=pl.BlockSpec((tm, tn), lambda i, j: (i, j)),
        out_shape=jax.ShapeDtypeStruct((m, n), BF16),
        compiler_params=_cp("parallel", "arbitrary"),
        name="swiglu_up",
    )(xn, wg, wu)


def _ple_kernel(*refs, with_next):
    if with_next:
        h_ref, p_ref, wp_ref, g_ref, wg_ref, gn_ref, o_ref, xn_ref = refs
    else:
        h_ref, p_ref, wp_ref, g_ref, wg_ref, o_ref = refs
    h = h_ref[...]
    e = jnp.dot(p_ref[0].astype(BF16), wp_ref[...], preferred_element_type=F32)
    e = _rms(e, g_ref[...])
    gate = jax.nn.sigmoid(jnp.dot(h.astype(BF16), wg_ref[...], preferred_element_type=F32))
    out = h + gate * e
    o_ref[...] = out
    if with_next:
        xn_ref[...] = _rms(out, gn_ref[...]).astype(xn_ref.dtype)


def _ple(h, p_all, layer, wp, gain, wg, *, tm, next_gain=None):
    m, d = h.shape
    dp = p_all.shape[2]
    tm = min(tm, m)
    with_next = next_gain is not None
    row_spec = pl.BlockSpec((tm, d), lambda i: (i, 0))
    vec_spec = pl.BlockSpec((1, d), lambda i: (0, 0))
    in_specs = [row_spec, pl.BlockSpec((1, tm, dp), lambda i: (layer, i, 0)), pl.BlockSpec((dp, d), lambda i: (0, 0)),
                vec_spec, pl.BlockSpec((d, d), lambda i: (0, 0))]
    args = [h, p_all, wp, gain.reshape(1, d).astype(F32), wg]
    out_specs, out_shape = [row_spec], [jax.ShapeDtypeStruct((m, d), F32)]
    if with_next:
        in_specs.append(vec_spec)
        args.append(next_gain.reshape(1, d).astype(F32))
        out_specs.append(row_spec)
        out_shape.append(jax.ShapeDtypeStruct((m, d), BF16))
    outs = pl.pallas_call(
        functools.partial(_ple_kernel, with_next=with_next),
        grid=(m // tm,),
        in_specs=in_specs,
        out_specs=out_specs,
        out_shape=out_shape,
        compiler_params=_cp("parallel"),
        name="ple",
    )(*args)
    return outs if with_next else (outs[0], None)


def _conv_silu_slab(x, first, tail_ref, xbuf_ref, cw, bias):
    rows = x.shape[0]

    @pl.when(first)
    def _():
        tail_ref[...] = jnp.zeros_like(tail_ref)

    xbuf_ref[0:8, :] = tail_ref[...]
    xbuf_ref[8:8 + rows, :] = x
    tail_ref[...] = x[rows - 8:rows, :]
    y = (xbuf_ref[5:5 + rows, :] * cw[0:1, :] + xbuf_ref[6:6 + rows, :] * cw[1:2, :]
         + xbuf_ref[7:7 + rows, :] * cw[2:3, :] + x * cw[3:4, :])
    if bias is not None:
        y = y + bias
    return _silu(y)


def _gdn_kernel(qkv_ref, z_ref, sm_ref, cw_ref, vec_ref, ng_ref, o_ref, state_ref, tail_ref, xbuf_ref, xc_ref):
    rows = GDN_BLOCK
    c = GDN_CHUNK
    first = pl.program_id(1) == 0

    @pl.when(first)
    def _():
        state_ref[...] = jnp.zeros_like(state_ref)

    for s in range(3):
        sl = slice(s * GDN_WIDTH, (s + 1) * GDN_WIDTH)
        xc_ref[:, sl] = _conv_silu_slab(qkv_ref[0, :, sl], first, tail_ref.at[s], xbuf_ref, cw_ref[:, sl], None)

    sm = sm_ref[0]
    beta_all = jax.nn.sigmoid(sm)
    g_all = vec_ref[0:1, :] * jax.nn.softplus(sm + vec_ref[1:2, :])

    row = lax.broadcasted_iota(I32, (rows, rows), 0)
    col = lax.broadcasted_iota(I32, (rows, rows), 1)
    same = (row // c) == (col // c)
    causal = same & (col <= row)
    strict = same & (col < row)
    tri01 = jnp.where(causal, 1.0, 0.0).astype(BF16)
    blk01 = jnp.where(same, 1.0, 0.0).astype(BF16)
    gc_all = _dot_exact_lhs(tri01, g_all)
    gl_all = _dot_exact_lhs(blk01, g_all)
    gc_all_t = gc_all.T

    heads = range(GDN_HEADS)
    eye = jnp.where(row == col, 1.0, 0.0).astype(F32)
    q_l, k_l, vb_l, kb_l, gc_l, gl_l, decay_l, p_l, x_l = [], [], [], [], [], [], [], [], []
    for h in heads:
        la = LANE_GDN_A + h
        q = xc_ref[:, h * GDN_DH:(h + 1) * GDN_DH]
        k = xc_ref[:, GDN_WIDTH + h * GDN_DH:GDN_WIDTH + (h + 1) * GDN_DH]
        v = xc_ref[:, 2 * GDN_WIDTH + h * GDN_DH:2 * GDN_WIDTH + (h + 1) * GDN_DH]
        q = q * lax.rsqrt(jnp.sum(q * q, axis=-1, keepdims=True) + EPS) * (GDN_DH ** -0.5)
        k = k * lax.rsqrt(jnp.sum(k * k, axis=-1, keepdims=True) + EPS)
        beta = beta_all[:, LANE_GDN_B + h:LANE_GDN_B + h + 1]
        gc = gc_all[:, la:la + 1]
        decay = jnp.where(causal, jnp.exp(gc - gc_all_t[la:la + 1, :]), 0.0)
        kb = k * beta
        m = jnp.where(strict, _bdot_nt(kb, k) * decay, 0.0)
        q_l.append(q), k_l.append(k), vb_l.append(v * beta), kb_l.append(kb)
        gc_l.append(gc), gl_l.append(gl_all[:, la:la + 1]), decay_l.append(decay)
        p_l.append(m), x_l.append(eye - m)

    span = 2
    while span < c:
        for h in heads:
            p_l[h] = _bdot(p_l[h], p_l[h])
        for h in heads:
            x_l[h] = x_l[h] + _bdot(x_l[h], p_l[h])
        span *= 2

    u_l, w_l, intra_l, qd_l, kd_l, ge_l = [], [], [], [], [], []
    for h in heads:
        eg = jnp.exp(gc_l[h])
        sol = _bdot(x_l[h], jnp.concatenate([vb_l[h], kb_l[h] * eg], axis=1))
        u_l.append(sol[:, :GDN_DH]), w_l.append(sol[:, GDN_DH:])
        intra_l.append((_bdot_nt(q_l[h], k_l[h]) * decay_l[h]).astype(BF16))
        qd_l.append(q_l[h] * eg)
        kd_l.append(k_l[h] * jnp.exp(gl_l[h] - gc_l[h]))
        ge_l.append(jnp.exp(gl_l[h]))

    st_l = [state_ref[h] for h in heads]
    out_l = [[] for _ in heads]
    for ci in range(rows // c):
        rs = slice(ci * c, (ci + 1) * c)
        for h in heads:
            st = st_l[h]
            v_new = u_l[h][rs] - _bdot(w_l[h][rs], st)
            pieces = []
            if ci > 0:
                pieces.append(jnp.zeros((ci * c, GDN_DH), F32))
            pieces.append(v_new)
            if (ci + 1) * c < rows:
                pieces.append(jnp.zeros((rows - (ci + 1) * c, GDN_DH), F32))
            v_pad = jnp.concatenate(pieces, axis=0) if len(pieces) > 1 else v_new
            out_l[h].append(_bdot(qd_l[h][rs], st)
                            + jnp.dot(intra_l[h][rs], v_pad.astype(BF16), preferred_element_type=F32))
            st_l[h] = st * ge_l[h][ci * c:ci * c + 1, :] + _bdot_tn(kd_l[h][rs], v_new)

    for h in heads:
        hs = slice(h * GDN_DH, (h + 1) * GDN_DH)
        state_ref[h] = st_l[h]
        o = jnp.concatenate(out_l[h], axis=0)
        o = _rms(o, ng_ref[...]) * _silu(z_ref[0, :, hs])
        o_ref[0, :, hs] = o.astype(o_ref.dtype)


def _gdn(u3, conv_w, a_log, dt_bias, norm_g):
    b, s, _ = u3.shape
    cw = conv_w.T.astype(F32)
    vec = jnp.zeros((2, LANES), F32)
    vec = vec.at[0, LANE_GDN_A:LANE_GDN_A + GDN_HEADS].set(-jnp.exp(a_log.astype(F32)))
    vec = vec.at[1, LANE_GDN_A:LANE_GDN_A + GDN_HEADS].set(dt_bias.astype(F32))
    blk = GDN_BLOCK
    return pl.pallas_call(
        _gdn_kernel,
        grid=(b, s // blk),
        in_specs=[pl.BlockSpec((1, blk, 3 * GDN_WIDTH), lambda i, j: (i, j, COL_GDN_QKV // (3 * GDN_WIDTH))),
                  pl.BlockSpec((1, blk, GDN_WIDTH), lambda i, j: (i, j, COL_GDN_Z // GDN_WIDTH)),
                  pl.BlockSpec((1, blk, LANES), lambda i, j: (i, j, COL_SMALL // LANES)),
                  pl.BlockSpec((4, 3 * GDN_WIDTH), lambda i, j: (0, 0)),
                  pl.BlockSpec((2, LANES), lambda i, j: (0, 0)),
                  pl.BlockSpec((1, GDN_DH), lambda i, j: (0, 0))],
        out_specs=pl.BlockSpec((1, blk, GDN_WIDTH), lambda i, j: (i, j, 0)),
        out_shape=jax.ShapeDtypeStruct((b, s, GDN_WIDTH), BF16),
        scratch_shapes=[pltpu.VMEM((GDN_HEADS, GDN_DH, GDN_DH), F32),
                        pltpu.VMEM((3, 8, GDN_WIDTH), F32),
                        pltpu.VMEM((blk + 8, GDN_WIDTH), F32),
                        pltpu.VMEM((blk, 3 * GDN_WIDTH), F32)],
        compiler_params=_cp("parallel", "arbitrary"),
        name="gdn",
    )(u3, u3, u3, cw, vec, norm_g.reshape(1, GDN_DH).astype(F32))


def _ssd_kernel(z_ref, xbc_ref, sm_ref, cw_ref, cb_ref, vec_ref, dsk_ref, ng_ref, o_ref,
                state_ref, tail_ref, xbuf_ref, xc_ref, y_ref):
    rows = SEQ_BLOCK
    first = pl.program_id(1) == 0
    half = SSD_P

    @pl.when(first)
    def _():
        state_ref[...] = jnp.zeros_like(state_ref)

    slab = 512
    for s in range(SSD_CONV_DIM // slab):
        sl = slice(s * slab, (s + 1) * slab)
        xc_ref[:, sl] = _conv_silu_slab(xbc_ref[0, :, sl], first, tail_ref.at[s], xbuf_ref, cw_ref[:, sl], cb_ref[:, sl])

    dt_all = jax.nn.softplus(sm_ref[0] + vec_ref[1:2, :])
    da_all = dt_all * vec_ref[0:1, :]
    row = lax.broadcasted_iota(I32, (rows, rows), 0)
    col = lax.broadcasted_iota(I32, (rows, rows), 1)
    causal = col <= row
    tri01 = jnp.where(causal, 1.0, 0.0).astype(BF16)
    acs = _dot_exact_lhs(tri01, da_all)
    acs_t = acs.T
    lane = lax.broadcasted_iota(I32, (rows, LANES), 1)
    lo = lane < half
    lane1 = lax.broadcasted_iota(I32, (1, LANES), 1)
    lo1 = lane1 < half

    d = functools.partial(jnp.dot, preferred_element_type=F32)
    pairs_per_group = SSD_HEADS // SSD_GROUPS // 2
    n_pairs = SSD_HEADS // 2
    cb_l, bmt_l, cmb_l = [], [], []
    for g in range(SSD_GROUPS):
        bm = xc_ref[:, SSD_WIDTH + g * SSD_N:SSD_WIDTH + (g + 1) * SSD_N]
        cm = xc_ref[:, SSD_WIDTH + (SSD_GROUPS + g) * SSD_N:SSD_WIDTH + (SSD_GROUPS + g + 1) * SSD_N]
        cb_l.append(_bdot_nt(cm, bm))
        bmt_l.append(bm.T)
        cmb_l.append(cm.astype(BF16))

    att_l, dec_l, xdt_l, x_l, expa_l, ge_l = [], [], [], [], [], []
    for p in range(n_pairs):
        g = p // pairs_per_group
        la = LANE_SSD_DT + 2 * p
        lb = la + 1
        col_a, col_b = acs[:, la:la + 1], acs[:, lb:lb + 1]
        row_a, row_b = acs_t[la:la + 1, :], acs_t[lb:lb + 1, :]
        last_a, last_b = row_a[:, rows - 1:rows], row_b[:, rows - 1:rows]
        att_l.append(((cb_l[g] * jnp.where(causal, jnp.exp(col_a - row_a), 0.0)).astype(BF16),
                      (cb_l[g] * jnp.where(causal, jnp.exp(col_b - row_b), 0.0)).astype(BF16)))
        dec_l.append(((bmt_l[g] * jnp.exp(last_a - row_a)).astype(BF16),
                      (bmt_l[g] * jnp.exp(last_b - row_b)).astype(BF16)))
        x_pair = xc_ref[:, p * LANES:(p + 1) * LANES]
        xdt = x_pair * jnp.where(lo, dt_all[:, la:la + 1], dt_all[:, lb:lb + 1])
        xdt_l.append((jnp.where(lo, xdt, 0.0).astype(BF16), jnp.where(lo, 0.0, xdt).astype(BF16)))
        x_l.append(x_pair)
        expa_l.append(jnp.where(lo, jnp.exp(col_a), jnp.exp(col_b)))
        ge_l.append(jnp.where(lo1, jnp.exp(last_a), jnp.exp(last_b)))

    for p in range(n_pairs):
        g = p // pairs_per_group
        ps = slice(p * LANES, (p + 1) * LANES)
        xdt_a, xdt_b = xdt_l[p]
        y_diag = d(att_l[p][0], xdt_a) + d(att_l[p][1], xdt_b)
        st_new = d(dec_l[p][0], xdt_a) + d(dec_l[p][1], xdt_b)
        prev = state_ref[p]
        y_off = d(cmb_l[g], prev.astype(BF16)) * expa_l[p]
        state_ref[p] = prev * ge_l[p] + st_new
        y_ref[:, ps] = y_diag + y_off + x_l[p] * dsk_ref[:, ps]

    gw = SSD_WIDTH // SSD_GROUPS
    for g in range(SSD_GROUPS):
        gs = slice(g * gw, (g + 1) * gw)
        y = y_ref[:, gs] * _silu(z_ref[0, :, gs])
        o_ref[0, :, gs] = _rms(y, ng_ref[:, gs]).astype(o_ref.dtype)


def _ssd(u3, conv_w, conv_b, a_log, dt_bias, d_skip, norm_g):
    b, s, _ = u3.shape
    cw = conv_w.T.astype(F32)
    vec = jnp.zeros((2, LANES), F32)
    vec = vec.at[0, LANE_SSD_DT:LANE_SSD_DT + SSD_HEADS].set(-jnp.exp(a_log.astype(F32)))
    vec = vec.at[1, LANE_SSD_DT:LANE_SSD_DT + SSD_HEADS].set(dt_bias.astype(F32))
    dsk = jnp.repeat(d_skip.astype(F32), SSD_P).reshape(1, SSD_WIDTH)
    blk = SEQ_BLOCK
    return pl.pallas_call(
        _ssd_kernel,
        grid=(b, s // blk),
        in_specs=[pl.BlockSpec((1, blk, SSD_WIDTH), lambda i, j: (i, j, COL_SSD_Z // SSD_WIDTH)),
                  pl.BlockSpec((1, blk, SSD_CONV_DIM), lambda i, j: (i, j, COL_SSD_XBC // SSD_CONV_DIM)),
                  pl.BlockSpec((1, blk, LANES), lambda i, j: (i, j, COL_SMALL // LANES)),
                  pl.BlockSpec((4, SSD_CONV_DIM), lambda i, j: (0, 0)),
                  pl.BlockSpec((1, SSD_CONV_DIM), lambda i, j: (0, 0)),
                  pl.BlockSpec((2, LANES), lambda i, j: (0, 0)),
                  pl.BlockSpec((1, SSD_WIDTH), lambda i, j: (0, 0)),
                  pl.BlockSpec((1, SSD_WIDTH), lambda i, j: (0, 0))],
        out_specs=pl.BlockSpec((1, blk, SSD_WIDTH), lambda i, j: (i, j, 0)),
        out_shape=jax.ShapeDtypeStruct((b, s, SSD_WIDTH), BF16),
        scratch_shapes=[pltpu.VMEM((SSD_HEADS // 2, SSD_N, 2 * SSD_P), F32),
                        pltpu.VMEM((SSD_CONV_DIM // 512, 8, 512), F32),
                        pltpu.VMEM((blk + 8, 512), F32),
                        pltpu.VMEM((blk, SSD_CONV_DIM), F32),
                        pltpu.VMEM((blk, SSD_WIDTH), F32)],
        compiler_params=_cp("parallel", "arbitrary"),
        name="ssd",
    )(u3, u3, u3, cw, conv_b.reshape(1, SSD_CONV_DIM).astype(F32), vec, dsk,
      norm_g.reshape(1, SSD_WIDTH).astype(F32))


def _mla_pre_kernel(ql_ref, kpe_ref, kvl_ref, pos_ref, qag_ref, kvag_ref, wq_ref, wkv_ref, vec_ref,
                    q_ref, k_ref, v_ref):
    qn = _rms(ql_ref[0], qag_ref[...])
    qall = jnp.dot(qn.astype(BF16), wq_ref[...], preferred_element_type=F32)
    kvn = _rms(kvl_ref[0], kvag_ref[...])
    kv = jnp.dot(kvn.astype(BF16), wkv_ref[...], preferred_element_type=F32)
    kpe = kpe_ref[0]
    ang = pos_ref[0] * vec_ref[6:7, :]
    cos = jnp.cos(ang)
    sin = jnp.sin(ang) * vec_ref[7:8, :]
    gqn, gqa, gqb = vec_ref[0:1, :], vec_ref[1:2, :], vec_ref[2:3, :]
    gkn, gka, gkb = vec_ref[3:4, :], vec_ref[4:5, :], vec_ref[5:6, :]
    scale = MLA_QK ** -0.5 * LOG2_E
    k_rot = kpe * (cos * gka) + pltpu.roll(kpe, 64, 1) * (sin * gkb)
    k_pe_ss = 0.5 * jnp.sum(kpe * kpe, axis=-1, keepdims=True)
    nh = MLA_HEADS
    for h in range(nh):
        q_nope = qall[:, h * LANES:(h + 1) * LANES]
        q_pe = qall[:, (nh + h) * LANES:(nh + h + 1) * LANES]
        ss = jnp.sum(q_nope * q_nope, axis=-1, keepdims=True) + 0.5 * jnp.sum(q_pe * q_pe, axis=-1, keepdims=True)
        rstd = lax.rsqrt(ss * (1.0 / MLA_QK) + EPS)
        q_rot = q_pe * (cos * gqa) + pltpu.roll(q_pe, 64, 1) * (sin * gqb)
        q_ref[0, h, :, 0:LANES] = (q_nope * gqn * rstd * scale).astype(q_ref.dtype)
        q_ref[0, h, :, LANES:2 * LANES] = (q_rot * rstd * (0.5 * scale)).astype(q_ref.dtype)
        k_nope = kv[:, 2 * h * LANES:(2 * h + 1) * LANES]
        ssk = jnp.sum(k_nope * k_nope, axis=-1, keepdims=True) + k_pe_ss
        rstdk = lax.rsqrt(ssk * (1.0 / MLA_QK) + EPS)
        k_ref[0, h, :, 0:LANES] = (k_nope * gkn * rstdk).astype(k_ref.dtype)
        k_ref[0, h, :, LANES:2 * LANES] = (k_rot * rstdk).astype(k_ref.dtype)
        v_ref[0, h, :, 0:LANES] = kv[:, (2 * h + 1) * LANES:(2 * h + 2) * LANES].astype(v_ref.dtype)
        v_ref[0, h, :, LANES:2 * LANES] = jnp.ones((kv.shape[0], LANES), v_ref.dtype)


def _rope_pair_gains(g):
    g1, g2 = g[MLA_NOPE:MLA_NOPE + 32], g[MLA_NOPE + 32:MLA_NOPE + 64]
    return jnp.concatenate([g1, g2, g2, g1]), jnp.concatenate([g2, g1, g1, g2])


def _mla_pre(u3, pos_f, q_a_g, w_q_b, kv_a_g, w_kv_b, q_norm_g, k_norm_g, *, ts):
    b, s, _ = u3.shape
    nh = MLA_HEADS
    wq = w_q_b.reshape(MLA_Q_RANK, nh, MLA_QK)
    x1, x2 = wq[:, :, MLA_NOPE:MLA_NOPE + 32], wq[:, :, MLA_NOPE + 32:]
    wq_all = jnp.concatenate([wq[:, :, :MLA_NOPE].reshape(MLA_Q_RANK, nh * MLA_NOPE),
                              jnp.concatenate([x1, x2, x2, x1], axis=-1).reshape(MLA_Q_RANK, nh * LANES)],
                             axis=1).astype(BF16)
    half = MLA_ROPE // 2
    inv_freq = 1.0 / (ROPE_THETA ** (jnp.arange(half, dtype=F32) / half))
    gqa, gqb = _rope_pair_gains(q_norm_g.astype(F32))
    gka, gkb = _rope_pair_gains(k_norm_g.astype(F32))
    ones = jnp.ones((half,), F32)
    vec = jnp.stack([q_norm_g[:MLA_NOPE].astype(F32), gqa, gqb, k_norm_g[:MLA_NOPE].astype(F32), gka, gkb,
                     jnp.tile(inv_freq, 4), jnp.concatenate([-ones, ones, ones, -ones])])
    ts = min(ts, s)
    qk_shape = jax.ShapeDtypeStruct((b, nh, s, 2 * LANES), BF16)
    return pl.pallas_call(
        _mla_pre_kernel,
        grid=(b, s // ts),
        in_specs=[pl.BlockSpec((1, ts, MLA_Q_RANK), lambda i, j: (i, j, COL_QLAT // MLA_Q_RANK)),
                  pl.BlockSpec((1, ts, LANES), lambda i, j: (i, j, COL_KPE // LANES)),
                  pl.BlockSpec((1, ts, MLA_KV_RANK), lambda i, j: (i, j, COL_KVLAT // MLA_KV_RANK)),
                  pl.BlockSpec((1, ts, 1), lambda i, j: (i, j, 0)),
                  pl.BlockSpec((1, MLA_Q_RANK), lambda i, j: (0, 0)),
                  pl.BlockSpec((1, MLA_KV_RANK), lambda i, j: (0, 0)),
                  pl.BlockSpec((MLA_Q_RANK, 2 * nh * LANES), lambda i, j: (0, 0)),
                  pl.BlockSpec((MLA_KV_RANK, 2 * nh * LANES), lambda i, j: (0, 0)),
                  pl.BlockSpec((8, LANES), lambda i, j: (0, 0))],
        out_specs=[pl.BlockSpec((1, nh, ts, 2 * LANES), lambda i, j: (i, 0, j, 0)),
                   pl.BlockSpec((1, nh, ts, 2 * LANES), lambda i, j: (i, 0, j, 0)),
                   pl.BlockSpec((1, nh, ts, 2 * LANES), lambda i, j: (i, 0, j, 0))],
        out_shape=[qk_shape, qk_shape, qk_shape],
        compiler_params=_cp("parallel", "parallel"),
        name="mla_pre",
    )(u3, u3, u3, pos_f, q_a_g.reshape(1, -1).astype(F32), kv_a_g.reshape(1, -1).astype(F32),
      wq_all, w_kv_b.astype(BF16), vec)


def _flash_kernel(qi_ref, kj_ref, q_ref, k_ref, v_ref, o_ref, m_ref, acc_ref, *, t):
    qi = qi_ref[pl.program_id(1)]
    kj = kj_ref[pl.program_id(1)]
    nh = q_ref.shape[1]

    @pl.when(kj == 0)
    def _():
        m_ref[...] = jnp.full_like(m_ref, -jnp.inf)
        acc_ref[...] = jnp.zeros_like(acc_ref)

    def step(diagonal):
        s_l = [lax.dot_general(q_ref[0, h], k_ref[0, h], (((1,), (1,)), ((), ())), preferred_element_type=F32)
               for h in range(nh)]
        if diagonal:
            keep = lax.broadcasted_iota(I32, (t, t), 0) >= lax.broadcasted_iota(I32, (t, t), 1)
            s_l = [jnp.where(keep, s, -jnp.inf) for s in s_l]
        p_l, alpha_l = [], []
        for h in range(nh):
            m_old = m_ref[h]
            m_new = jnp.maximum(m_old, jnp.max(s_l[h], axis=-1, keepdims=True))
            p_l.append(jnp.exp2(s_l[h] - jnp.tile(m_new, (1, t // LANES))).astype(BF16))
            alpha_l.append(jnp.exp2(m_old - m_new))
            m_ref[h] = m_new
        for h in range(nh):
            pv = jnp.dot(p_l[h], v_ref[0, h], preferred_element_type=F32)
            acc_ref[h] = jnp.tile(alpha_l[h], (1, 2)) * acc_ref[h] + pv

    @pl.when(kj < qi)
    def _():
        step(False)

    @pl.when(kj == qi)
    def _():
        step(True)
        for h in range(nh):
            acc = acc_ref[h]
            o_ref[0, :, h * LANES:(h + 1) * LANES] = (acc[:, :LANES] / acc[:, LANES:]).astype(o_ref.dtype)


def _flash(q, k, v, *, t):
    b, nh, s, dq = q.shape
    t = min(t, s)
    n = s // t
    pairs = [(qi, kj) for qi in range(n) for kj in range(qi + 1)]
    qi_tab = jnp.asarray([pr[0] for pr in pairs], I32)
    kj_tab = jnp.asarray([pr[1] for pr in pairs], I32)
    kv_spec = pl.BlockSpec((1, nh, t, dq), lambda i, pr, qt, kt: (i, 0, kt[pr], 0))
    grid_spec = pltpu.PrefetchScalarGridSpec(
        num_scalar_prefetch=2,
        grid=(b, len(pairs)),
        in_specs=[pl.BlockSpec((1, nh, t, dq), lambda i, pr, qt, kt: (i, 0, qt[pr], 0)), kv_spec, kv_spec],
        out_specs=pl.BlockSpec((1, t, nh * LANES), lambda i, pr, qt, kt: (i, qt[pr], 0)),
        scratch_shapes=[pltpu.VMEM((nh, t, LANES), F32), pltpu.VMEM((nh, t, 2 * LANES), F32)],
    )
    return pl.pallas_call(
        functools.partial(_flash_kernel, t=t),
        grid_spec=grid_spec,
        out_shape=jax.ShapeDtypeStruct((b, s, nh * LANES), BF16),
        compiler_params=_cp("parallel", "arbitrary"),
        name="mla_flash",
    )(qi_tab, kj_tab, q, k, v)


def _router_kernel(h_ref, g_ref, wr_ref, idx_ref, wt_ref):
    xn = _rms(h_ref[...], g_ref[...])
    w = wr_ref[...]
    xh, wh = xn.astype(BF16), w.astype(BF16)
    xl, wl = (xn - xh.astype(F32)).astype(BF16), (w - wh.astype(F32)).astype(BF16)
    d = functools.partial(jnp.dot, preferred_element_type=F32)
    logits = d(xh, wh) + d(xh, wl) + d(xl, wh)
    lane = lax.broadcasted_iota(I32, logits.shape, 1)
    logits = jnp.where(lane < N_EXPERTS, logits, -jnp.inf)
    m1 = jnp.max(logits, axis=-1, keepdims=True)
    i1 = jnp.min(jnp.where(logits == m1, lane, LANES), axis=-1, keepdims=True)
    rest = jnp.where(lane == i1, -jnp.inf, logits)
    m2 = jnp.max(rest, axis=-1, keepdims=True)
    i2 = jnp.min(jnp.where(rest == m2, lane, LANES), axis=-1, keepdims=True)
    e2 = jnp.exp(m2 - m1)
    w1 = 1.0 / (1.0 + e2)
    w2 = e2 / (1.0 + e2)
    idx_ref[...] = jnp.where(lane == 0, i1, jnp.where(lane == 1, i2, 0))
    wt_ref[...] = jnp.where(lane == 0, w1, jnp.where(lane == 1, w2, 0.0))


def _router(h, gain, w_router, *, tm):
    m, d = h.shape
    tm = min(tm, m)
    wr = jnp.zeros((d, LANES), F32).at[:, :N_EXPERTS].set(w_router.astype(F32))
    return pl.pallas_call(
        _router_kernel,
        grid=(m // tm,),
        in_specs=[pl.BlockSpec((tm, d), lambda i: (i, 0)),
                  pl.BlockSpec((1, d), lambda i: (0, 0)),
                  pl.BlockSpec((d, LANES), lambda i: (0, 0))],
        out_specs=[pl.BlockSpec((tm, LANES), lambda i: (i, 0)),
                   pl.BlockSpec((tm, LANES), lambda i: (i, 0))],
        out_shape=[jax.ShapeDtypeStruct((m, LANES), I32),
                   jax.ShapeDtypeStruct((m, LANES), F32)],
        compiler_params=_cp("parallel"),
        name="moe_router",
    )(h, gain.reshape(1, d).astype(F32), wr)


def _row_copy(src_ref, t, dst_ref, r, sem):
    return pltpu.make_async_copy(src_ref.at[pl.ds(t, 1), :], dst_ref.at[pl.ds(r, 1), :], sem)


def _gather_kernel(idx_ref, nu_ref, src_ref, g_ref, o_ref, buf_ref, sem):
    rows = buf_ref.shape[1]
    i = pl.program_id(0)
    n = nu_ref[0]

    def start_step(step, slot):
        def issue(r2, c):
            for k in range(2):
                r = 2 * r2 + k
                _row_copy(src_ref, idx_ref[step * rows + r], buf_ref.at[slot], r, sem.at[slot]).start(priority=k)
            return c
        lax.fori_loop(0, rows // 2, issue, 0, unroll=4)

    @pl.when(i == 0)
    def _():
        start_step(0, 0)

    @pl.when(i + 1 < n)
    def _():
        start_step(i + 1, (i + 1) % 2)

    slot = i % 2

    @pl.when(i < n)
    def _():
        def wait(r, c):
            _row_copy(src_ref, 0, buf_ref.at[slot], r, sem.at[slot]).wait()
            return c

        lax.fori_loop(0, rows, wait, 0, unroll=8)
        o_ref[...] = _rms(buf_ref[slot], g_ref[...]).astype(o_ref.dtype)

    @pl.when(i >= n)
    def _():
        o_ref[...] = jnp.zeros_like(o_ref)


def _gather_norm_rows(src_tok, n_used, h, gain):
    p = src_tok.shape[0]
    d = h.shape[1]
    rows = MOE_TM
    grid_spec = pltpu.PrefetchScalarGridSpec(
        num_scalar_prefetch=2,
        grid=(p // rows,),
        in_specs=[pl.BlockSpec(memory_space=pl.ANY), pl.BlockSpec((1, d), lambda i, idx, nu: (0, 0))],
        out_specs=pl.BlockSpec((rows, d), lambda i, idx, nu: (i, 0)),
        scratch_shapes=[pltpu.VMEM((2, rows, d), F32), pltpu.SemaphoreType.DMA((2,))],
    )
    return pl.pallas_call(
        _gather_kernel,
        grid_spec=grid_spec,
        out_shape=jax.ShapeDtypeStruct((p, d), BF16),
        compiler_params=_cp("arbitrary"),
        name="moe_gather",
    )(src_tok, n_used, h, gain.reshape(1, d).astype(F32))


def _gup_kernel(te_ref, nu_ref, x_ref, wg_ref, wu_ref, wd_ref, o_ref, wdb_ref, wgb_ref, wub_ref):
    i = pl.program_id(1)
    used = i < nu_ref[0]

    wdb_ref[...] = wd_ref[...].astype(BF16)

    changed = (i == 0) | (te_ref[i] != te_ref[jnp.maximum(i - 1, 0)])

    def compute(convert):
        x = x_ref[...]
        for c in range(0, o_ref.shape[1], MXU_COLS):
            cs = slice(c, c + MXU_COLS)
            if convert:
                wgb_ref[:, cs] = wg_ref[0, :, cs].astype(BF16)
                wub_ref[:, cs] = wu_ref[0, :, cs].astype(BF16)
            gate = jnp.dot(x, wgb_ref[:, cs], preferred_element_type=F32)
            up = jnp.dot(x, wub_ref[:, cs], preferred_element_type=F32)
            o_ref[:, cs] = (_silu(gate) * up).astype(o_ref.dtype)

    @pl.when(changed)
    def _():
        compute(True)

    @pl.when(used & jnp.logical_not(changed))
    def _():
        compute(False)

    @pl.when(jnp.logical_not(used))
    def _():
        o_ref[...] = jnp.zeros_like(o_ref)


def _grouped_up(tile_expert, n_used, xs, wg, wu, wd, *, tm, tf):
    p, d = xs.shape
    f = wg.shape[2]
    n_i = p // tm
    n_steps = (f // tf) * n_i
    wd2 = wd.reshape(-1, wd.shape[-1])
    slab = _slab_rows(wd2.shape[0], n_steps)
    n_slabs = wd2.shape[0] // slab

    def slab_map(j, i, te, nu):
        return (jnp.minimum(j * n_i + i, n_slabs - 1), 0)

    grid_spec = pltpu.PrefetchScalarGridSpec(
        num_scalar_prefetch=2,
        grid=(f // tf, n_i),
        in_specs=[pl.BlockSpec((tm, d), lambda j, i, te, nu: (jnp.minimum(i, nu[0] - 1), 0)),
                  pl.BlockSpec((1, d, tf), lambda j, i, te, nu: (te[i], 0, j)),
                  pl.BlockSpec((1, d, tf), lambda j, i, te, nu: (te[i], 0, j)),
                  pl.BlockSpec((slab, wd2.shape[1]), slab_map)],
        out_specs=[pl.BlockSpec((tm, tf), lambda j, i, te, nu: (i, j)),
                   pl.BlockSpec((slab, wd2.shape[1]), slab_map)],
        scratch_shapes=[pltpu.VMEM((d, tf), BF16), pltpu.VMEM((d, tf), BF16)],
    )
    hff, wdb = pl.pallas_call(
        _gup_kernel,
        grid_spec=grid_spec,
        out_shape=[jax.ShapeDtypeStruct((p, f), BF16), jax.ShapeDtypeStruct(wd2.shape, BF16)],
        compiler_params=_cp("arbitrary", "arbitrary"),
        name="moe_up",
    )(tile_expert, n_used, xs, wg, wu, wd2)
    return hff, wdb.reshape(wd.shape)


def _gdown_kernel(te_ref, nu_ref, x_ref, wd_ref, o_ref):
    used = pl.program_id(1) < nu_ref[0]

    @pl.when(used)
    def _():
        o_ref[...] = jnp.dot(x_ref[...], wd_ref[0], preferred_element_type=F32)

    @pl.when(jnp.logical_not(used))
    def _():
        o_ref[...] = jnp.zeros_like(o_ref)


def _grouped_down(tile_expert, n_used, hff, wd, *, tm, tn):
    p, f = hff.shape
    d = wd.shape[2]
    grid_spec = pltpu.PrefetchScalarGridSpec(
        num_scalar_prefetch=2,
        grid=(d // tn, p // tm),
        in_specs=[pl.BlockSpec((tm, f), lambda j, i, te, nu: (jnp.minimum(i, nu[0] - 1), 0)),
                  pl.BlockSpec((1, f, tn), lambda j, i, te, nu: (te[i], 0, j))],
        out_specs=pl.BlockSpec((tm, tn), lambda j, i, te, nu: (i, j)),
    )
    return pl.pallas_call(
        _gdown_kernel,
        grid_spec=grid_spec,
        out_shape=jax.ShapeDtypeStruct((p, d), F32),
        compiler_params=_cp("arbitrary", "arbitrary"),
        name="moe_down",
    )(tile_expert, n_used, hff, wd)


def _combine_kernel(pos_ref, h_ref, wt_ref, y_ref, o_ref, buf_ref, sem):
    rows = h_ref.shape[0]
    i = pl.program_id(0)
    n = pl.num_programs(0)

    def start_step(step, slot):
        def issue(r, c):
            base = 2 * (step * rows + r)
            _row_copy(y_ref, pos_ref[base], buf_ref.at[slot, 0], r, sem.at[slot]).start(priority=0)
            _row_copy(y_ref, pos_ref[base + 1], buf_ref.at[slot, 1], r, sem.at[slot]).start(priority=1)
            return c
        lax.fori_loop(0, rows, issue, 0, unroll=4)

    @pl.when(i == 0)
    def _():
        start_step(0, 0)

    @pl.when(i + 1 < n)
    def _():
        start_step(i + 1, (i + 1) % 2)

    slot = i % 2

    def wait(r, c):
        _row_copy(y_ref, 0, buf_ref.at[slot, 0], r, sem.at[slot]).wait()
        _row_copy(y_ref, 0, buf_ref.at[slot, 1], r, sem.at[slot]).wait()
        return c

    lax.fori_loop(0, rows, wait, 0, unroll=4)
    wt = wt_ref[...]
    o_ref[...] = h_ref[...] + wt[:, 0:1] * buf_ref[slot, 0] + wt[:, 1:2] * buf_ref[slot, 1]


def _combine(h, y, pos, wts):
    m, d = h.shape
    rows = min(COMBINE_ROWS, m)
    grid_spec = pltpu.PrefetchScalarGridSpec(
        num_scalar_prefetch=1,
        grid=(m // rows,),
        in_specs=[pl.BlockSpec((rows, d), lambda i, pos: (i, 0)),
                  pl.BlockSpec((rows, LANES), lambda i, pos: (i, 0)),
                  pl.BlockSpec(memory_space=pl.ANY)],
        out_specs=pl.BlockSpec((rows, d), lambda i, pos: (i, 0)),
        scratch_shapes=[pltpu.VMEM((2, 2, rows, d), F32), pltpu.SemaphoreType.DMA((2,))],
    )
    return pl.pallas_call(
        _combine_kernel,
        grid_spec=grid_spec,
        out_shape=jax.ShapeDtypeStruct((m, d), F32),
        compiler_params=_cp("arbitrary"),
        name="moe_combine",
    )(pos, h, wts, y)


def _moe(h, gain, w_router, wg, wu, wd):
    t, d = h.shape
    tm = MOE_TM
    n2 = 2 * t
    idx, wts = _router(h, gain, w_router, tm=512)
    flat_e = idx[:, :2].reshape(-1)
    onehot = (flat_e[:, None] == jnp.arange(N_EXPERTS, dtype=I32)[None, :]).astype(I32)
    csum = jnp.cumsum(onehot, axis=0)
    rank = jnp.sum((csum - onehot) * onehot, axis=1)
    counts = csum[-1]
    padded = ((counts + tm - 1) // tm) * tm
    ends = jnp.cumsum(padded)
    starts = ends - padded
    pos = (starts[flat_e] + rank).astype(I32)
    p_rows = n2 + N_EXPERTS * tm
    n_tiles = p_rows // tm
    n_used = (ends[-1] // tm).astype(I32).reshape(1)
    tile_start = jnp.arange(n_tiles, dtype=I32) * tm
    tile_expert = jnp.sum((tile_start[:, None] >= ends[None, :]).astype(I32), axis=1)
    last_expert = jnp.sum((ends[-1] - 1 >= ends).astype(I32))
    tile_expert = jnp.minimum(tile_expert, last_expert).astype(I32)
    order = jnp.sort(flat_e * n2 + jnp.arange(n2, dtype=I32)) % n2
    row = jnp.arange(p_rows, dtype=I32)
    row_e = jnp.repeat(tile_expert, tm)
    local = row - starts[row_e]
    first = (jnp.cumsum(counts) - counts)[row_e]
    src_tok = jnp.where(local < counts[row_e], order[jnp.clip(first + local, 0, n2 - 1)] // 2, row % t).astype(I32)

    xs = _gather_norm_rows(src_tok, n_used, h, gain)
    hff, wd_bf16 = _grouped_up(tile_expert, n_used, xs, wg, wu, wd, tm=tm, tf=1024)
    y = _grouped_down(tile_expert, n_used, hff, wd_bf16, tm=tm, tn=1024)
    return _combine(h, y, pos, wts)


def _w_in_moves():
    gdn0, ssd0 = 0, 4 * GDN_WIDTH + 2 * GDN_HEADS
    mla0 = ssd0 + SSD_WIDTH + SSD_CONV_DIM + SSD_HEADS
    kpe0 = mla0 + MLA_Q_RANK + MLA_KV_RANK
    gb0 = gdn0 + 4 * GDN_WIDTH
    dt0 = ssd0 + SSD_WIDTH + SSD_CONV_DIM
    half = MLA_ROPE // 2
    moves = [(COL_GDN_QKV, gdn0, 4 * GDN_WIDTH),
             (COL_SSD_Z, ssd0, SSD_WIDTH + SSD_CONV_DIM),
             (COL_QLAT, mla0, MLA_Q_RANK),
             (COL_KPE, kpe0, MLA_ROPE), (COL_KPE + MLA_ROPE, kpe0 + half, half),
             (COL_KPE + MLA_ROPE + half, kpe0, half),
             (COL_KVLAT, mla0 + MLA_Q_RANK, MLA_KV_RANK),
             (COL_SMALL + LANE_GDN_B, gb0, 2 * GDN_HEADS), (COL_SMALL + LANE_SSD_DT, dt0, SSD_HEADS)]
    small_end = LANE_SSD_DT + SSD_HEADS
    zeros = [(COL_SMALL, LANE_GDN_B), (COL_SMALL + small_end, LANES - small_end), (COL_SMALL + LANES, LANES)]
    assert LANE_GDN_A == LANE_GDN_B + GDN_HEADS and LANE_SSD_DT == LANE_GDN_A + GDN_HEADS
    assert sum(m[2] for m in moves) + sum(z[1] for z in zeros) == U_COLS
    return moves, zeros


def _w_in_kernel(w_ref, o_ref):
    moves, zeros = _w_in_moves()
    cols = o_ref.shape[1]
    small_lo, small_hi = COL_SMALL, COL_SMALL + LANES
    pieces = {}
    for dst, src, width in moves:
        if small_lo <= dst < small_hi:
            pieces[dst] = w_ref[0, src:src + width, :]
        else:
            o_ref[dst:dst + width, :] = w_ref[0, src:src + width, :].astype(o_ref.dtype)
    for dst, width in zeros:
        if small_lo <= dst < small_hi:
            pieces[dst] = jnp.zeros((width, cols), F32)
        else:
            o_ref[dst:dst + width, :] = jnp.zeros((width, cols), o_ref.dtype)
    small = jnp.concatenate([pieces[k] for k in sorted(pieces)], axis=0)
    o_ref[small_lo:small_hi, :] = small.astype(o_ref.dtype)


def _rearranged_w_in_t(w_in_all, layer):
    _, d, n = w_in_all.shape
    w_t = jnp.transpose(w_in_all, (0, 2, 1))
    tc = min(512, d)
    return pl.pallas_call(
        _w_in_kernel,
        grid=(d // tc,),
        in_specs=[pl.BlockSpec((1, n, tc), lambda i: (layer, 0, i))],
        out_specs=pl.BlockSpec((U_COLS, tc), lambda i: (0, i)),
        out_shape=jax.ShapeDtypeStruct((U_COLS, d), BF16),
        compiler_params=_cp("parallel"),
        name="w_in_prep",
    )(w_t)


def kernel(x, p, positions, norm_mix_g, w_in, w_out, gdn_conv_w, gdn_a_log, gdn_dt_bias, gdn_norm_g, ssd_conv_w, ssd_conv_b, ssd_a_log, ssd_dt_bias, ssd_d, ssd_norm_g, mla_q_a_g, mla_w_q_b, mla_kv_a_g, mla_w_kv_b, mla_q_norm_g, mla_k_norm_g, norm_ffn_g, ffn_w_gate, ffn_w_up, ffn_w_down, router_w, moe_w_gate, moe_w_up, moe_w_down, ple_w_proj, ple_w_gate, ple_norm_g):
    b, s, d = x.shape
    t = b * s
    depth = w_in.shape[0]
    h = x.reshape(t, d).astype(F32)
    pos_f = positions.astype(F32).reshape(b, s, 1)
    p_all = p.reshape(depth, t, p.shape[-1])
    d_ff = ffn_w_gate.shape[-1]
    w_out2, ple_gate2 = w_out.reshape(depth * d, d), ple_w_gate.reshape(depth * d, d)
    xn = None
    for i in range(depth):
        side = [(w_out2, i * d, d), (ple_gate2, i * d, d)]
        if i % 2 == 0:
            side += [(ffn_w_gate.reshape(-1, d_ff), (i // 2) * d, d), (ffn_w_up.reshape(-1, d_ff), (i // 2) * d, d),
                     (ffn_w_down.reshape(-1, d), (i // 2) * d_ff, d_ff)]
        if xn is None:
            proj_in = dict(x=h, gain=norm_mix_g[i])
        else:
            proj_in = dict(x=xn)
        u, w_out_b, ple_gate_b, *ffn_b = _mm(w=_rearranged_w_in_t(w_in, i), tm=1024, tn=1408, name="in_proj",
                                             side=side, w_transposed=True, **proj_in)
        u3 = u.reshape(b, s, U_COLS)
        o_gdn = _gdn(u3, gdn_conv_w[i], gdn_a_log[i], gdn_dt_bias[i], gdn_norm_g[i])
        o_ssd = _ssd(u3, ssd_conv_w[i], ssd_conv_b[i], ssd_a_log[i], ssd_dt_bias[i], ssd_d[i], ssd_norm_g[i])
        q, k, v = _mla_pre(u3, pos_f, mla_q_a_g[i], mla_w_q_b[i], mla_kv_a_g[i], mla_w_kv_b[i],
                           mla_q_norm_g[i], mla_k_norm_g[i], ts=512)
        o_mla = _flash(q, k, v, t=512)
        mix_parts = [o.reshape(t, o.shape[-1]) for o in (o_gdn, o_ssd, o_mla)]
        j = i // 2
        if i % 2 == 0:
            h, hn = _mm_parts(mix_parts, w_out_b, h, tm=512, tn=d, name="out_proj", next_gain=norm_ffn_g[i])
            ff = _swiglu_up(hn, ffn_b[0], ffn_b[1], tm=1024, tn=512)
            h = _mm(ff, ffn_b[2], res=h, tm=1024, tn=512, name="ffn_down")
        else:
            h, _ = _mm_parts(mix_parts, w_out_b, h, tm=1024, tn=1024, name="out_proj")
            h = _moe(h, norm_ffn_g[i], router_w[j], moe_w_gate[j], moe_w_up[j], moe_w_down[j])
        h, xn = _ple(h, p_all, i, ple_w_proj[i].astype(BF16), ple_norm_g[i], ple_gate_b, tm=512,
                     next_gain=norm_mix_g[i + 1] if i + 1 < depth else None)
    return h.reshape(b, s, d).astype(x.dtype)
```

```python
import functools

import jax
import jax.numpy as jnp
from jax import lax
from jax.experimental import pallas as pl
from jax.experimental.pallas import tpu as pltpu

F32 = jnp.float32
BF16 = jnp.bfloat16
U32 = jnp.uint32
I32 = jnp.int32
EPS = 1e-6

D_MODEL = 2048
GDN_HEADS, GDN_DH, GDN_CHUNK = 4, 128, 64
GDN_WIDTH = GDN_HEADS * GDN_DH
SSD_HEADS, SSD_P, SSD_GROUPS, SSD_N, SSD_CHUNK = 16, 64, 2, 128, 256
SSD_WIDTH = SSD_HEADS * SSD_P
SSD_CONV_DIM = SSD_WIDTH + 2 * SSD_GROUPS * SSD_N
MLA_HEADS, MLA_Q_RANK, MLA_KV_RANK = 4, 384, 256
MLA_NOPE, MLA_ROPE, MLA_V = 128, 64, 128
MLA_QK = MLA_NOPE + MLA_ROPE
ROPE_THETA = 10000.0
N_EXPERTS = 8
LANES = 128
MXU_COLS = 256
LOG2_E = 1.4426950408889634

U_COLS = 5632
COL_GDN_QKV, COL_GDN_Z, COL_SSD_Z, COL_SSD_XBC = 0, 1536, 2048, 3072
COL_QLAT, COL_KPE, COL_KVLAT, COL_SMALL = 4608, 4992, 5120, 5376
LANE_GDN_B, LANE_GDN_A, LANE_SSD_DT = 64, 68, 72

SEQ_BLOCK = 256
GDN_BLOCK = 256
VMEM_LIMIT_BYTES = 56 * 1024 * 1024
MOE_TM = 512
COMBINE_ROWS = 256


def _cp(*sem):
    return pltpu.CompilerParams(dimension_semantics=sem, vmem_limit_bytes=VMEM_LIMIT_BYTES)


def _rms(x, gain):
    return x * lax.rsqrt(jnp.mean(x * x, axis=-1, keepdims=True) + EPS) * gain


def _silu(x):
    return x * jax.nn.sigmoid(x)


def _bdot(a, b):
    return jnp.dot(a.astype(BF16), b.astype(BF16), preferred_element_type=F32)


def _bdot_nt(a, b):
    return lax.dot_general(a.astype(BF16), b.astype(BF16), (((1,), (1,)), ((), ())), preferred_element_type=F32)


def _bdot_tn(a, b):
    return lax.dot_general(a.astype(BF16), b.astype(BF16), (((0,), (0,)), ((), ())), preferred_element_type=F32)


def _split3(x):
    x1 = x.astype(BF16)
    r1 = x - x1.astype(F32)
    x2 = r1.astype(BF16)
    x3 = (r1 - x2.astype(F32)).astype(BF16)
    return x1, x2, x3


def _dot_exact_lhs(m01, x):
    x1, x2, x3 = _split3(x)
    d = functools.partial(jnp.dot, preferred_element_type=F32)
    return d(m01, x1) + d(m01, x2) + d(m01, x3)


def _slab_rows(rows, n_steps):
    bf16_sublanes = 16
    return next(r for r in range(bf16_sublanes, rows + 1, bf16_sublanes) if rows % r == 0 and rows // r <= n_steps)


def _mm_kernel(*refs, norm, cast, has_res, n_side, w_transposed):
    it = iter(refs)
    x_ref = next(it)
    g_ref = next(it) if norm else None
    w_ref = next(it)
    res_ref = next(it) if has_res else None
    side_in = [next(it) for _ in range(n_side)]
    o_ref = next(it)
    for src_ref in side_in:
        dst_ref = next(it)
        dst_ref[...] = src_ref[...].astype(BF16)
    if cast:
        xs_ref = next(it)

        @pl.when(pl.program_id(1) == 0)
        def _():
            x = x_ref[...].astype(F32)
            if norm:
                x = _rms(x, g_ref[...])
            xs_ref[...] = x.astype(BF16)

        a = xs_ref[...]
    else:
        a = x_ref[...]
    if w_transposed:
        acc = lax.dot_general(a, w_ref[...], (((1,), (1,)), ((), ())), preferred_element_type=F32)
    else:
        acc = jnp.dot(a, w_ref[...], preferred_element_type=F32)
    if has_res:
        acc = acc + res_ref[...]
    o_ref[...] = acc.astype(o_ref.dtype)


def _mm(x, w, *, gain=None, res=None, out_dtype=F32, tm, tn, name, side=(), w_transposed=False):
    m, k = x.shape
    n = w.shape[0] if w_transposed else w.shape[1]
    tm, tn = min(tm, m), min(tn, n)
    grid = (m // tm, n // tn)
    n_steps = grid[0] * grid[1]
    norm = gain is not None
    cast = norm or x.dtype != BF16
    in_specs = [pl.BlockSpec((tm, k), lambda i, j: (i, 0))]
    args = [x]
    if norm:
        in_specs.append(pl.BlockSpec((1, k), lambda i, j: (0, 0)))
        args.append(gain.reshape(1, k).astype(F32))
    in_specs.append(pl.BlockSpec((tn, k), lambda i, j: (j, 0)) if w_transposed
                    else pl.BlockSpec((k, tn), lambda i, j: (0, j)))
    args.append(w)
    if res is not None:
        in_specs.append(pl.BlockSpec((tm, tn), lambda i, j: (i, j)))
        args.append(res)
    out_specs = [pl.BlockSpec((tm, tn), lambda i, j: (i, j))]
    out_shape = [jax.ShapeDtypeStruct((m, n), out_dtype)]
    for arr, first_row, n_rows in side:
        slab = _slab_rows(n_rows, n_steps)
        n_slabs = n_rows // slab
        assert first_row % slab == 0
        first_slab = first_row // slab
        in_specs.append(pl.BlockSpec(
            (slab, arr.shape[1]),
            lambda i, j, n_slabs=n_slabs, first_slab=first_slab: (first_slab + jnp.minimum(i * grid[1] + j, n_slabs - 1), 0)))
        out_specs.append(pl.BlockSpec(
            (slab, arr.shape[1]), lambda i, j, n_slabs=n_slabs: (jnp.minimum(i * grid[1] + j, n_slabs - 1), 0)))
        out_shape.append(jax.ShapeDtypeStruct((n_rows, arr.shape[1]), BF16))
        args.append(arr)
    outs = pl.pallas_call(
        functools.partial(_mm_kernel, norm=norm, cast=cast, has_res=res is not None, n_side=len(side),
                          w_transposed=w_transposed),
        grid=grid,
        in_specs=in_specs,
        out_specs=out_specs,
        out_shape=out_shape,
        scratch_shapes=[pltpu.VMEM((tm, k), BF16)] if cast else [],
        compiler_params=_cp("arbitrary", "arbitrary"),
        name=name,
    )(*args)
    return outs if side else outs[0]


def _mm_parts_kernel(*refs, widths, with_next):
    n = len(widths)
    x_refs, w_ref, res_ref = refs[:n], refs[n], refs[n + 1]
    acc = res_ref[...]
    off = 0
    for x_ref, width in zip(x_refs, widths):
        acc = acc + jnp.dot(x_ref[...], w_ref[off:off + width, :], preferred_element_type=F32)
        off += width
    if with_next:
        gn_ref, o_ref, xn_ref = refs[n + 2:]
        xn_ref[...] = _rms(acc, gn_ref[...]).astype(xn_ref.dtype)
    else:
        o_ref = refs[n + 2]
    o_ref[...] = acc


def _mm_parts(xs, w, res, *, tm, tn, name, next_gain=None):
    m = xs[0].shape[0]
    widths = tuple(x.shape[1] for x in xs)
    k, n = w.shape
    assert sum(widths) == k
    tm, tn = min(tm, m), min(tn, n)
    with_next = next_gain is not None
    assert not with_next or tn == n
    tile = pl.BlockSpec((tm, tn), lambda i, j: (i, j))
    in_specs = [pl.BlockSpec((tm, width), lambda i, j: (i, 0)) for width in widths]
    in_specs += [pl.BlockSpec((k, tn), lambda i, j: (0, j)), tile]
    args = [*xs, w, res]
    out_specs, out_shape = [tile], [jax.ShapeDtypeStruct((m, n), F32)]
    if with_next:
        in_specs.append(pl.BlockSpec((1, n), lambda i, j: (0, 0)))
        args.append(next_gain.reshape(1, n).astype(F32))
        out_specs.append(tile)
        out_shape.append(jax.ShapeDtypeStruct((m, n), BF16))
    outs = pl.pallas_call(
        functools.partial(_mm_parts_kernel, widths=widths, with_next=with_next),
        grid=(m // tm, n // tn),
        in_specs=in_specs,
        out_specs=out_specs,
        out_shape=out_shape,
        compiler_params=_cp("parallel", "arbitrary"),
        name=name,
    )(*args)
    return outs if with_next else (outs[0], None)


def _swiglu_up_kernel(xn_ref, wg_ref, wu_ref, o_ref):
    a = xn_ref[...]
    for c in range(0, o_ref.shape[1], MXU_COLS):
        cs = slice(c, c + MXU_COLS)
        gate = jnp.dot(a, wg_ref[:, cs], preferred_element_type=F32)
        up = jnp.dot(a, wu_ref[:, cs], preferred_element_type=F32)
        o_ref[:, cs] = (_silu(gate) * up).astype(o_ref.dtype)


def _swiglu_up(xn, wg, wu, *, tm, tn):
    m, k = xn.shape
    n = wg.shape[1]
    tm, tn = min(tm, m), min(tn, n)
    return pl.pallas_call(
        _swiglu_up_kernel,
        grid=(m // tm, n // tn),
        in_specs=[pl.BlockSpec((tm, k), lambda i, j: (i, 0)),
                  pl.BlockSpec((k, tn), lambda i, j: (0, j)),
                  pl.BlockSpec((k, tn), lambda i, j: (0, j))],
        out_specs=pl.BlockSpec((tm, tn), lambda i, j: (i, j)),
        out_shape=jax.ShapeDtypeStruct((m, n), BF16),
        compiler_params=_cp("parallel", "arbitrary"),
        name="swiglu_up",
    )(xn, wg, wu)


def _ple_kernel(*refs, with_next):
    if with_next:
        h_ref, p_ref, wp_ref, g_ref, wg_ref, gn_ref, o_ref, xn_ref = refs
    else:
        h_ref, p_ref, wp_ref, g_ref, wg_ref, o_ref = refs
    h = h_ref[...]
    e = jnp.dot(p_ref[0].astype(BF16), wp_ref[...], preferred_element_type=F32)
    e = _rms(e, g_ref[...])
    gate = jax.nn.sigmoid(jnp.dot(h.astype(BF16), wg_ref[...], preferred_element_type=F32))
    out = h + gate * e
    o_ref[...] = out
    if with_next:
        xn_ref[...] = _rms(out, gn_ref[...]).astype(xn_ref.dtype)


def _ple(h, p_all, layer, wp, gain, wg, *, tm, next_gain=None):
    m, d = h.shape
    dp = p_all.shape[2]
    tm = min(tm, m)
    with_next = next_gain is not None
    row_spec = pl.BlockSpec((tm, d), lambda i: (i, 0))
    vec_spec = pl.BlockSpec((1, d), lambda i: (0, 0))
    in_specs = [row_spec, pl.BlockSpec((1, tm, dp), lambda i: (layer, i, 0)), pl.BlockSpec((dp, d), lambda i: (0, 0)),
                vec_spec, pl.BlockSpec((d, d), lambda i: (0, 0))]
    args = [h, p_all, wp, gain.reshape(1, d).astype(F32), wg]
    out_specs, out_shape = [row_spec], [jax.ShapeDtypeStruct((m, d), F32)]
    if with_next:
        in_specs.append(vec_spec)
        args.append(next_gain.reshape(1, d).astype(F32))
        out_specs.append(row_spec)
        out_shape.append(jax.ShapeDtypeStruct((m, d), BF16))
    outs = pl.pallas_call(
        functools.partial(_ple_kernel, with_next=with_next),
        grid=(m // tm,),
        in_specs=in_specs,
        out_specs=out_specs,
        out_shape=out_shape,
        compiler_params=_cp("parallel"),
        name="ple",
    )(*args)
    return outs if with_next else (outs[0], None)


def _conv_silu_slab(x, first, tail_ref, xbuf_ref, cw, bias):
    rows = x.shape[0]

    @pl.when(first)
    def _():
        tail_ref[...] = jnp.zeros_like(tail_ref)

    xbuf_ref[0:8, :] = tail_ref[...]
    xbuf_ref[8:8 + rows, :] = x
    tail_ref[...] = x[rows - 8:rows, :]
    y = (xbuf_ref[5:5 + rows, :] * cw[0:1, :] + xbuf_ref[6:6 + rows, :] * cw[1:2, :]
         + xbuf_ref[7:7 + rows, :] * cw[2:3, :] + x * cw[3:4, :])
    if bias is not None:
        y = y + bias
    return _silu(y)


def _gdn_kernel(qkv_ref, z_ref, sm_ref, cw_ref, vec_ref, ng_ref, o_ref, state_ref, tail_ref, xbuf_ref, xc_ref):
    rows = GDN_BLOCK
    c = GDN_CHUNK
    first = pl.program_id(1) == 0

    @pl.when(first)
    def _():
        state_ref[...] = jnp.zeros_like(state_ref)

    nb = qkv_ref.shape[0]
    for bi in range(nb):
        for s in range(3):
            sl = slice(s * GDN_WIDTH, (s + 1) * GDN_WIDTH)
            xc_ref[bi, :, sl] = _conv_silu_slab(qkv_ref[bi, :, sl], first, tail_ref.at[bi * 3 + s], xbuf_ref,
                                                cw_ref[:, sl], None)

    row = lax.broadcasted_iota(I32, (rows, rows), 0)
    col = lax.broadcasted_iota(I32, (rows, rows), 1)
    same = (row // c) == (col // c)
    causal = same & (col <= row)
    strict = same & (col < row)
    tri01 = jnp.where(causal, 1.0, 0.0).astype(BF16)
    blk01 = jnp.where(same, 1.0, 0.0).astype(BF16)
    beta_b, gc_b, gl_b, gct_b = [], [], [], []
    for bi in range(nb):
        sm = sm_ref[bi]
        beta_b.append(jax.nn.sigmoid(sm))
        g_all = vec_ref[0:1, :] * jax.nn.softplus(sm + vec_ref[1:2, :])
        gc_b.append(_dot_exact_lhs(tri01, g_all))
        gl_b.append(_dot_exact_lhs(blk01, g_all))
        gct_b.append(gc_b[bi].T)

    heads = range(nb * GDN_HEADS)
    eye = jnp.where(row == col, 1.0, 0.0).astype(F32)
    q_l, k_l, vb_l, kb_l, gc_l, gl_l, decay_l, p_l, x_l = [], [], [], [], [], [], [], [], []
    for n in heads:
        bi, h = divmod(n, GDN_HEADS)
        beta_all, gc_all, gl_all, gc_all_t = beta_b[bi], gc_b[bi], gl_b[bi], gct_b[bi]
        la = LANE_GDN_A + h
        q = xc_ref[bi, :, h * GDN_DH:(h + 1) * GDN_DH]
        k = xc_ref[bi, :, GDN_WIDTH + h * GDN_DH:GDN_WIDTH + (h + 1) * GDN_DH]
        v = xc_ref[bi, :, 2 * GDN_WIDTH + h * GDN_DH:2 * GDN_WIDTH + (h + 1) * GDN_DH]
        q = q * lax.rsqrt(jnp.sum(q * q, axis=-1, keepdims=True) + EPS) * (GDN_DH ** -0.5)
        k = k * lax.rsqrt(jnp.sum(k * k, axis=-1, keepdims=True) + EPS)
        beta = beta_all[:, LANE_GDN_B + h:LANE_GDN_B + h + 1]
        gc = gc_all[:, la:la + 1]
        decay = jnp.where(causal, jnp.exp(gc - gc_all_t[la:la + 1, :]), 0.0)
        kb = k * beta
        m = jnp.where(strict, _bdot_nt(kb, k) * decay, 0.0)
        q_l.append(q), k_l.append(k), vb_l.append(v * beta), kb_l.append(kb)
        gc_l.append(gc), gl_l.append(gl_all[:, la:la + 1]), decay_l.append(decay)
        p_l.append(m), x_l.append(eye - m)

    span = 2
    while span < c:
        for h in heads:
            p_l[h] = _bdot(p_l[h], p_l[h])
        for h in heads:
            x_l[h] = x_l[h] + _bdot(x_l[h], p_l[h])
        span *= 2

    u_l, w_l, intra_l, qd_l, kd_l, ge_l = [], [], [], [], [], []
    for h in heads:
        eg = jnp.exp(gc_l[h])
        sol = _bdot(x_l[h], jnp.concatenate([vb_l[h], kb_l[h] * eg], axis=1))
        u_l.append(sol[:, :GDN_DH]), w_l.append(sol[:, GDN_DH:])
        intra_l.append((_bdot_nt(q_l[h], k_l[h]) * decay_l[h]).astype(BF16))
        qd_l.append(q_l[h] * eg)
        kd_l.append(k_l[h] * jnp.exp(gl_l[h] - gc_l[h]))
        ge_l.append(jnp.exp(gl_l[h]))

    st_l = [state_ref[h] for h in heads]
    out_l = [[] for _ in heads]
    for ci in range(rows // c):
        rs = slice(ci * c, (ci + 1) * c)
        for h in heads:
            st = st_l[h]
            v_new = u_l[h][rs] - _bdot(w_l[h][rs], st)
            pieces = []
            if ci > 0:
                pieces.append(jnp.zeros((ci * c, GDN_DH), F32))
            pieces.append(v_new)
            if (ci + 1) * c < rows:
                pieces.append(jnp.zeros((rows - (ci + 1) * c, GDN_DH), F32))
            v_pad = jnp.concatenate(pieces, axis=0) if len(pieces) > 1 else v_new
            out_l[h].append(_bdot(qd_l[h][rs], st)
                            + jnp.dot(intra_l[h][rs], v_pad.astype(BF16), preferred_element_type=F32))
            st_l[h] = st * ge_l[h][ci * c:ci * c + 1, :] + _bdot_tn(kd_l[h][rs], v_new)

    for n in heads:
        bi, h = divmod(n, GDN_HEADS)
        hs = slice(h * GDN_DH, (h + 1) * GDN_DH)
        state_ref[n] = st_l[n]
        o = jnp.concatenate(out_l[n], axis=0)
        o = _rms(o, ng_ref[...]) * _silu(z_ref[bi, :, hs])
        o_ref[bi, :, hs] = o.astype(o_ref.dtype)


def _gdn(u3, conv_w, a_log, dt_bias, norm_g):
    b, s, _ = u3.shape
    cw = conv_w.T.astype(F32)
    vec = jnp.zeros((2, LANES), F32)
    vec = vec.at[0, LANE_GDN_A:LANE_GDN_A + GDN_HEADS].set(-jnp.exp(a_log.astype(F32)))
    vec = vec.at[1, LANE_GDN_A:LANE_GDN_A + GDN_HEADS].set(dt_bias.astype(F32))
    blk = GDN_BLOCK
    nb = 2 if b % 2 == 0 else 1
    return pl.pallas_call(
        _gdn_kernel,
        grid=(b // nb, s // blk),
        in_specs=[pl.BlockSpec((nb, blk, 3 * GDN_WIDTH), lambda i, j: (i, j, COL_GDN_QKV // (3 * GDN_WIDTH))),
                  pl.BlockSpec((nb, blk, GDN_WIDTH), lambda i, j: (i, j, COL_GDN_Z // GDN_WIDTH)),
                  pl.BlockSpec((nb, blk, LANES), lambda i, j: (i, j, COL_SMALL // LANES)),
                  pl.BlockSpec((4, 3 * GDN_WIDTH), lambda i, j: (0, 0)),
                  pl.BlockSpec((2, LANES), lambda i, j: (0, 0)),
                  pl.BlockSpec((1, GDN_DH), lambda i, j: (0, 0))],
        out_specs=pl.BlockSpec((nb, blk, GDN_WIDTH), lambda i, j: (i, j, 0)),
        out_shape=jax.ShapeDtypeStruct((b, s, GDN_WIDTH), BF16),
        scratch_shapes=[pltpu.VMEM((nb * GDN_HEADS, GDN_DH, GDN_DH), F32),
                        pltpu.VMEM((nb * 3, 8, GDN_WIDTH), F32),
                        pltpu.VMEM((blk + 8, GDN_WIDTH), F32),
                        pltpu.VMEM((nb, blk, 3 * GDN_WIDTH), F32)],
        compiler_params=_cp("parallel", "arbitrary"),
        name="gdn",
    )(u3, u3, u3, cw, vec, norm_g.reshape(1, GDN_DH).astype(F32))


def _ssd_kernel(z_ref, xbc_ref, sm_ref, cw_ref, cb_ref, vec_ref, dsk_ref, ng_ref, o_ref,
                state_ref, tail_ref, xbuf_ref, xc_ref, y_ref):
    rows = SEQ_BLOCK
    first = pl.program_id(1) == 0
    half = SSD_P

    @pl.when(first)
    def _():
        state_ref[...] = jnp.zeros_like(state_ref)

    slab = 512
    for s in range(SSD_CONV_DIM // slab):
        sl = slice(s * slab, (s + 1) * slab)
        xc_ref[:, sl] = _conv_silu_slab(xbc_ref[0, :, sl], first, tail_ref.at[s], xbuf_ref, cw_ref[:, sl], cb_ref[:, sl])

    dt_all = jax.nn.softplus(sm_ref[0] + vec_ref[1:2, :])
    da_all = dt_all * vec_ref[0:1, :]
    row = lax.broadcasted_iota(I32, (rows, rows), 0)
    col = lax.broadcasted_iota(I32, (rows, rows), 1)
    causal = col <= row
    tri01 = jnp.where(causal, 1.0, 0.0).astype(BF16)
    acs = _dot_exact_lhs(tri01, da_all)
    acs_t = acs.T
    lane = lax.broadcasted_iota(I32, (rows, LANES), 1)
    lo = lane < half
    lane1 = lax.broadcasted_iota(I32, (1, LANES), 1)
    lo1 = lane1 < half

    d = functools.partial(jnp.dot, preferred_element_type=F32)
    pairs_per_group = SSD_HEADS // SSD_GROUPS // 2
    n_pairs = SSD_HEADS // 2
    cb_l, bmt_l, cmb_l = [], [], []
    for g in range(SSD_GROUPS):
        bm = xc_ref[:, SSD_WIDTH + g * SSD_N:SSD_WIDTH + (g + 1) * SSD_N]
        cm = xc_ref[:, SSD_WIDTH + (SSD_GROUPS + g) * SSD_N:SSD_WIDTH + (SSD_GROUPS + g + 1) * SSD_N]
        cb_l.append(_bdot_nt(cm, bm))
        bmt_l.append(bm.T)
        cmb_l.append(cm.astype(BF16))

    att_l, dec_l, xdt_l, x_l, expa_l, ge_l = [], [], [], [], [], []
    for p in range(n_pairs):
        g = p // pairs_per_group
        la = LANE_SSD_DT + 2 * p
        lb = la + 1
        col_a, col_b = acs[:, la:la + 1], acs[:, lb:lb + 1]
        row_a, row_b = acs_t[la:la + 1, :], acs_t[lb:lb + 1, :]
        last_a, last_b = row_a[:, rows - 1:rows], row_b[:, rows - 1:rows]
        att_l.append(((cb_l[g] * jnp.where(causal, jnp.exp(col_a - row_a), 0.0)).astype(BF16),
                      (cb_l[g] * jnp.where(causal, jnp.exp(col_b - row_b), 0.0)).astype(BF16)))
        dec_l.append(((bmt_l[g] * jnp.exp(last_a - row_a)).astype(BF16),
                      (bmt_l[g] * jnp.exp(last_b - row_b)).astype(BF16)))
        x_pair = xc_ref[:, p * LANES:(p + 1) * LANES]
        xdt = x_pair * jnp.where(lo, dt_all[:, la:la + 1], dt_all[:, lb:lb + 1])
        xdt_l.append((jnp.where(lo, xdt, 0.0).astype(BF16), jnp.where(lo, 0.0, xdt).astype(BF16)))
        x_l.append(x_pair)
        expa_l.append(jnp.where(lo, jnp.exp(col_a), jnp.exp(col_b)))
        ge_l.append(jnp.where(lo1, jnp.exp(last_a), jnp.exp(last_b)))

    for p in range(n_pairs):
        g = p // pairs_per_group
        ps = slice(p * LANES, (p + 1) * LANES)
        xdt_a, xdt_b = xdt_l[p]
        y_diag = d(att_l[p][0], xdt_a) + d(att_l[p][1], xdt_b)
        st_new = d(dec_l[p][0], xdt_a) + d(dec_l[p][1], xdt_b)
        prev = state_ref[p]
        y_off = d(cmb_l[g], prev.astype(BF16)) * expa_l[p]
        state_ref[p] = prev * ge_l[p] + st_new
        y_ref[:, ps] = y_diag + y_off + x_l[p] * dsk_ref[:, ps]

    gw = SSD_WIDTH // SSD_GROUPS
    for g in range(SSD_GROUPS):
        gs = slice(g * gw, (g + 1) * gw)
        y = y_ref[:, gs] * _silu(z_ref[0, :, gs])
        o_ref[0, :, gs] = _rms(y, ng_ref[:, gs]).astype(o_ref.dtype)


def _ssd(u3, conv_w, conv_b, a_log, dt_bias, d_skip, norm_g):
    b, s, _ = u3.shape
    cw = conv_w.T.astype(F32)
    vec = jnp.zeros((2, LANES), F32)
    vec = vec.at[0, LANE_SSD_DT:LANE_SSD_DT + SSD_HEADS].set(-jnp.exp(a_log.astype(F32)))
    vec = vec.at[1, LANE_SSD_DT:LANE_SSD_DT + SSD_HEADS].set(dt_bias.astype(F32))
    dsk = jnp.repeat(d_skip.astype(F32), SSD_P).reshape(1, SSD_WIDTH)
    blk = SEQ_BLOCK
    return pl.pallas_call(
        _ssd_kernel,
        grid=(b, s // blk),
        in_specs=[pl.BlockSpec((1, blk, SSD_WIDTH), lambda i, j: (i, j, COL_SSD_Z // SSD_WIDTH)),
                  pl.BlockSpec((1, blk, SSD_CONV_DIM), lambda i, j: (i, j, COL_SSD_XBC // SSD_CONV_DIM)),
                  pl.BlockSpec((1, blk, LANES), lambda i, j: (i, j, COL_SMALL // LANES)),
                  pl.BlockSpec((4, SSD_CONV_DIM), lambda i, j: (0, 0)),
                  pl.BlockSpec((1, SSD_CONV_DIM), lambda i, j: (0, 0)),
                  pl.BlockSpec((2, LANES), lambda i, j: (0, 0)),
                  pl.BlockSpec((1, SSD_WIDTH), lambda i, j: (0, 0)),
                  pl.BlockSpec((1, SSD_WIDTH), lambda i, j: (0, 0))],
        out_specs=pl.BlockSpec((1, blk, SSD_WIDTH), lambda i, j: (i, j, 0)),
        out_shape=jax.ShapeDtypeStruct((b, s, SSD_WIDTH), BF16),
        scratch_shapes=[pltpu.VMEM((SSD_HEADS // 2, SSD_N, 2 * SSD_P), F32),
                        pltpu.VMEM((SSD_CONV_DIM // 512, 8, 512), F32),
                        pltpu.VMEM((blk + 8, 512), F32),
                        pltpu.VMEM((blk, SSD_CONV_DIM), F32),
                        pltpu.VMEM((blk, SSD_WIDTH), F32)],
        compiler_params=_cp("parallel", "arbitrary"),
        name="ssd",
    )(u3, u3, u3, cw, conv_b.reshape(1, SSD_CONV_DIM).astype(F32), vec, dsk,
      norm_g.reshape(1, SSD_WIDTH).astype(F32))


def _mla_pre_kernel(ql_ref, kpe_ref, kvl_ref, pos_ref, qag_ref, kvag_ref, wq_ref, wkv_ref, vec_ref,
                    q_ref, k_ref, v_ref):
    qn = _rms(ql_ref[0], qag_ref[...])
    qall = jnp.dot(qn.astype(BF16), wq_ref[...], preferred_element_type=F32)
    kvn = _rms(kvl_ref[0], kvag_ref[...])
    kv = jnp.dot(kvn.astype(BF16), wkv_ref[...], preferred_element_type=F32)
    kpe = kpe_ref[0]
    ang = pos_ref[0] * vec_ref[6:7, :]
    cos = jnp.cos(ang)
    sin = jnp.sin(ang) * vec_ref[7:8, :]
    gqn, gqa, gqb = vec_ref[0:1, :], vec_ref[1:2, :], vec_ref[2:3, :]
    gkn, gka, gkb = vec_ref[3:4, :], vec_ref[4:5, :], vec_ref[5:6, :]
    scale = MLA_QK ** -0.5 * LOG2_E
    k_rot = kpe * (cos * gka) + pltpu.roll(kpe, 64, 1) * (sin * gkb)
    k_pe_ss = 0.5 * jnp.sum(kpe * kpe, axis=-1, keepdims=True)
    nh = MLA_HEADS
    for h in range(nh):
        q_nope = qall[:, h * LANES:(h + 1) * LANES]
        q_pe = qall[:, (nh + h) * LANES:(nh + h + 1) * LANES]
        ss = jnp.sum(q_nope * q_nope, axis=-1, keepdims=True) + 0.5 * jnp.sum(q_pe * q_pe, axis=-1, keepdims=True)
        rstd = lax.rsqrt(ss * (1.0 / MLA_QK) + EPS)
        q_rot = q_pe * (cos * gqa) + pltpu.roll(q_pe, 64, 1) * (sin * gqb)
        q_ref[0, h, :, 0:LANES] = (q_nope * gqn * rstd * scale).astype(q_ref.dtype)
        q_ref[0, h, :, LANES:2 * LANES] = (q_rot * rstd * (0.5 * scale)).astype(q_ref.dtype)
        k_nope = kv[:, 2 * h * LANES:(2 * h + 1) * LANES]
        ssk = jnp.sum(k_nope * k_nope, axis=-1, keepdims=True) + k_pe_ss
        rstdk = lax.rsqrt(ssk * (1.0 / MLA_QK) + EPS)
        k_ref[0, h, :, 0:LANES] = (k_nope * gkn * rstdk).astype(k_ref.dtype)
        k_ref[0, h, :, LANES:2 * LANES] = (k_rot * rstdk).astype(k_ref.dtype)
        v_ref[0, h, :, 0:LANES] = kv[:, (2 * h + 1) * LANES:(2 * h + 2) * LANES].astype(v_ref.dtype)
        v_ref[0, h, :, LANES:2 * LANES] = jnp.ones((kv.shape[0], LANES), v_ref.dtype)


def _rope_pair_gains(g):
    g1, g2 = g[MLA_NOPE:MLA_NOPE + 32], g[MLA_NOPE + 32:MLA_NOPE + 64]
    return jnp.concatenate([g1, g2, g2, g1]), jnp.concatenate([g2, g1, g1, g2])


def _mla_pre(u3, pos_f, q_a_g, w_q_b, kv_a_g, w_kv_b, q_norm_g, k_norm_g, *, ts):
    b, s, _ = u3.shape
    nh = MLA_HEADS
    wq = w_q_b.reshape(MLA_Q_RANK, nh, MLA_QK)
    x1, x2 = wq[:, :, MLA_NOPE:MLA_NOPE + 32], wq[:, :, MLA_NOPE + 32:]
    wq_all = jnp.concatenate([wq[:, :, :MLA_NOPE].reshape(MLA_Q_RANK, nh * MLA_NOPE),
                              jnp.concatenate([x1, x2, x2, x1], axis=-1).reshape(MLA_Q_RANK, nh * LANES)],
                             axis=1).astype(BF16)
    half = MLA_ROPE // 2
    inv_freq = 1.0 / (ROPE_THETA ** (jnp.arange(half, dtype=F32) / half))
    gqa, gqb = _rope_pair_gains(q_norm_g.astype(F32))
    gka, gkb = _rope_pair_gains(k_norm_g.astype(F32))
    ones = jnp.ones((half,), F32)
    vec = jnp.stack([q_norm_g[:MLA_NOPE].astype(F32), gqa, gqb, k_norm_g[:MLA_NOPE].astype(F32), gka, gkb,
                     jnp.tile(inv_freq, 4), jnp.concatenate([-ones, ones, ones, -ones])])
    ts = min(ts, s)
    qk_shape = jax.ShapeDtypeStruct((b, nh, s, 2 * LANES), BF16)
    return pl.pallas_call(
        _mla_pre_kernel,
        grid=(b, s // ts),
        in_specs=[pl.BlockSpec((1, ts, MLA_Q_RANK), lambda i, j: (i, j, COL_QLAT // MLA_Q_RANK)),
                  pl.BlockSpec((1, ts, LANES), lambda i, j: (i, j, COL_KPE // LANES)),
                  pl.BlockSpec((1, ts, MLA_KV_RANK), lambda i, j: (i, j, COL_KVLAT // MLA_KV_RANK)),
                  pl.BlockSpec((1, ts, 1), lambda i, j: (i, j, 0)),
                  pl.BlockSpec((1, MLA_Q_RANK), lambda i, j: (0, 0)),
                  pl.BlockSpec((1, MLA_KV_RANK), lambda i, j: (0, 0)),
                  pl.BlockSpec((MLA_Q_RANK, 2 * nh * LANES), lambda i, j: (0, 0)),
                  pl.BlockSpec((MLA_KV_RANK, 2 * nh * LANES), lambda i, j: (0, 0)),
                  pl.BlockSpec((8, LANES), lambda i, j: (0, 0))],
        out_specs=[pl.BlockSpec((1, nh, ts, 2 * LANES), lambda i, j: (i, 0, j, 0)),
                   pl.BlockSpec((1, nh, ts, 2 * LANES), lambda i, j: (i, 0, j, 0)),
                   pl.BlockSpec((1, nh, ts, 2 * LANES), lambda i, j: (i, 0, j, 0))],
        out_shape=[qk_shape, qk_shape, qk_shape],
        compiler_params=_cp("parallel", "parallel"),
        name="mla_pre",
    )(u3, u3, u3, pos_f, q_a_g.reshape(1, -1).astype(F32), kv_a_g.reshape(1, -1).astype(F32),
      wq_all, w_kv_b.astype(BF16), vec)


def _flash_kernel(qi_ref, kj_ref, q_ref, k_ref, v_ref, o_ref, m_ref, acc_ref, *, t):
    qi = qi_ref[pl.program_id(1)]
    kj = kj_ref[pl.program_id(1)]
    nh = q_ref.shape[1]

    @pl.when(kj == 0)
    def _():
        m_ref[...] = jnp.full_like(m_ref, -jnp.inf)
        acc_ref[...] = jnp.zeros_like(acc_ref)

    def step(diagonal):
        s_l = [lax.dot_general(q_ref[0, h], k_ref[0, h], (((1,), (1,)), ((), ())), preferred_element_type=F32)
               for h in range(nh)]
        if diagonal:
            keep = lax.broadcasted_iota(I32, (t, t), 0) >= lax.broadcasted_iota(I32, (t, t), 1)
            s_l = [jnp.where(keep, s, -jnp.inf) for s in s_l]
        p_l, alpha_l = [], []
        for h in range(nh):
            m_old = m_ref[h]
            m_new = jnp.maximum(m_old, jnp.max(s_l[h], axis=-1, keepdims=True))
            p_l.append(jnp.exp2(s_l[h] - jnp.tile(m_new, (1, t // LANES))).astype(BF16))
            alpha_l.append(jnp.exp2(m_old - m_new))
            m_ref[h] = m_new
        for h in range(nh):
            pv = jnp.dot(p_l[h], v_ref[0, h], preferred_element_type=F32)
            acc_ref[h] = jnp.tile(alpha_l[h], (1, 2)) * acc_ref[h] + pv

    @pl.when(kj < qi)
    def _():
        step(False)

    @pl.when(kj == qi)
    def _():
        step(True)
        for h in range(nh):
            acc = acc_ref[h]
            o_ref[0, :, h * LANES:(h + 1) * LANES] = (acc[:, :LANES] / acc[:, LANES:]).astype(o_ref.dtype)


def _flash(q, k, v, *, t):
    b, nh, s, dq = q.shape
    t = min(t, s)
    n = s // t
    pairs = [(qi, kj) for qi in range(n) for kj in range(qi + 1)]
    qi_tab = jnp.asarray([pr[0] for pr in pairs], I32)
    kj_tab = jnp.asarray([pr[1] for pr in pairs], I32)
    kv_spec = pl.BlockSpec((1, nh, t, dq), lambda i, pr, qt, kt: (i, 0, kt[pr], 0))
    grid_spec = pltpu.PrefetchScalarGridSpec(
        num_scalar_prefetch=2,
        grid=(b, len(pairs)),
        in_specs=[pl.BlockSpec((1, nh, t, dq), lambda i, pr, qt, kt: (i, 0, qt[pr], 0)), kv_spec, kv_spec],
        out_specs=pl.BlockSpec((1, t, nh * LANES), lambda i, pr, qt, kt: (i, qt[pr], 0)),
        scratch_shapes=[pltpu.VMEM((nh, t, LANES), F32), pltpu.VMEM((nh, t, 2 * LANES), F32)],
    )
    return pl.pallas_call(
        functools.partial(_flash_kernel, t=t),
        grid_spec=grid_spec,
        out_shape=jax.ShapeDtypeStruct((b, s, nh * LANES), BF16),
        compiler_params=_cp("parallel", "arbitrary"),
        name="mla_flash",
    )(qi_tab, kj_tab, q, k, v)


def _router_kernel(h_ref, g_ref, wr_ref, idx_ref, wt_ref):
    xn = _rms(h_ref[...], g_ref[...])
    w = wr_ref[...]
    xh, wh = xn.astype(BF16), w.astype(BF16)
    xl, wl = (xn - xh.astype(F32)).astype(BF16), (w - wh.astype(F32)).astype(BF16)
    d = functools.partial(jnp.dot, preferred_element_type=F32)
    logits = d(xh, wh) + d(xh, wl) + d(xl, wh)
    lane = lax.broadcasted_iota(I32, logits.shape, 1)
    logits = jnp.where(lane < N_EXPERTS, logits, -jnp.inf)
    m1 = jnp.max(logits, axis=-1, keepdims=True)
    i1 = jnp.min(jnp.where(logits == m1, lane, LANES), axis=-1, keepdims=True)
    rest = jnp.where(lane == i1, -jnp.inf, logits)
    m2 = jnp.max(rest, axis=-1, keepdims=True)
    i2 = jnp.min(jnp.where(rest == m2, lane, LANES), axis=-1, keepdims=True)
    e2 = jnp.exp(m2 - m1)
    w1 = 1.0 / (1.0 + e2)
    w2 = e2 / (1.0 + e2)
    idx_ref[...] = jnp.where(lane == 0, i1, jnp.where(lane == 1, i2, 0))
    wt_ref[...] = jnp.where(lane == 0, w1, jnp.where(lane == 1, w2, 0.0))


def _router(h, gain, w_router, *, tm):
    m, d = h.shape
    tm = min(tm, m)
    wr = jnp.zeros((d, LANES), F32).at[:, :N_EXPERTS].set(w_router.astype(F32))
    return pl.pallas_call(
        _router_kernel,
        grid=(m // tm,),
        in_specs=[pl.BlockSpec((tm, d), lambda i: (i, 0)),
                  pl.BlockSpec((1, d), lambda i: (0, 0)),
                  pl.BlockSpec((d, LANES), lambda i: (0, 0))],
        out_specs=[pl.BlockSpec((tm, LANES), lambda i: (i, 0)),
                   pl.BlockSpec((tm, LANES), lambda i: (i, 0))],
        out_shape=[jax.ShapeDtypeStruct((m, LANES), I32),
                   jax.ShapeDtypeStruct((m, LANES), F32)],
        compiler_params=_cp("parallel"),
        name="moe_router",
    )(h, gain.reshape(1, d).astype(F32), wr)


def _row_copy(src_ref, t, dst_ref, r, sem):
    return pltpu.make_async_copy(src_ref.at[pl.ds(t, 1), :], dst_ref.at[pl.ds(r, 1), :], sem)


def _gather_kernel(idx_ref, nu_ref, src_ref, g_ref, o_ref, buf_ref, sem):
    rows = buf_ref.shape[1]
    i = pl.program_id(0)
    n = nu_ref[0]

    def start_step(step, slot):
        def issue(r2, c):
            for k in range(2):
                r = 2 * r2 + k
                _row_copy(src_ref, idx_ref[step * rows + r], buf_ref.at[slot], r, sem.at[slot]).start(priority=k)
            return c
        lax.fori_loop(0, rows // 2, issue, 0, unroll=4)

    @pl.when(i == 0)
    def _():
        start_step(0, 0)

    @pl.when(i + 1 < n)
    def _():
        start_step(i + 1, (i + 1) % 2)

    slot = i % 2

    @pl.when(i < n)
    def _():
        def wait(r, c):
            _row_copy(src_ref, 0, buf_ref.at[slot], r, sem.at[slot]).wait()
            return c

        lax.fori_loop(0, rows, wait, 0, unroll=8)
        o_ref[...] = _rms(buf_ref[slot], g_ref[...]).astype(o_ref.dtype)

    @pl.when(i >= n)
    def _():
        o_ref[...] = jnp.zeros_like(o_ref)


def _gather_norm_rows(src_tok, n_used, h, gain):
    p = src_tok.shape[0]
    d = h.shape[1]
    rows = MOE_TM
    grid_spec = pltpu.PrefetchScalarGridSpec(
        num_scalar_prefetch=2,
        grid=(p // rows,),
        in_specs=[pl.BlockSpec(memory_space=pl.ANY), pl.BlockSpec((1, d), lambda i, idx, nu: (0, 0))],
        out_specs=pl.BlockSpec((rows, d), lambda i, idx, nu: (i, 0)),
        scratch_shapes=[pltpu.VMEM((2, rows, d), F32), pltpu.SemaphoreType.DMA((2,))],
    )
    return pl.pallas_call(
        _gather_kernel,
        grid_spec=grid_spec,
        out_shape=jax.ShapeDtypeStruct((p, d), BF16),
        compiler_params=_cp("arbitrary"),
        name="moe_gather",
    )(src_tok, n_used, h, gain.reshape(1, d).astype(F32))


def _gup_kernel(te_ref, nu_ref, x_ref, wg_ref, wu_ref, wd_ref, o_ref, wdb_ref, wgb_ref, wub_ref):
    i = pl.program_id(1)
    used = i < nu_ref[0]

    wdb_ref[...] = wd_ref[...].astype(BF16)

    changed = (i == 0) | (te_ref[i] != te_ref[jnp.maximum(i - 1, 0)])

    def compute(convert):
        x = x_ref[...]
        for c in range(0, o_ref.shape[1], MXU_COLS):
            cs = slice(c, c + MXU_COLS)
            if convert:
                wgb_ref[:, cs] = wg_ref[0, :, cs].astype(BF16)
                wub_ref[:, cs] = wu_ref[0, :, cs].astype(BF16)
            gate = jnp.dot(x, wgb_ref[:, cs], preferred_element_type=F32)
            up = jnp.dot(x, wub_ref[:, cs], preferred_element_type=F32)
            o_ref[:, cs] = (_silu(gate) * up).astype(o_ref.dtype)

    @pl.when(changed)
    def _():
        compute(True)

    @pl.when(used & jnp.logical_not(changed))
    def _():
        compute(False)

    @pl.when(jnp.logical_not(used))
    def _():
        o_ref[...] = jnp.zeros_like(o_ref)


def _grouped_up(tile_expert, n_used, xs, wg, wu, wd, *, tm, tf):
    p, d = xs.shape
    f = wg.shape[2]
    n_i = p // tm
    n_steps = (f // tf) * n_i
    wd2 = wd.reshape(-1, wd.shape[-1])
    slab = _slab_rows(wd2.shape[0], n_steps)
    n_slabs = wd2.shape[0] // slab

    def slab_map(j, i, te, nu):
        return (jnp.minimum(j * n_i + i, n_slabs - 1), 0)

    grid_spec = pltpu.PrefetchScalarGridSpec(
        num_scalar_prefetch=2,
        grid=(f // tf, n_i),
        in_specs=[pl.BlockSpec((tm, d), lambda j, i, te, nu: (jnp.minimum(i, nu[0] - 1), 0)),
                  pl.BlockSpec((1, d, tf), lambda j, i, te, nu: (te[i], 0, j)),
                  pl.BlockSpec((1, d, tf), lambda j, i, te, nu: (te[i], 0, j)),
                  pl.BlockSpec((slab, wd2.shape[1]), slab_map)],
        out_specs=[pl.BlockSpec((tm, tf), lambda j, i, te, nu: (i, j)),
                   pl.BlockSpec((slab, wd2.shape[1]), slab_map)],
        scratch_shapes=[pltpu.VMEM((d, tf), BF16), pltpu.VMEM((d, tf), BF16)],
    )
    hff, wdb = pl.pallas_call(
        _gup_kernel,
        grid_spec=grid_spec,
        out_shape=[jax.ShapeDtypeStruct((p, f), BF16), jax.ShapeDtypeStruct(wd2.shape, BF16)],
        compiler_params=_cp("arbitrary", "arbitrary"),
        name="moe_up",
    )(tile_expert, n_used, xs, wg, wu, wd2)
    return hff, wdb.reshape(wd.shape)


def _gdown_kernel(te_ref, nu_ref, x_ref, wd_ref, o_ref):
    used = pl.program_id(1) < nu_ref[0]

    @pl.when(used)
    def _():
        o_ref[...] = jnp.dot(x_ref[...], wd_ref[0], preferred_element_type=F32)

    @pl.when(jnp.logical_not(used))
    def _():
        o_ref[...] = jnp.zeros_like(o_ref)


def _grouped_down(tile_expert, n_used, hff, wd, *, tm, tn):
    p, f = hff.shape
    d = wd.shape[2]
    grid_spec = pltpu.PrefetchScalarGridSpec(
        num_scalar_prefetch=2,
        grid=(d // tn, p // tm),
        in_specs=[pl.BlockSpec((tm, f), lambda j, i, te, nu: (jnp.minimum(i, nu[0] - 1), 0)),
                  pl.BlockSpec((1, f, tn), lambda j, i, te, nu: (te[i], 0, j))],
        out_specs=pl.BlockSpec((tm, tn), lambda j, i, te, nu: (i, j)),
    )
    return pl.pallas_call(
        _gdown_kernel,
        grid_spec=grid_spec,
        out_shape=jax.ShapeDtypeStruct((p, d), F32),
        compiler_params=_cp("arbitrary", "arbitrary"),
        name="moe_down",
    )(tile_expert, n_used, hff, wd)


def _combine_kernel(pos_ref, h_ref, wt_ref, y_ref, o_ref, buf_ref, sem):
    rows = h_ref.shape[0]
    i = pl.program_id(0)
    n = pl.num_programs(0)

    def start_step(step, slot):
        def issue(r, c):
            base = 2 * (step * rows + r)
            _row_copy(y_ref, pos_ref[base], buf_ref.at[slot, 0], r, sem.at[slot]).start(priority=0)
            _row_copy(y_ref, pos_ref[base + 1], buf_ref.at[slot, 1], r, sem.at[slot]).start(priority=1)
            return c
        lax.fori_loop(0, rows, issue, 0, unroll=4)

    @pl.when(i == 0)
    def _():
        start_step(0, 0)

    @pl.when(i + 1 < n)
    def _():
        start_step(i + 1, (i + 1) % 2)

    slot = i % 2

    def wait(r, c):
        _row_copy(y_ref, 0, buf_ref.at[slot, 0], r, sem.at[slot]).wait()
        _row_copy(y_ref, 0, buf_ref.at[slot, 1], r, sem.at[slot]).wait()
        return c

    lax.fori_loop(0, rows, wait, 0, unroll=4)
    wt = wt_ref[...]
    o_ref[...] = h_ref[...] + wt[:, 0:1] * buf_ref[slot, 0] + wt[:, 1:2] * buf_ref[slot, 1]


def _combine(h, y, pos, wts):
    m, d = h.shape
    rows = min(COMBINE_ROWS, m)
    grid_spec = pltpu.PrefetchScalarGridSpec(
        num_scalar_prefetch=1,
        grid=(m // rows,),
        in_specs=[pl.BlockSpec((rows, d), lambda i, pos: (i, 0)),
                  pl.BlockSpec((rows, LANES), lambda i, pos: (i, 0)),
                  pl.BlockSpec(memory_space=pl.ANY)],
        out_specs=pl.BlockSpec((rows, d), lambda i, pos: (i, 0)),
        scratch_shapes=[pltpu.VMEM((2, 2, rows, d), F32), pltpu.SemaphoreType.DMA((2,))],
    )
    return pl.pallas_call(
        _combine_kernel,
        grid_spec=grid_spec,
        out_shape=jax.ShapeDtypeStruct((m, d), F32),
        compiler_params=_cp("arbitrary"),
        name="moe_combine",
    )(pos, h, wts, y)


def _moe(h, gain, w_router, wg, wu, wd):
    t, d = h.shape
    tm = MOE_TM
    n2 = 2 * t
    idx, wts = _router(h, gain, w_router, tm=512)
    flat_e = idx[:, :2].reshape(-1)
    onehot = (flat_e[:, None] == jnp.arange(N_EXPERTS, dtype=I32)[None, :]).astype(I32)
    csum = jnp.cumsum(onehot, axis=0)
    rank = jnp.sum((csum - onehot) * onehot, axis=1)
    counts = csum[-1]
    padded = ((counts + tm - 1) // tm) * tm
    ends = jnp.cumsum(padded)
    starts = ends - padded
    pos = (starts[flat_e] + rank).astype(I32)
    p_rows = n2 + N_EXPERTS * tm
    n_tiles = p_rows // tm
    n_used = (ends[-1] // tm).astype(I32).reshape(1)
    tile_start = jnp.arange(n_tiles, dtype=I32) * tm
    tile_expert = jnp.sum((tile_start[:, None] >= ends[None, :]).astype(I32), axis=1)
    last_expert = jnp.sum((ends[-1] - 1 >= ends).astype(I32))
    tile_expert = jnp.minimum(tile_expert, last_expert).astype(I32)
    order = jnp.sort(flat_e * n2 + jnp.arange(n2, dtype=I32)) % n2
    row = jnp.arange(p_rows, dtype=I32)
    row_e = jnp.repeat(tile_expert, tm)
    local = row - starts[row_e]
    first = (jnp.cumsum(counts) - counts)[row_e]
    src_tok = jnp.where(local < counts[row_e], order[jnp.clip(first + local, 0, n2 - 1)] // 2, row % t).astype(I32)

    xs = _gather_norm_rows(src_tok, n_used, h, gain)
    hff, wd_bf16 = _grouped_up(tile_expert, n_used, xs, wg, wu, wd, tm=tm, tf=1024)
    y = _grouped_down(tile_expert, n_used, hff, wd_bf16, tm=tm, tn=1024)
    return _combine(h, y, pos, wts)


def _w_in_moves():
    gdn0, ssd0 = 0, 4 * GDN_WIDTH + 2 * GDN_HEADS
    mla0 = ssd0 + SSD_WIDTH + SSD_CONV_DIM + SSD_HEADS
    kpe0 = mla0 + MLA_Q_RANK + MLA_KV_RANK
    gb0 = gdn0 + 4 * GDN_WIDTH
    dt0 = ssd0 + SSD_WIDTH + SSD_CONV_DIM
    half = MLA_ROPE // 2
    moves = [(COL_GDN_QKV, gdn0, 4 * GDN_WIDTH),
             (COL_SSD_Z, ssd0, SSD_WIDTH + SSD_CONV_DIM),
             (COL_QLAT, mla0, MLA_Q_RANK),
             (COL_KPE, kpe0, MLA_ROPE), (COL_KPE + MLA_ROPE, kpe0 + half, half),
             (COL_KPE + MLA_ROPE + half, kpe0, half),
             (COL_KVLAT, mla0 + MLA_Q_RANK, MLA_KV_RANK),
             (COL_SMALL + LANE_GDN_B, gb0, 2 * GDN_HEADS), (COL_SMALL + LANE_SSD_DT, dt0, SSD_HEADS)]
    small_end = LANE_SSD_DT + SSD_HEADS
    zeros = [(COL_SMALL, LANE_GDN_B), (COL_SMALL + small_end, LANES - small_end), (COL_SMALL + LANES, LANES)]
    assert LANE_GDN_A == LANE_GDN_B + GDN_HEADS and LANE_SSD_DT == LANE_GDN_A + GDN_HEADS
    assert sum(m[2] for m in moves) + sum(z[1] for z in zeros) == U_COLS
    return moves, zeros


def _w_in_kernel(w_ref, o_ref):
    moves, zeros = _w_in_moves()
    cols = o_ref.shape[1]
    small_lo, small_hi = COL_SMALL, COL_SMALL + LANES
    pieces = {}
    for dst, src, width in moves:
        if small_lo <= dst < small_hi:
            pieces[dst] = w_ref[0, src:src + width, :]
        else:
            o_ref[dst:dst + width, :] = w_ref[0, src:src + width, :].astype(o_ref.dtype)
    for dst, width in zeros:
        if small_lo <= dst < small_hi:
            pieces[dst] = jnp.zeros((width, cols), F32)
        else:
            o_ref[dst:dst + width, :] = jnp.zeros((width, cols), o_ref.dtype)
    small = jnp.concatenate([pieces[k] for k in sorted(pieces)], axis=0)
    o_ref[small_lo:small_hi, :] = small.astype(o_ref.dtype)


def _rearranged_w_in_t(w_in_all, layer):
    _, d, n = w_in_all.shape
    w_t = jnp.transpose(w_in_all, (0, 2, 1))
    tc = min(512, d)
    return pl.pallas_call(
        _w_in_kernel,
        grid=(d // tc,),
        in_specs=[pl.BlockSpec((1, n, tc), lambda i: (layer, 0, i))],
        out_specs=pl.BlockSpec((U_COLS, tc), lambda i: (0, i)),
        out_shape=jax.ShapeDtypeStruct((U_COLS, d), BF16),
        compiler_params=_cp("parallel"),
        name="w_in_prep",
    )(w_t)


def kernel(x, p, positions, norm_mix_g, w_in, w_out, gdn_conv_w, gdn_a_log, gdn_dt_bias, gdn_norm_g, ssd_conv_w, ssd_conv_b, ssd_a_log, ssd_dt_bias, ssd_d, ssd_norm_g, mla_q_a_g, mla_w_q_b, mla_kv_a_g, mla_w_kv_b, mla_q_norm_g, mla_k_norm_g, norm_ffn_g, ffn_w_gate, ffn_w_up, ffn_w_down, router_w, moe_w_gate, moe_w_up, moe_w_down, ple_w_proj, ple_w_gate, ple_norm_g):
    b, s, d = x.shape
    t = b * s
    depth = w_in.shape[0]
    h = x.reshape(t, d).astype(F32)
    pos_f = positions.astype(F32).reshape(b, s, 1)
    p_all = p.reshape(depth, t, p.shape[-1])
    d_ff = ffn_w_gate.shape[-1]
    w_out2, ple_gate2 = w_out.reshape(depth * d, d), ple_w_gate.reshape(depth * d, d)
    xn = None
    for i in range(depth):
        side = [(w_out2, i * d, d), (ple_gate2, i * d, d)]
        if i % 2 == 0:
            side += [(ffn_w_gate.reshape(-1, d_ff), (i // 2) * d, d), (ffn_w_up.reshape(-1, d_ff), (i // 2) * d, d),
                     (ffn_w_down.reshape(-1, d), (i // 2) * d_ff, d_ff)]
        if xn is None:
            proj_in = dict(x=h, gain=norm_mix_g[i])
        else:
            proj_in = dict(x=xn)
        u, w_out_b, ple_gate_b, *ffn_b = _mm(w=_rearranged_w_in_t(w_in, i), tm=1024, tn=1408, name="in_proj",
                                             side=side, w_transposed=True, **proj_in)
        u3 = u.reshape(b, s, U_COLS)
        o_gdn = _gdn(u3, gdn_conv_w[i], gdn_a_log[i], gdn_dt_bias[i], gdn_norm_g[i])
        o_ssd = _ssd(u3, ssd_conv_w[i], ssd_conv_b[i], ssd_a_log[i], ssd_dt_bias[i], ssd_d[i], ssd_norm_g[i])
        q, k, v = _mla_pre(u3, pos_f, mla_q_a_g[i], mla_w_q_b[i], mla_kv_a_g[i], mla_w_kv_b[i],
                           mla_q_norm_g[i], mla_k_norm_g[i], ts=512)
        o_mla = _flash(q, k, v, t=512)
        mix_parts = [o.reshape(t, o.shape[-1]) for o in (o_gdn, o_ssd, o_mla)]
        j = i // 2
        if i % 2 == 0:
            h, hn = _mm_parts(mix_parts, w_out_b, h, tm=512, tn=d, name="out_proj", next_gain=norm_ffn_g[i])
            ff = _swiglu_up(hn, ffn_b[0], ffn_b[1], tm=1024, tn=512)
            h = _mm(ff, ffn_b[2], res=h, tm=1024, tn=512, name="ffn_down")
        else:
            h, _ = _mm_parts(mix_parts, w_out_b, h, tm=1024, tn=1024, name="out_proj")
            h = _moe(h, norm_ffn_g[i], router_w[j], moe_w_gate[j], moe_w_up[j], moe_w_down[j])
        h, xn = _ple(h, p_all, i, ple_w_proj[i].astype(BF16), ple_norm_g[i], ple_gate_b, tm=512,
                     next_gain=norm_mix_g[i + 1] if i + 1 < depth else None)
    return h.reshape(b, s, d).astype(x.dtype)
```
